```python
import math
import jax, jax.numpy as jnp
from jax import lax
import numpy as np

D_MODEL = 1024
BATCH = 8
SEQ = 4096
DEPTH = 1

DA_HEADS = 4
DA_QK_DIM = 64
DA_V_DIM = 2 * DA_QK_DIM
DA_WIDTH = DA_HEADS * DA_V_DIM
ML_HEADS = 4
ML_DIM = 128
ML_WIDTH = ML_HEADS * ML_DIM
MIX_WIDTH = DA_WIDTH + ML_WIDTH
ROPE_THETA = 500000.0
ROPE_DIM = DA_QK_DIM // 4
D_FF = 2816
CONV_K = 4
CHUNK = 64
Q_BLOCK = 128
EPS = 1e-6
IN_SIZES = (2 * DA_HEADS * DA_QK_DIM,
            2 * DA_HEADS * DA_QK_DIM,
            DA_WIDTH,
            2 * ML_WIDTH,
            ML_WIDTH,
            ML_WIDTH,
            ML_HEADS,
            ML_HEADS)
N_IN = sum(IN_SIZES)
IN_SPLITS = tuple(int(s) for s in np.cumsum(IN_SIZES)[:-1])

kernel_name = "hybrid_diffattn_mlstm_macaron_adaln"


def rmsnorm(t, g):
    tf = t.astype(jnp.float32)
    tf = tf * lax.rsqrt(jnp.mean(tf * tf, axis=-1, keepdims=True) + EPS)
    return tf.astype(t.dtype) * g


def swiglu(h, w12, w3):
    a, b = jnp.split(h @ w12, 2, axis=-1)
    return (jax.nn.silu(a) * b) @ w3


def causal_conv(u, w, b):
    s = u.shape[1]
    up = jnp.pad(u, ((0, 0), (CONV_K - 1, 0), (0, 0)))
    out = b
    for j in range(CONV_K):
        out = out + up[:, j:j + s] * w[j]
    return out


def partial_rope(t, cos, sin):
    tr, tp = t[..., :ROPE_DIM], t[..., ROPE_DIM:]
    x1, x2 = jnp.split(tr, 2, axis=-1)
    rot = jnp.concatenate([-x2, x1], axis=-1)
    return jnp.concatenate([tr * cos + rot * sin, tp], axis=-1)


def diff_attention(q, k, v, g_q, g_k, lam_vecs, g_out, lambda_init):
    bsz, s, _ = q.shape
    q = rmsnorm(q.reshape(bsz, s, DA_HEADS, 2, DA_QK_DIM), g_q)
    k = rmsnorm(k.reshape(bsz, s, DA_HEADS, 2, DA_QK_DIM), g_k)
    pos = jnp.arange(s, dtype=jnp.float32)
    inv_freq = ROPE_THETA ** (-jnp.arange(0, ROPE_DIM, 2, dtype=jnp.float32) / ROPE_DIM)
    ang = pos[:, None] * inv_freq[None, :]
    ang = jnp.concatenate([ang, ang], axis=-1)[:, None, None, :]
    cos, sin = jnp.cos(ang).astype(q.dtype), jnp.sin(ang).astype(q.dtype)
    q = partial_rope(q, cos, sin).transpose(0, 2, 3, 1, 4)
    k = partial_rope(k, cos, sin).transpose(0, 2, 3, 1, 4)
    v = v.reshape(bsz, s, DA_HEADS, DA_V_DIM).transpose(0, 2, 1, 3)
    lv = lam_vecs.astype(jnp.float32)
    lam = jnp.exp(jnp.sum(lv[0] * lv[1])) - jnp.exp(jnp.sum(lv[2] * lv[3])) + lambda_init
    scale = DA_QK_DIM ** -0.5
    kpos = jnp.arange(s)

    def block(i):
        qs = lax.dynamic_slice_in_dim(q, i * Q_BLOCK, Q_BLOCK, axis=3)
        sc = jnp.einsum('bhcqd,bhckd->bhcqk', qs, k).astype(jnp.float32) * scale
        qpos = i * Q_BLOCK + jnp.arange(Q_BLOCK)
        sc = jnp.where(kpos[None, :] <= qpos[:, None], sc, -jnp.inf)
        p = jax.nn.softmax(sc, axis=-1)
        a = (p[:, :, 0] - lam * p[:, :, 1]).astype(v.dtype)
        return jnp.einsum('bhqk,bhkd->bhqd', a, v)

    o = lax.map(block, jnp.arange(s // Q_BLOCK))
    o = o.transpose(1, 0, 3, 2, 4).reshape(bsz, s, DA_HEADS, DA_V_DIM)
    o = rmsnorm(o, g_out) * (1.0 - lambda_init)
    return o.reshape(bsz, s, DA_WIDTH)


def mlstm(q, k, v, o_pre, i_pre, f_pre, g_out):
    dtype = q.dtype
    bsz, s, _ = q.shape
    nc = s // CHUNK

    def to_chunks(t):
        return t.astype(jnp.float32).reshape(bsz, nc, CHUNK, ML_HEADS, ML_DIM).transpose(1, 0, 3, 2, 4)

    def gates_to_chunks(t):
        return t.astype(jnp.float32).reshape(bsz, nc, CHUNK, ML_HEADS).transpose(1, 0, 3, 2)

    qc = to_chunks(q) * (ML_DIM ** -0.5)
    kc, vc = to_chunks(k), to_chunks(v)
    ic = gates_to_chunks(i_pre)
    fc = jax.nn.log_sigmoid(gates_to_chunks(f_pre))
    tri = jnp.arange(CHUNK)[:, None] >= jnp.arange(CHUNK)[None, :]

    def step(carry, inp):
        C, n, m = carry
        qb, kb, vb, ib, fb = inp
        b = jnp.cumsum(fb, axis=-1)
        D = jnp.where(tri, b[..., :, None] - b[..., None, :] + ib[..., None, :], -jnp.inf)
        inter = b + m[..., None]
        m_t = jnp.maximum(inter, jnp.max(D, axis=-1))
        W = jnp.einsum('bhtd,bhsd->bhts', qb, kb) * jnp.exp(D - m_t[..., None])
        e_inter = jnp.exp(inter - m_t)
        num = e_inter[..., None] * jnp.einsum('bhtd,bhde->bhte', qb, C) + jnp.einsum('bhts,bhse->bhte', W, vb)
        den = e_inter * jnp.einsum('bhtd,bhd->bht', qb, n) + jnp.sum(W, axis=-1)
        h = num / jnp.maximum(jnp.abs(den), jnp.exp(-m_t))[..., None]
        g = b[..., -1]
        a = g[..., None] - b + ib
        m_new = jnp.maximum(g + m, jnp.max(a, axis=-1))
        decay = jnp.exp(g + m - m_new)
        w = jnp.exp(a - m_new[..., None])
        C_new = decay[..., None, None] * C + jnp.einsum('bhs,bhsd,bhse->bhde', w, kb, vb)
        n_new = decay[..., None] * n + jnp.einsum('bhs,bhsd->bhd', w, kb)
        return (C_new, n_new, m_new), h

    init = (jnp.zeros((bsz, ML_HEADS, ML_DIM, ML_DIM), jnp.float32),
            jnp.zeros((bsz, ML_HEADS, ML_DIM), jnp.float32),
            jnp.zeros((bsz, ML_HEADS), jnp.float32))
    _, h = lax.scan(step, init, (qc, kc, vc, ic, fc))
    h = h.transpose(1, 0, 3, 2, 4).reshape(bsz, s, ML_HEADS, ML_DIM).astype(dtype)
    h = rmsnorm(h, 1.0).reshape(bsz, s, ML_WIDTH) * g_out
    return jax.nn.sigmoid(o_pre) * h


def token_mix(h, w_in, conv_w, conv_b, b_igate, b_fgate, g_qnorm, g_knorm,
              lambda_qk, g_da_out, g_ml_out, w_out, lambda_init):
    u = h @ w_in
    da_q, da_k, da_v, ml_qk, ml_v, ml_o, ml_i, ml_f = jnp.split(u, IN_SPLITS, axis=-1)
    y_da = diff_attention(da_q, da_k, da_v, g_qnorm, g_knorm, lambda_qk, g_da_out, lambda_init)
    ml_qk = jax.nn.silu(causal_conv(ml_qk, conv_w, conv_b))
    ml_q, ml_k = jnp.split(ml_qk, 2, axis=-1)
    y_ml = mlstm(ml_q, ml_k, ml_v, ml_o, ml_i + b_igate, ml_f + b_fgate, g_ml_out)
    return jnp.concatenate([y_da, y_ml], axis=-1) @ w_out


def setup_inputs(seed: int = 0) -> dict:
    key = jax.random.key(seed)
    ks = jax.random.split(key, 24)
    f32 = jnp.float32
    nrm = lambda k, shape, s: jax.random.normal(k, shape, f32) * s
    L = DEPTH
    return {
        "x": jax.random.normal(ks[0], (BATCH, SEQ, D_MODEL), f32),
        "c": jax.random.normal(ks[1], (BATCH, D_MODEL), f32),
        "w_ada": nrm(ks[2], (L, D_MODEL, 9 * D_MODEL), 0.1 * D_MODEL ** -0.5),
        "b_ada": nrm(ks[3], (L, 9 * D_MODEL), 0.02),
        "g_norm": 1.0 + nrm(ks[4], (L, 3, D_MODEL), 0.02),
        "ffn1_w12": nrm(ks[5], (L, D_MODEL, 2 * D_FF), D_MODEL ** -0.5),
        "ffn1_w3": nrm(ks[6], (L, D_FF, D_MODEL), D_FF ** -0.5),
        "w_in": nrm(ks[7], (L, D_MODEL, N_IN), D_MODEL ** -0.5),
        "conv_w": nrm(ks[8], (L, CONV_K, 2 * ML_WIDTH), CONV_K ** -0.5),
        "conv_b": nrm(ks[9], (L, 2 * ML_WIDTH), 0.02),
        "b_igate": nrm(ks[10], (L, ML_HEADS), 0.1),
        "b_fgate": 3.0 + 3.0 * jax.random.uniform(ks[11], (L, ML_HEADS), f32),
        "g_qnorm": 1.0 + nrm(ks[12], (L, DA_QK_DIM), 0.02),
        "g_knorm": 1.0 + nrm(ks[13], (L, DA_QK_DIM), 0.02),
        "lambda_qk": nrm(ks[14], (L, 4, DA_QK_DIM), 0.1),
        "g_da_out": 1.0 + nrm(ks[15], (L, DA_V_DIM), 0.02),
        "g_ml_out": 1.0 + nrm(ks[16], (L, ML_WIDTH), 0.02),
        "w_out": nrm(ks[17], (L, MIX_WIDTH, D_MODEL), MIX_WIDTH ** -0.5),
        "ffn2_w12": nrm(ks[18], (L, D_MODEL, 2 * D_FF), D_MODEL ** -0.5),
        "ffn2_w3": nrm(ks[19], (L, D_FF, D_MODEL), D_FF ** -0.5),
    }


def reference(x, c, w_ada, b_ada, g_norm, ffn1_w12, ffn1_w3, w_in, conv_w, conv_b,
              b_igate, b_fgate, g_qnorm, g_knorm, lambda_qk, g_da_out, g_ml_out,
              w_out, ffn2_w12, ffn2_w3):
    bsz = x.shape[0]
    cs = jax.nn.silu(c)
    for l in range(DEPTH):
        lambda_init = 0.8 - 0.6 * math.exp(-0.3 * l)
        mod = (cs @ w_ada[l] + b_ada[l]).reshape(bsz, 3, 3, D_MODEL)
        shift, scale, gate = mod[:, :, 0], mod[:, :, 1], mod[:, :, 2]

        h = rmsnorm(x, g_norm[l, 0]) * (1.0 + scale[:, 0, None]) + shift[:, 0, None]
        x = x + 0.5 * (1.0 + gate[:, 0, None]) * swiglu(h, ffn1_w12[l], ffn1_w3[l])

        h = rmsnorm(x, g_norm[l, 1]) * (1.0 + scale[:, 1, None]) + shift[:, 1, None]
        x = x + (1.0 + gate[:, 1, None]) * token_mix(
            h, w_in[l], conv_w[l], conv_b[l], b_igate[l], b_fgate[l], g_qnorm[l], g_knorm[l],
            lambda_qk[l], g_da_out[l], g_ml_out[l], w_out[l], lambda_init)

        h = rmsnorm(x, g_norm[l, 2]) * (1.0 + scale[:, 2, None]) + shift[:, 2, None]
        x = x + 0.5 * (1.0 + gate[:, 2, None]) * swiglu(h, ffn2_w12[l], ffn2_w3[l])
    return x
```

```python
import functools

import jax
import jax.numpy as jnp
from jax import lax
from jax.experimental import pallas as pl
from jax.experimental.pallas import tpu as pltpu

F32 = jnp.float32
BF16 = jnp.bfloat16

D_MODEL = 1024
DA_HEADS = 4
DA_QK_DIM = 64
DA_V_DIM = 2 * DA_QK_DIM
DA_WIDTH = DA_HEADS * DA_V_DIM
ML_HEADS = 4
ML_DIM = 128
ML_WIDTH = ML_HEADS * ML_DIM
ROPE_THETA = 500000.0
ROPE_DIM = DA_QK_DIM // 4
D_FF = 2816
CONV_K = 4
EPS = 1e-6
LAMBDA_INIT = 0.8 - 0.6 * 1.0

LANES = 128
VMEM_LIMIT = 56 * 1024 * 1024

FFN_TM = 512
FFN_SUB = 256
PROJ_TM = 512
ATT_T = 256
ML_CHUNK = 256
CONV_HALO = 8


def _const_spec(shape):
    nd = len(shape)
    return pl.BlockSpec(shape, lambda *_: (0,) * nd, pipeline_mode=pl.Buffered(1))


def _sigmoid(x):
    return 1.0 / (1.0 + jnp.exp(-x))


def _mod_norm(x, g, shift, scale):
    ms = jnp.mean(x * x, axis=-1, keepdims=True)
    return (x * lax.rsqrt(ms + EPS)) * g * (1.0 + scale) + shift


def _ada_kernel(c_ref, w_ref, b_ref, o_ref):
    c = c_ref[...]
    cs = (c * _sigmoid(c)).astype(BF16)
    o_ref[...] = jnp.dot(cs, w_ref[...].astype(BF16), preferred_element_type=F32) + b_ref[...]


def _ada(c, w_ada, b_ada):
    bsz, d = c.shape
    n = w_ada.shape[1]
    tn = 1024
    return pl.pallas_call(
        _ada_kernel,
        grid=(n // tn,),
        in_specs=[pl.BlockSpec((bsz, d), lambda j: (0, 0)),
                  pl.BlockSpec((d, tn), lambda j: (0, j)),
                  pl.BlockSpec((1, tn), lambda j: (0, j))],
        out_specs=pl.BlockSpec((bsz, tn), lambda j: (0, j)),
        out_shape=jax.ShapeDtypeStruct((bsz, n), F32),
        compiler_params=pltpu.CompilerParams(dimension_semantics=("arbitrary",)),
        name="adaln_mod",
    )(c, w_ada, b_ada.reshape(1, n))


def _ffn_kernel(x_ref, mod_ref, g_ref, w1_ref, w2_ref, w3_ref, o_ref, act_ref):
    x = x_ref[0]
    mod = mod_ref[0]
    hb = _mod_norm(x, g_ref[...], mod[0:1], mod[1:2]).astype(BF16)
    for c in range(D_FF // FFN_SUB):
        cols = slice(c * FFN_SUB, (c + 1) * FFN_SUB)
        a = jnp.dot(hb, w1_ref[:, cols], preferred_element_type=F32)
        b = jnp.dot(hb, w2_ref[:, cols], preferred_element_type=F32)
        act_ref[:, cols] = (a * _sigmoid(a) * b).astype(BF16)
    y = jnp.dot(act_ref[...], w3_ref[...], preferred_element_type=F32)
    o_ref[0] = x + (0.5 * (1.0 + mod[2:3])) * y


def _ffn(x, mod3, g, w1, w2, w3):
    bsz, s, d = x.shape
    tm = FFN_TM
    return pl.pallas_call(
        _ffn_kernel,
        grid=(bsz, s // tm),
        in_specs=[pl.BlockSpec((1, tm, d), lambda b, i: (b, i, 0)),
                  pl.BlockSpec((1, 3, d), lambda b, i: (b, 0, 0)),
                  _const_spec((1, d)),
                  _const_spec(w1.shape), _const_spec(w2.shape), _const_spec(w3.shape)],
        out_specs=pl.BlockSpec((1, tm, d), lambda b, i: (b, i, 0)),
        out_shape=jax.ShapeDtypeStruct(x.shape, F32),
        scratch_shapes=[pltpu.VMEM((tm, D_FF), BF16)],
        compiler_params=pltpu.CompilerParams(
            dimension_semantics=("arbitrary", "arbitrary"), vmem_limit_bytes=VMEM_LIMIT),
        name="ffn",
    )(x, mod3, g.reshape(1, d), w1, w2, w3)


def _group_norm_rope(u, gvec, bd, cos, sina, sinb):
    x2 = u * u
    hi = x2.astype(BF16)
    lo = (x2 - hi.astype(F32)).astype(BF16)
    ssq = (jnp.dot(hi, bd, preferred_element_type=F32) + jnp.dot(lo, bd, preferred_element_type=F32))
    xn = (u * lax.rsqrt(ssq * (1.0 / DA_QK_DIM) + EPS)) * gvec
    outs = []
    for h in range(DA_HEADS):
        xh = xn[:, h * LANES:(h + 1) * LANES]
        up = pltpu.roll(xh, LANES - ROPE_DIM // 2, 1)
        dn = pltpu.roll(xh, ROPE_DIM // 2, 1)
        outs.append(xh * cos + up * sina + dn * sinb)
    return jnp.concatenate(outs, axis=1)


def _inproj_kernel(x_ref, mod_ref, g_ref, wq_ref, wk_ref, wv_ref, wmqk_ref, wmv_ref, wmo_ref, wif_ref,
                   gq_ref, gk_ref, cos_ref, sina_ref, sinb_ref, bd_ref, cw_ref, cb_ref, bif_ref,
                   q_out, k_out, v_out, mq_out, mk_out, mv_out, mo_out, gate_out, ext_ref):
    tm = x_ref.shape[1]
    si = pl.program_id(1)
    mod = mod_ref[0]
    hb = _mod_norm(x_ref[0], g_ref[...], mod[0:1], mod[1:2]).astype(BF16)

    cos, sina, sinb = cos_ref[...], sina_ref[...], sinb_ref[...]
    bd = bd_ref[...]
    uq = jnp.dot(hb, wq_ref[...], preferred_element_type=F32)
    q_out[0] = _group_norm_rope(uq, gq_ref[...], bd, cos, sina, sinb).astype(BF16)
    uk = jnp.dot(hb, wk_ref[...], preferred_element_type=F32)
    k_out[0] = _group_norm_rope(uk, gk_ref[...], bd, cos, sina, sinb).astype(BF16)
    v_out[0] = jnp.dot(hb, wv_ref[...], preferred_element_type=F32).astype(BF16)

    @pl.when(si == 0)
    def _():
        ext_ref[0:CONV_HALO, :] = jnp.zeros((CONV_HALO, 2 * ML_WIDTH), F32)

    uqk = jnp.dot(hb, wmqk_ref[...], preferred_element_type=F32)
    ext_ref[CONV_HALO:CONV_HALO + tm, :] = uqk
    cw = cw_ref[...]
    acc = cb_ref[...]
    for j in range(CONV_K - 1):
        off = CONV_HALO - (CONV_K - 1) + j
        acc = acc + ext_ref[off:off + tm, :] * cw[j:j + 1]
    acc = acc + uqk * cw[CONV_K - 1:CONV_K]
    ext_ref[0:CONV_HALO, :] = ext_ref[tm:tm + CONV_HALO, :]
    qk = acc * _sigmoid(acc)
    mq_out[0] = (qk[:, :ML_WIDTH] * (ML_DIM ** -0.5)).astype(BF16)
    mk_out[0] = qk[:, ML_WIDTH:].astype(BF16)

    mv_out[0] = jnp.dot(hb, wmv_ref[...], preferred_element_type=F32).astype(BF16)
    mo_out[0] = _sigmoid(jnp.dot(hb, wmo_ref[...], preferred_element_type=F32)).astype(BF16)

    uif = jnp.dot(hb, wif_ref[...], preferred_element_type=F32)[:, :2 * ML_HEADS] + bif_ref[...]
    lane = lax.broadcasted_iota(jnp.int32, uif.shape, 1)
    logsig = jnp.minimum(uif, 0.0) - jnp.log1p(jnp.exp(-jnp.abs(uif)))
    gate_out[0] = jnp.where(lane < ML_HEADS, uif, logsig)


def _inproj(x, mod3, g, wq, wk, wv, wmqk, wmv, wmo, wif, gq, gk, cos, sina, sinb, bd, cw, cb, bif):
    bsz, s, d = x.shape
    tm = PROJ_TM
    tok = lambda width, dt: jax.ShapeDtypeStruct((bsz, s, width), dt)
    tok_spec = lambda width: pl.BlockSpec((1, tm, width), lambda b, i: (b, i, 0))
    tab_spec = pl.BlockSpec((tm, LANES), lambda b, i: (i, 0))
    return pl.pallas_call(
        _inproj_kernel,
        grid=(bsz, s // tm),
        in_specs=[tok_spec(d),
                  pl.BlockSpec((1, 3, d), lambda b, i: (b, 0, 0)),
                  _const_spec((1, d)),
                  _const_spec(wq.shape), _const_spec(wk.shape), _const_spec(wv.shape),
                  _const_spec(wmqk.shape), _const_spec(wmv.shape), _const_spec(wmo.shape),
                  _const_spec(wif.shape),
                  _const_spec(gq.shape), _const_spec(gk.shape),
                  tab_spec, tab_spec, tab_spec,
                  _const_spec(bd.shape), _const_spec(cw.shape), _const_spec(cb.shape),
                  _const_spec(bif.shape)],
        out_specs=[tok_spec(DA_WIDTH), tok_spec(DA_WIDTH), tok_spec(DA_WIDTH),
                   tok_spec(ML_WIDTH), tok_spec(ML_WIDTH), tok_spec(ML_WIDTH), tok_spec(ML_WIDTH),
                   tok_spec(2 * ML_HEADS)],
        out_shape=[tok(DA_WIDTH, BF16), tok(DA_WIDTH, BF16), tok(DA_WIDTH, BF16),
                   tok(ML_WIDTH, BF16), tok(ML_WIDTH, BF16), tok(ML_WIDTH, BF16), tok(ML_WIDTH, BF16),
                   tok(2 * ML_HEADS, F32)],
        scratch_shapes=[pltpu.VMEM((tm + CONV_HALO, 2 * ML_WIDTH), F32)],
        compiler_params=pltpu.CompilerParams(
            dimension_semantics=("arbitrary", "arbitrary"), vmem_limit_bytes=VMEM_LIMIT),
        name="in_proj",
    )(x, mod3, g.reshape(1, d), wq, wk, wv, wmqk, wmv, wmo, wif, gq, gk, cos, sina, sinb, bd, cw, cb, bif)


def _attn_kernel(q_ref, k_ref, v_ref, lam_ref, gout_ref, o_ref, qs_ref, m_ref, l_ref, acc_ref):
    t = ATT_T
    qi = pl.program_id(2)

    q = q_ref[0]
    lane = lax.broadcasted_iota(jnp.int32, q.shape, 1)
    zero = jnp.zeros_like(q)
    qs_ref[0:t, :] = jnp.where(lane < DA_QK_DIM, q, zero)
    qs_ref[t:2 * t, :] = jnp.where(lane >= DA_QK_DIM, q, zero)
    m_ref[...] = jnp.full(m_ref.shape, -jnp.inf, F32)
    l_ref[...] = jnp.zeros(l_ref.shape, F32)
    acc_ref[...] = jnp.zeros(acc_ref.shape, F32)

    def block(kb, diagonal):
        start = pl.multiple_of(kb * t, t)
        ks = k_ref[0, pl.ds(start, t), :]
        vs = v_ref[0, pl.ds(start, t), :]
        s = lax.dot_general(qs_ref[...], ks, (((1,), (1,)), ((), ())), preferred_element_type=F32)
        if diagonal:
            row = lax.broadcasted_iota(jnp.int32, s.shape, 0)
            col = lax.broadcasted_iota(jnp.int32, s.shape, 1)
            row = jnp.where(row >= t, row - t, row)
            s = jnp.where(col <= row, s, -jnp.inf)
        m_prev = m_ref[...]
        m_new = jnp.maximum(m_prev, jnp.max(s, axis=1, keepdims=True))
        alpha = jnp.exp(m_prev - m_new)
        p = jnp.exp(s - m_new)
        l_ref[...] = alpha * l_ref[...] + jnp.sum(p, axis=1, keepdims=True)
        acc_ref[...] = alpha * acc_ref[...] + jnp.dot(p.astype(BF16), vs, preferred_element_type=F32)
        m_ref[...] = m_new

    def body(kb, carry):
        block(kb, False)
        return carry

    lax.fori_loop(0, qi, body, 0)
    block(qi, True)

    o = acc_ref[...] / l_ref[...]
    lv = lam_ref[...]
    lam = (jnp.exp(jnp.sum(lv[0:1] * lv[1:2], axis=1, keepdims=True))
           - jnp.exp(jnp.sum(lv[2:3] * lv[3:4], axis=1, keepdims=True)) + LAMBDA_INIT)
    od = o[0:t] - lam * o[t:2 * t]
    ms = jnp.mean(od * od, axis=-1, keepdims=True)
    o_ref[0] = ((od * lax.rsqrt(ms + EPS)) * gout_ref[...] * (1.0 - LAMBDA_INIT)).astype(BF16)


def _attention(q, k, v, lam_vecs, g_out):
    bsz, s, _ = q.shape
    t = ATT_T
    blk = pl.BlockSpec((1, t, DA_V_DIM), lambda b, h, i: (b, i, h))
    full = pl.BlockSpec((1, s, DA_V_DIM), lambda b, h, i: (b, 0, h))
    return pl.pallas_call(
        _attn_kernel,
        grid=(bsz, DA_HEADS, s // t),
        in_specs=[blk, full, full, _const_spec(lam_vecs.shape), _const_spec((1, DA_V_DIM))],
        out_specs=blk,
        out_shape=jax.ShapeDtypeStruct((bsz, s, DA_WIDTH), BF16),
        scratch_shapes=[pltpu.VMEM((2 * t, DA_V_DIM), BF16),
                        pltpu.VMEM((2 * t, 1), F32),
                        pltpu.VMEM((2 * t, 1), F32),
                        pltpu.VMEM((2 * t, DA_V_DIM), F32)],
        compiler_params=pltpu.CompilerParams(
            dimension_semantics=("arbitrary", "arbitrary", "arbitrary"), vmem_limit_bytes=VMEM_LIMIT),
        name="diff_attention",
    )(q, k, v, lam_vecs, g_out.reshape(1, DA_V_DIM))


def _mlstm_kernel(q_ref, k_ref, v_ref, o_ref, gc_ref, gr_ref, gout_ref, y_ref, c_ref, m_ref):
    ln = ML_CHUNK

    @pl.when(pl.program_id(1) == 0)
    def _():
        c_ref[...] = jnp.zeros(c_ref.shape, F32)
        m_ref[...] = jnp.zeros(m_ref.shape, F32)

    gc = gc_ref[0]
    gr = gr_ref[0]
    row = lax.broadcasted_iota(jnp.int32, (ln, ln), 0)
    col = lax.broadcasted_iota(jnp.int32, (ln, ln), 1)
    lower = col <= row
    upper = row <= col
    ones_col = (lax.broadcasted_iota(jnp.int32, (ln, ML_DIM), 1) == 0).astype(BF16)

    for h in range(ML_HEADS):
        lanes = slice(h * ML_DIM, (h + 1) * ML_DIM)
        q = q_ref[0, :, lanes]
        k = k_ref[0, :, lanes]
        v_aug = jnp.concatenate([v_ref[0, :, lanes], ones_col], axis=1)
        i_col, f_col = gc[:, h:h + 1], gc[:, ML_HEADS + h:ML_HEADS + h + 1]
        i_row, f_row = gr[h:h + 1, :], gr[ML_HEADS + h:ML_HEADS + h + 1, :]
        b_col = jnp.sum(jnp.where(lower, f_row, 0.0), axis=1, keepdims=True)
        b_row = jnp.sum(jnp.where(upper, f_col, 0.0), axis=0, keepdims=True)
        g = b_row[:, ln - 1:ln]
        m = m_ref[h:h + 1, 0:1]

        d = jnp.where(lower, b_col - b_row + i_row, -jnp.inf)
        inter = b_col + m
        m_t = jnp.maximum(inter, jnp.max(d, axis=1, keepdims=True))
        s = lax.dot_general(q, k, (((1,), (1,)), ((), ())), preferred_element_type=F32)
        w = (s * jnp.exp(d - m_t)).astype(BF16)
        state = c_ref[h]
        full = (jnp.exp(inter - m_t) * jnp.dot(q, state.astype(BF16), preferred_element_type=F32)
                + jnp.dot(w, v_aug, preferred_element_type=F32))
        num = full[:, :ML_DIM]
        den = full[:, ML_DIM:ML_DIM + 1]
        hh = num / jnp.maximum(jnp.abs(den), jnp.exp(-m_t))
        hn = hh * lax.rsqrt(jnp.mean(hh * hh, axis=-1, keepdims=True) + EPS)
        y_ref[0, :, lanes] = (o_ref[0, :, lanes].astype(F32) * (hn * gout_ref[:, lanes])).astype(BF16)

        a_col = g - b_col + i_col
        a_row = g - b_row + i_row
        m_new = jnp.maximum(g + m, jnp.max(a_row, axis=1, keepdims=True))
        kw_t = (k.astype(F32) * jnp.exp(a_col - m_new)).T.astype(BF16)
        c_ref[h] = jnp.exp(g + m - m_new) * state + jnp.dot(kw_t, v_aug, preferred_element_type=F32)
        m_ref[h:h + 1, :] = jnp.broadcast_to(m_new, (1, LANES))


def _mlstm(q, k, v, o, gates_col, gates_row, g_out):
    bsz, s, _ = q.shape
    ln = ML_CHUNK
    tok = pl.BlockSpec((1, ln, ML_WIDTH), lambda b, c: (b, c, 0))
    return pl.pallas_call(
        _mlstm_kernel,
        grid=(bsz, s // ln),
        in_specs=[tok, tok, tok, tok,
                  pl.BlockSpec((1, ln, 2 * ML_HEADS), lambda b, c: (b, c, 0)),
                  pl.BlockSpec((1, 2 * ML_HEADS, ln), lambda b, c: (b, 0, c)),
                  _const_spec((1, ML_WIDTH))],
        out_specs=tok,
        out_shape=jax.ShapeDtypeStruct((bsz, s, ML_WIDTH), BF16),
        scratch_shapes=[pltpu.VMEM((ML_HEADS, ML_DIM, 2 * ML_DIM), F32),
                        pltpu.VMEM((8, LANES), F32)],
        compiler_params=pltpu.CompilerParams(
            dimension_semantics=("arbitrary", "arbitrary"), vmem_limit_bytes=VMEM_LIMIT),
        name="mlstm",
    )(q, k, v, o, gates_col, gates_row, g_out.reshape(1, ML_WIDTH))


def _outproj_kernel(x_ref, mod_ref, ya_ref, ym_ref, wa_ref, wm_ref, o_ref):
    y = (jnp.dot(ya_ref[0], wa_ref[...], preferred_element_type=F32)
         + jnp.dot(ym_ref[0], wm_ref[...], preferred_element_type=F32))
    o_ref[0] = x_ref[0] + (1.0 + mod_ref[0][2:3]) * y


def _outproj(x, mod3, y_da, y_ml, w_a, w_m):
    bsz, s, d = x.shape
    tm = PROJ_TM
    tok_spec = lambda width: pl.BlockSpec((1, tm, width), lambda b, i: (b, i, 0))
    return pl.pallas_call(
        _outproj_kernel,
        grid=(bsz, s // tm),
        in_specs=[tok_spec(d), pl.BlockSpec((1, 3, d), lambda b, i: (b, 0, 0)),
                  tok_spec(DA_WIDTH), tok_spec(ML_WIDTH),
                  _const_spec(w_a.shape), _const_spec(w_m.shape)],
        out_specs=tok_spec(d),
        out_shape=jax.ShapeDtypeStruct(x.shape, F32),
        compiler_params=pltpu.CompilerParams(
            dimension_semantics=("arbitrary", "arbitrary"), vmem_limit_bytes=VMEM_LIMIT),
        name="out_proj",
    )(x, mod3, y_da, y_ml, w_a, w_m)


def _rope_tables(s):
    half = ROPE_DIM // 2
    pos = jnp.arange(s, dtype=F32)
    inv_freq = ROPE_THETA ** (-jnp.arange(0, ROPE_DIM, 2, dtype=F32) / ROPE_DIM)
    ang = pos[:, None] * inv_freq[None, :]
    cos, sin = jnp.cos(ang), jnp.sin(ang)
    rest = DA_QK_DIM - ROPE_DIM
    cos_g = jnp.concatenate([cos, cos, jnp.ones((s, rest), F32)], axis=1)
    sina_g = jnp.concatenate([-sin, jnp.zeros((s, DA_QK_DIM - half), F32)], axis=1)
    sinb_g = jnp.concatenate([jnp.zeros((s, half), F32), sin, jnp.zeros((s, rest), F32)], axis=1)
    rep = LANES // DA_QK_DIM
    return jnp.tile(cos_g, (1, rep)), jnp.tile(sina_g, (1, rep)), jnp.tile(sinb_g, (1, rep))


def kernel(x, c, w_ada, b_ada, g_norm, ffn1_w12, ffn1_w3, w_in, conv_w, conv_b, b_igate, b_fgate,
           g_qnorm, g_knorm, lambda_qk, g_da_out, g_ml_out, w_out, ffn2_w12, ffn2_w3):
    bsz, s, d = x.shape
    l = 0
    mod = _ada(c, w_ada[l], b_ada[l]).reshape(bsz, 3, 3, d)

    def ffn(xin, sub, w12, w3):
        w12b = w12.astype(BF16)
        return _ffn(xin, mod[:, sub], g_norm[l, sub], w12b[:, :D_FF], w12b[:, D_FF:], w3.astype(BF16))

    x = ffn(x, 0, ffn1_w12[l], ffn1_w3[l])

    wb = w_in[l].astype(BF16)
    o0 = 0
    parts = []
    for width in (DA_WIDTH, DA_WIDTH, DA_WIDTH, 2 * ML_WIDTH, ML_WIDTH, ML_WIDTH, 2 * ML_HEADS):
        parts.append(wb[:, o0:o0 + width])
        o0 += width
    wq, wk, wv, wmqk, wmv, wmo, wif = parts
    wif = jnp.pad(wif, ((0, 0), (0, LANES - 2 * ML_HEADS)))
    groups = DA_WIDTH // DA_QK_DIM
    gq = jnp.tile(g_qnorm[l] * (DA_QK_DIM ** -0.5), groups).reshape(1, DA_WIDTH)
    gk = jnp.tile(g_knorm[l], groups).reshape(1, DA_WIDTH)
    cos, sina, sinb = _rope_tables(s)
    gid = jnp.arange(DA_WIDTH) // DA_QK_DIM
    bd = (gid[:, None] == gid[None, :]).astype(BF16)
    bif = jnp.concatenate([b_igate[l], b_fgate[l]]).reshape(1, 2 * ML_HEADS)
    da_q, da_k, da_v, ml_q, ml_k, ml_v, ml_o, gates = _inproj(
        x, mod[:, 1], g_norm[l, 1], wq, wk, wv, wmqk, wmv, wmo, wif, gq, gk, cos, sina, sinb, bd,
        conv_w[l], conv_b[l].reshape(1, 2 * ML_WIDTH), bif)

    y_da = _attention(da_q, da_k, da_v, lambda_qk[l], g_da_out[l])
    y_ml = _mlstm(ml_q, ml_k, ml_v, ml_o, gates, jnp.swapaxes(gates, 1, 2), g_ml_out[l])

    wo = w_out[l].astype(BF16)
    x = _outproj(x, mod[:, 1], y_da, y_ml, wo[:DA_WIDTH], wo[DA_WIDTH:])
    return ffn(x, 2, ffn2_w12[l], ffn2_w3[l])
```

```python
import functools

import jax
import jax.numpy as jnp
from jax import lax
from jax.experimental import pallas as pl
from jax.experimental.pallas import tpu as pltpu

F32 = jnp.float32
BF16 = jnp.bfloat16

D_MODEL = 1024
DA_HEADS = 4
DA_QK_DIM = 64
DA_V_DIM = 2 * DA_QK_DIM
DA_WIDTH = DA_HEADS * DA_V_DIM
ML_HEADS = 4
ML_DIM = 128
ML_WIDTH = ML_HEADS * ML_DIM
ROPE_THETA = 500000.0
ROPE_DIM = DA_QK_DIM // 4
D_FF = 2816
CONV_K = 4
EPS = 1e-6
LAMBDA_INIT = 0.8 - 0.6 * 1.0
LOG2E = 1.4426950408889634

LANES = 128
VMEM_LIMIT = 56 * 1024 * 1024

FFN_TM = 512
FFN_SUB = 256
PROJ_TM = 512
ATT_T = 512
SAFE_SCORE_BOUND = 40.0
ML_CHUNK = 256
CONV_HALO = 8


def _const_spec(shape):
    nd = len(shape)
    return pl.BlockSpec(shape, lambda *_: (0,) * nd, pipeline_mode=pl.Buffered(1))


def _sigmoid(x):
    return 1.0 / (1.0 + jnp.exp(-x))


def _mod_norm(x, g, shift, scale):
    ms = jnp.mean(x * x, axis=-1, keepdims=True)
    return (x * lax.rsqrt(ms + EPS)) * g * (1.0 + scale) + shift


def _ada_kernel(c_ref, w_ref, b_ref, o_ref):
    c = c_ref[...]
    cs = (c * _sigmoid(c)).astype(BF16)
    o_ref[...] = jnp.dot(cs, w_ref[...].astype(BF16), preferred_element_type=F32) + b_ref[...]


def _ada(c, w_ada, b_ada):
    bsz, d = c.shape
    n = w_ada.shape[1]
    tn = 1024
    return pl.pallas_call(
        _ada_kernel,
        grid=(n // tn,),
        in_specs=[pl.BlockSpec((bsz, d), lambda j: (0, 0)),
                  pl.BlockSpec((d, tn), lambda j: (0, j)),
                  pl.BlockSpec((1, tn), lambda j: (0, j))],
        out_specs=pl.BlockSpec((bsz, tn), lambda j: (0, j)),
        out_shape=jax.ShapeDtypeStruct((bsz, n), F32),
        compiler_params=pltpu.CompilerParams(dimension_semantics=("arbitrary",)),
        name="adaln_mod",
    )(c, w_ada, b_ada.reshape(1, n))


def _ffn_kernel(x_ref, mod_ref, g_ref, w1_ref, w2_ref, w3_ref, o_ref, act_ref):
    x = x_ref[0]
    mod = mod_ref[0]
    hb = _mod_norm(x, g_ref[...], mod[0:1], mod[1:2]).astype(BF16)
    for c in range(D_FF // FFN_SUB):
        cols = slice(c * FFN_SUB, (c + 1) * FFN_SUB)
        a = jnp.dot(hb, w1_ref[:, cols], preferred_element_type=F32)
        b = jnp.dot(hb, w2_ref[:, cols], preferred_element_type=F32)
        act_ref[:, cols] = (a * _sigmoid(a) * b).astype(BF16)
    y = jnp.dot(act_ref[...], w3_ref[...], preferred_element_type=F32)
    o_ref[0] = x + (0.5 * (1.0 + mod[2:3])) * y


def _ffn(x, mod3, g, w1, w2, w3):
    bsz, s, d = x.shape
    tm = FFN_TM
    return pl.pallas_call(
        _ffn_kernel,
        grid=(bsz, s // tm),
        in_specs=[pl.BlockSpec((1, tm, d), lambda b, i: (b, i, 0)),
                  pl.BlockSpec((1, 3, d), lambda b, i: (b, 0, 0)),
                  _const_spec((1, d)),
                  _const_spec(w1.shape), _const_spec(w2.shape), _const_spec(w3.shape)],
        out_specs=pl.BlockSpec((1, tm, d), lambda b, i: (b, i, 0)),
        out_shape=jax.ShapeDtypeStruct(x.shape, F32),
        scratch_shapes=[pltpu.VMEM((tm, D_FF), BF16)],
        compiler_params=pltpu.CompilerParams(
            dimension_semantics=("arbitrary", "arbitrary"), vmem_limit_bytes=VMEM_LIMIT),
        name="ffn",
    )(x, mod3, g.reshape(1, d), w1, w2, w3)


def _group_norm_rope(u, gvec, bd, cos, sina, sinb):
    x2 = u * u
    hi = x2.astype(BF16)
    lo = (x2 - hi.astype(F32)).astype(BF16)
    ssq = (jnp.dot(hi, bd, preferred_element_type=F32) + jnp.dot(lo, bd, preferred_element_type=F32))
    xn = (u * lax.rsqrt(ssq * (1.0 / DA_QK_DIM) + EPS)) * gvec
    outs = []
    for h in range(DA_HEADS):
        xh = xn[:, h * LANES:(h + 1) * LANES]
        up = pltpu.roll(xh, LANES - ROPE_DIM // 2, 1)
        dn = pltpu.roll(xh, ROPE_DIM // 2, 1)
        outs.append(xh * cos + up * sina + dn * sinb)
    return jnp.concatenate(outs, axis=1)


def _inproj_kernel(x_ref, mod_ref, g_ref, wq_ref, wk_ref, wv_ref, wmqk_ref, wmv_ref, wmo_ref, wif_ref,
                   gq_ref, gk_ref, cos_ref, sina_ref, sinb_ref, bd_ref, cw_ref, cb_ref, bif_ref,
                   q_out, k_out, v_out, mq_out, mk_out, mv_out, mo_out, gate_out, ext_ref):
    tm = x_ref.shape[1]
    si = pl.program_id(1)
    mod = mod_ref[0]
    hb = _mod_norm(x_ref[0], g_ref[...], mod[0:1], mod[1:2]).astype(BF16)

    cos, sina, sinb = cos_ref[...], sina_ref[...], sinb_ref[...]
    bd = bd_ref[...]
    uq = jnp.dot(hb, wq_ref[...], preferred_element_type=F32)
    q_out[0] = _group_norm_rope(uq, gq_ref[...], bd, cos, sina, sinb).astype(BF16)
    uk = jnp.dot(hb, wk_ref[...], preferred_element_type=F32)
    k_out[0] = _group_norm_rope(uk, gk_ref[...], bd, cos, sina, sinb).astype(BF16)
    v_out[0] = jnp.dot(hb, wv_ref[...], preferred_element_type=F32).astype(BF16)

    @pl.when(si == 0)
    def _():
        ext_ref[0:CONV_HALO, :] = jnp.zeros((CONV_HALO, 2 * ML_WIDTH), F32)

    uqk = jnp.dot(hb, wmqk_ref[...], preferred_element_type=F32)
    ext_ref[CONV_HALO:CONV_HALO + tm, :] = uqk
    cw = cw_ref[...]
    acc = cb_ref[...]
    for j in range(CONV_K - 1):
        off = CONV_HALO - (CONV_K - 1) + j
        acc = acc + ext_ref[off:off + tm, :] * cw[j:j + 1]
    acc = acc + uqk * cw[CONV_K - 1:CONV_K]
    ext_ref[0:CONV_HALO, :] = ext_ref[tm:tm + CONV_HALO, :]
    qk = acc * _sigmoid(acc)
    mq_out[0] = (qk[:, :ML_WIDTH] * (ML_DIM ** -0.5)).astype(BF16)
    mk_out[0] = qk[:, ML_WIDTH:].astype(BF16)

    mv_out[0] = jnp.dot(hb, wmv_ref[...], preferred_element_type=F32).astype(BF16)
    mo_out[0] = _sigmoid(jnp.dot(hb, wmo_ref[...], preferred_element_type=F32)).astype(BF16)

    uif = jnp.dot(hb, wif_ref[...], preferred_element_type=F32)[:, :2 * ML_HEADS] + bif_ref[...]
    lane = lax.broadcasted_iota(jnp.int32, uif.shape, 1)
    logsig = jnp.minimum(uif, 0.0) - jnp.log1p(jnp.exp(-jnp.abs(uif)))
    gate_out[0] = jnp.where(lane < ML_HEADS, uif, logsig)


def _inproj(x, mod3, g, wq, wk, wv, wmqk, wmv, wmo, wif, gq, gk, cos, sina, sinb, bd, cw, cb, bif):
    bsz, s, d = x.shape
    tm = PROJ_TM
    tok = lambda width, dt: jax.ShapeDtypeStruct((bsz, s, width), dt)
    tok_spec = lambda width: pl.BlockSpec((1, tm, width), lambda b, i: (b, i, 0))
    tab_spec = pl.BlockSpec((tm, LANES), lambda b, i: (i, 0))
    return pl.pallas_call(
        _inproj_kernel,
        grid=(bsz, s // tm),
        in_specs=[tok_spec(d),
                  pl.BlockSpec((1, 3, d), lambda b, i: (b, 0, 0)),
                  _const_spec((1, d)),
                  _const_spec(wq.shape), _const_spec(wk.shape), _const_spec(wv.shape),
                  _const_spec(wmqk.shape), _const_spec(wmv.shape), _const_spec(wmo.shape),
                  _const_spec(wif.shape),
                  _const_spec(gq.shape), _const_spec(gk.shape),
                  tab_spec, tab_spec, tab_spec,
                  _const_spec(bd.shape), _const_spec(cw.shape), _const_spec(cb.shape),
                  _const_spec(bif.shape)],
        out_specs=[tok_spec(DA_WIDTH), tok_spec(DA_WIDTH), tok_spec(DA_WIDTH),
                   tok_spec(ML_WIDTH), tok_spec(ML_WIDTH), tok_spec(ML_WIDTH), tok_spec(ML_WIDTH),
                   tok_spec(2 * ML_HEADS)],
        out_shape=[tok(DA_WIDTH, BF16), tok(DA_WIDTH, BF16), tok(DA_WIDTH, BF16),
                   tok(ML_WIDTH, BF16), tok(ML_WIDTH, BF16), tok(ML_WIDTH, BF16), tok(ML_WIDTH, BF16),
                   tok(2 * ML_HEADS, F32)],
        scratch_shapes=[pltpu.VMEM((tm + CONV_HALO, 2 * ML_WIDTH), F32)],
        compiler_params=pltpu.CompilerParams(
            dimension_semantics=("arbitrary", "arbitrary"), vmem_limit_bytes=VMEM_LIMIT),
        name="in_proj",
    )(x, mod3, g.reshape(1, d), wq, wk, wv, wmqk, wmv, wmo, wif, gq, gk, cos, sina, sinb, bd, cw, cb, bif)


def _attn_kernel(bounded_ref, q_ref, k_ref, v_ref, lam_ref, gout_ref, o_ref, qs_ref, m_ref, l_ref, acc_ref):
    t = ATT_T
    qi = pl.program_id(2)

    q = q_ref[0]
    lane = lax.broadcasted_iota(jnp.int32, q.shape, 1)
    zero = jnp.zeros_like(q)
    qs_ref[0:t, :] = jnp.where(lane < DA_QK_DIM, q, zero)
    qs_ref[t:2 * t, :] = jnp.where(lane >= DA_QK_DIM, q, zero)
    l_ref[...] = jnp.zeros(l_ref.shape, F32)
    acc_ref[...] = jnp.zeros(acc_ref.shape, F32)

    def block(kb, diagonal, stabilised):
        start = pl.multiple_of(kb * t, t)
        ks = k_ref[0, pl.ds(start, t), :]
        vs = v_ref[0, pl.ds(start, t), :]
        s = lax.dot_general(qs_ref[...], ks, (((1,), (1,)), ((), ())), preferred_element_type=F32)
        if diagonal:
            row = lax.broadcasted_iota(jnp.int32, s.shape, 0) & (t - 1)
            col = lax.broadcasted_iota(jnp.int32, s.shape, 1)
            s = jnp.where(col <= row, s, -jnp.inf)
        if stabilised:
            m_prev = m_ref[...]
            m_new = jnp.maximum(m_prev, jnp.max(s, axis=1, keepdims=True))
            alpha = jnp.exp2(m_prev - m_new)
            m_ref[...] = m_new
            s = s - m_new
        p = jnp.exp2(s)
        lp = p[:, 0:LANES]
        for c in range(1, t // LANES):
            lp = lp + p[:, c * LANES:(c + 1) * LANES]
        pv = jnp.dot(p.astype(BF16), vs, preferred_element_type=F32)
        if stabilised:
            l_ref[...] = alpha * l_ref[...] + lp
            acc_ref[...] = alpha * acc_ref[...] + pv
        else:
            l_ref[...] += lp
            acc_ref[...] += pv

    def run(stabilised):
        def body(kb, carry):
            block(kb, False, stabilised)
            return carry

        lax.fori_loop(0, qi, body, 0)
        block(qi, True, stabilised)

    @pl.when(bounded_ref[0] != 0)
    def _():
        run(False)

    @pl.when(bounded_ref[0] == 0)
    def _():
        m_ref[...] = jnp.full(m_ref.shape, -jnp.inf, F32)
        run(True)

    o = acc_ref[...] / jnp.sum(l_ref[...], axis=1, keepdims=True)
    lv = lam_ref[...]
    lam = (jnp.exp(jnp.sum(lv[0:1] * lv[1:2], axis=1, keepdims=True))
           - jnp.exp(jnp.sum(lv[2:3] * lv[3:4], axis=1, keepdims=True)) + LAMBDA_INIT)
    od = o[0:t] - lam * o[t:2 * t]
    ms = jnp.mean(od * od, axis=-1, keepdims=True)
    o_ref[0] = ((od * lax.rsqrt(ms + EPS)) * gout_ref[...] * (1.0 - LAMBDA_INIT)).astype(BF16)


def _attention(bounded, q, k, v, lam_vecs, g_out):
    bsz, s, _ = q.shape
    t = ATT_T
    blk = pl.BlockSpec((1, t, DA_V_DIM), lambda b, h, i: (b, i, h))
    full = pl.BlockSpec((1, s, DA_V_DIM), lambda b, h, i: (b, 0, h))
    return pl.pallas_call(
        _attn_kernel,
        grid=(bsz, DA_HEADS, s // t),
        in_specs=[pl.BlockSpec(memory_space=pltpu.SMEM),
                  blk, full, full, _const_spec(lam_vecs.shape), _const_spec((1, DA_V_DIM))],
        out_specs=blk,
        out_shape=jax.ShapeDtypeStruct((bsz, s, DA_WIDTH), BF16),
        scratch_shapes=[pltpu.VMEM((2 * t, DA_V_DIM), BF16),
                        pltpu.VMEM((2 * t, 1), F32),
                        pltpu.VMEM((2 * t, LANES), F32),
                        pltpu.VMEM((2 * t, DA_V_DIM), F32)],
        compiler_params=pltpu.CompilerParams(
            dimension_semantics=("arbitrary", "arbitrary", "arbitrary"), vmem_limit_bytes=VMEM_LIMIT),
        name="diff_attention",
    )(bounded, q, k, v, lam_vecs, g_out.reshape(1, DA_V_DIM))


def _mlstm_kernel(q_ref, k_ref, v_ref, o_ref, gc_ref, gr_ref, gout_ref, y_ref, c_ref, m_ref):
    ln = ML_CHUNK

    @pl.when(pl.program_id(1) == 0)
    def _():
        c_ref[...] = jnp.zeros(c_ref.shape, F32)
        m_ref[...] = jnp.zeros(m_ref.shape, F32)

    gc = gc_ref[0]
    gr = gr_ref[0]
    row = lax.broadcasted_iota(jnp.int32, (ln, ln), 0)
    col = lax.broadcasted_iota(jnp.int32, (ln, ln), 1)
    lower = col <= row
    upper = row <= col
    ones_col = (lax.broadcasted_iota(jnp.int32, (ln, ML_DIM), 1) == 0).astype(BF16)

    for h in range(ML_HEADS):
        lanes = slice(h * ML_DIM, (h + 1) * ML_DIM)
        q = q_ref[0, :, lanes]
        k = k_ref[0, :, lanes]
        v_aug = jnp.concatenate([v_ref[0, :, lanes], ones_col], axis=1)
        i_col, f_col = gc[:, h:h + 1], gc[:, ML_HEADS + h:ML_HEADS + h + 1]
        i_row, f_row = gr[h:h + 1, :], gr[ML_HEADS + h:ML_HEADS + h + 1, :]
        b_col = jnp.sum(jnp.where(lower, f_row, 0.0), axis=1, keepdims=True)
        b_row = jnp.sum(jnp.where(upper, f_col, 0.0), axis=0, keepdims=True)
        g = b_row[:, ln - 1:ln]
        m = m_ref[h:h + 1, 0:1]

        d = jnp.where(lower, b_col - b_row + i_row, -jnp.inf)
        inter = b_col + m
        m_t = jnp.maximum(inter, jnp.max(d, axis=1, keepdims=True))
        s = lax.dot_general(q, k, (((1,), (1,)), ((), ())), preferred_element_type=F32)
        w = (s * jnp.exp(d - m_t)).astype(BF16)
        state = c_ref[h]
        full = (jnp.exp(inter - m_t) * jnp.dot(q, state.astype(BF16), preferred_element_type=F32)
                + jnp.dot(w, v_aug, preferred_element_type=F32))
        num = full[:, :ML_DIM]
        den = full[:, ML_DIM:ML_DIM + 1]
        hh = num / jnp.maximum(jnp.abs(den), jnp.exp(-m_t))
        hn = hh * lax.rsqrt(jnp.mean(hh * hh, axis=-1, keepdims=True) + EPS)
        y_ref[0, :, lanes] = (o_ref[0, :, lanes].astype(F32) * (hn * gout_ref[:, lanes])).astype(BF16)

        a_col = g - b_col + i_col
        a_row = g - b_row + i_row
        m_new = jnp.maximum(g + m, jnp.max(a_row, axis=1, keepdims=True))
        kw_t = (k.astype(F32) * jnp.exp(a_col - m_new)).T.astype(BF16)
        c_ref[h] = jnp.exp(g + m - m_new) * state + jnp.dot(kw_t, v_aug, preferred_element_type=F32)
        m_ref[h:h + 1, :] = jnp.broadcast_to(m_new, (1, LANES))


def _mlstm(q, k, v, o, gates_col, gates_row, g_out):
    bsz, s, _ = q.shape
    ln = ML_CHUNK
    tok = pl.BlockSpec((1, ln, ML_WIDTH), lambda b, c: (b, c, 0))
    return pl.pallas_call(
        _mlstm_kernel,
        grid=(bsz, s // ln),
        in_specs=[tok, tok, tok, tok,
                  pl.BlockSpec((1, ln, 2 * ML_HEADS), lambda b, c: (b, c, 0)),
                  pl.BlockSpec((1, 2 * ML_HEADS, ln), lambda b, c: (b, 0, c)),
                  _const_spec((1, ML_WIDTH))],
        out_specs=tok,
        out_shape=jax.ShapeDtypeStruct((bsz, s, ML_WIDTH), BF16),
        scratch_shapes=[pltpu.VMEM((ML_HEADS, ML_DIM, 2 * ML_DIM), F32),
                        pltpu.VMEM((8, LANES), F32)],
        compiler_params=pltpu.CompilerParams(
            dimension_semantics=("arbitrary", "arbitrary"), vmem_limit_bytes=VMEM_LIMIT),
        name="mlstm",
    )(q, k, v, o, gates_col, gates_row, g_out.reshape(1, ML_WIDTH))


def _outproj_kernel(x_ref, mod_ref, ya_ref, ym_ref, wa_ref, wm_ref, o_ref):
    y = (jnp.dot(ya_ref[0], wa_ref[...], preferred_element_type=F32)
         + jnp.dot(ym_ref[0], wm_ref[...], preferred_element_type=F32))
    o_ref[0] = x_ref[0] + (1.0 + mod_ref[0][2:3]) * y


def _outproj(x, mod3, y_da, y_ml, w_a, w_m):
    bsz, s, d = x.shape
    tm = PROJ_TM
    tok_spec = lambda width: pl.BlockSpec((1, tm, width), lambda b, i: (b, i, 0))
    return pl.pallas_call(
        _outproj_kernel,
        grid=(bsz, s // tm),
        in_specs=[tok_spec(d), pl.BlockSpec((1, 3, d), lambda b, i: (b, 0, 0)),
                  tok_spec(DA_WIDTH), tok_spec(ML_WIDTH),
                  _const_spec(w_a.shape), _const_spec(w_m.shape)],
        out_specs=tok_spec(d),
        out_shape=jax.ShapeDtypeStruct(x.shape, F32),
        compiler_params=pltpu.CompilerParams(
            dimension_semantics=("arbitrary", "arbitrary"), vmem_limit_bytes=VMEM_LIMIT),
        name="out_proj",
    )(x, mod3, y_da, y_ml, w_a, w_m)


def _rope_tables(s):
    half = ROPE_DIM // 2
    pos = jnp.arange(s, dtype=F32)
    inv_freq = ROPE_THETA ** (-jnp.arange(0, ROPE_DIM, 2, dtype=F32) / ROPE_DIM)
    ang = pos[:, None] * inv_freq[None, :]
    cos, sin = jnp.cos(ang), jnp.sin(ang)
    rest = DA_QK_DIM - ROPE_DIM
    cos_g = jnp.concatenate([cos, cos, jnp.ones((s, rest), F32)], axis=1)
    sina_g = jnp.concatenate([-sin, jnp.zeros((s, DA_QK_DIM - half), F32)], axis=1)
    sinb_g = jnp.concatenate([jnp.zeros((s, half), F32), sin, jnp.zeros((s, rest), F32)], axis=1)
    rep = LANES // DA_QK_DIM
    return jnp.tile(cos_g, (1, rep)), jnp.tile(sina_g, (1, rep)), jnp.tile(sinb_g, (1, rep))


def kernel(x, c, w_ada, b_ada, g_norm, ffn1_w12, ffn1_w3, w_in, conv_w, conv_b, b_igate, b_fgate,
           g_qnorm, g_knorm, lambda_qk, g_da_out, g_ml_out, w_out, ffn2_w12, ffn2_w3):
    bsz, s, d = x.shape
    l = 0
    mod = _ada(c, w_ada[l], b_ada[l]).reshape(bsz, 3, 3, d)

    def ffn(xin, sub, w12, w3):
        w12b = w12.astype(BF16)
        return _ffn(xin, mod[:, sub], g_norm[l, sub], w12b[:, :D_FF], w12b[:, D_FF:], w3.astype(BF16))

    x = ffn(x, 0, ffn1_w12[l], ffn1_w3[l])

    wb = w_in[l].astype(BF16)
    o0 = 0
    parts = []
    for width in (DA_WIDTH, DA_WIDTH, DA_WIDTH, 2 * ML_WIDTH, ML_WIDTH, ML_WIDTH, 2 * ML_HEADS):
        parts.append(wb[:, o0:o0 + width])
        o0 += width
    wq, wk, wv, wmqk, wmv, wmo, wif = parts
    wif = jnp.pad(wif, ((0, 0), (0, LANES - 2 * ML_HEADS)))
    groups = DA_WIDTH // DA_QK_DIM
    q_gain = g_qnorm[l] * (DA_QK_DIM ** -0.5)
    gq = jnp.tile(q_gain * LOG2E, groups).reshape(1, DA_WIDTH)
    score_bound = 1.05 * DA_QK_DIM * jnp.max(jnp.abs(q_gain)) * jnp.max(jnp.abs(g_knorm[l]))
    bounded = (score_bound <= SAFE_SCORE_BOUND).astype(jnp.int32).reshape(1)
    gk = jnp.tile(g_knorm[l], groups).reshape(1, DA_WIDTH)
    cos, sina, sinb = _rope_tables(s)
    gid = jnp.arange(DA_WIDTH) // DA_QK_DIM
    bd = (gid[:, None] == gid[None, :]).astype(BF16)
    bif = jnp.concatenate([b_igate[l], b_fgate[l]]).reshape(1, 2 * ML_HEADS)
    da_q, da_k, da_v, ml_q, ml_k, ml_v, ml_o, gates = _inproj(
        x, mod[:, 1], g_norm[l, 1], wq, wk, wv, wmqk, wmv, wmo, wif, gq, gk, cos, sina, sinb, bd,
        conv_w[l], conv_b[l].reshape(1, 2 * ML_WIDTH), bif)

    y_da = _attention(bounded, da_q, da_k, da_v, lambda_qk[l], g_da_out[l])
    y_ml = _mlstm(ml_q, ml_k, ml_v, ml_o, gates, jnp.swapaxes(gates, 1, 2), g_ml_out[l])

    wo = w_out[l].astype(BF16)
    x = _outproj(x, mod[:, 1], y_da, y_ml, wo[:DA_WIDTH], wo[DA_WIDTH:])
    return ffn(x, 2, ffn2_w12[l], ffn2_w3[l])
```

```python
import functools

import jax
import jax.numpy as jnp
from jax import lax
from jax.experimental import pallas as pl
from jax.experimental.pallas import tpu as pltpu

F32 = jnp.float32
BF16 = jnp.bfloat16

D_MODEL = 1024
DA_HEADS = 4
DA_QK_DIM = 64
DA_V_DIM = 2 * DA_QK_DIM
DA_WIDTH = DA_HEADS * DA_V_DIM
ML_HEADS = 4
ML_DIM = 128
ML_WIDTH = ML_HEADS * ML_DIM
ROPE_THETA = 500000.0
ROPE_DIM = DA_QK_DIM // 4
D_FF = 2816
CONV_K = 4
EPS = 1e-6
LAMBDA_INIT = 0.8 - 0.6 * 1.0
LOG2E = 1.4426950408889634

LANES = 128
VMEM_LIMIT = 56 * 1024 * 1024

FFN_TM = 512
FFN_SUB = 256
PROJ_TM = 512
ATT_T = 512
SAFE_SCORE_BOUND = 40.0
ML_CHUNK = 256
GATE_ROWS = 16
CONV_HALO = 8


def _const_spec(shape):
    nd = len(shape)
    return pl.BlockSpec(shape, lambda *_: (0,) * nd, pipeline_mode=pl.Buffered(1))


def _sigmoid(x):
    return 1.0 / (1.0 + jnp.exp(-x))


def _mod_norm(x, g, shift, scale):
    ms = jnp.mean(x * x, axis=-1, keepdims=True)
    return (x * lax.rsqrt(ms + EPS)) * g * (1.0 + scale) + shift


def _ada_kernel(c_ref, w_ref, b_ref, o_ref):
    c = c_ref[...]
    cs = (c * _sigmoid(c)).astype(BF16)
    o_ref[...] = jnp.dot(cs, w_ref[...].astype(BF16), preferred_element_type=F32) + b_ref[...]


def _ada(c, w_ada, b_ada):
    bsz, d = c.shape
    n = w_ada.shape[1]
    tn = 1024
    return pl.pallas_call(
        _ada_kernel,
        grid=(n // tn,),
        in_specs=[pl.BlockSpec((bsz, d), lambda j: (0, 0)),
                  pl.BlockSpec((d, tn), lambda j: (0, j)),
                  pl.BlockSpec((1, tn), lambda j: (0, j))],
        out_specs=pl.BlockSpec((bsz, tn), lambda j: (0, j)),
        out_shape=jax.ShapeDtypeStruct((bsz, n), F32),
        compiler_params=pltpu.CompilerParams(dimension_semantics=("arbitrary",)),
        name="adaln_mod",
    )(c, w_ada, b_ada.reshape(1, n))


def _ffn_body(x, mod_ref, g_ref, w1_ref, w2_ref, w3_ref, o_ref, act_ref):
    mod = mod_ref[0]
    hb = _mod_norm(x, g_ref[...], mod[0:1], mod[1:2]).astype(BF16)
    for c in range(D_FF // FFN_SUB):
        cols = slice(c * FFN_SUB, (c + 1) * FFN_SUB)
        a = jnp.dot(hb, w1_ref[:, cols], preferred_element_type=F32)
        b = jnp.dot(hb, w2_ref[:, cols], preferred_element_type=F32)
        act_ref[:, cols] = (a * _sigmoid(a) * b).astype(BF16)
    y = jnp.dot(act_ref[...], w3_ref[...], preferred_element_type=F32)
    o_ref[0] = x + (0.5 * (1.0 + mod[2:3])) * y


def _ffn_kernel(x_ref, mod_ref, g_ref, w1_ref, w2_ref, w3_ref, o_ref, act_ref):
    _ffn_body(x_ref[0], mod_ref, g_ref, w1_ref, w2_ref, w3_ref, o_ref, act_ref)


def _mix_ffn_kernel(x_ref, mmod_ref, ya_ref, ym_ref, wa_ref, wm_ref,
                    mod_ref, g_ref, w1_ref, w2_ref, w3_ref, o_ref, act_ref):
    y = (jnp.dot(ya_ref[0], wa_ref[...], preferred_element_type=F32)
         + jnp.dot(ym_ref[0], wm_ref[...], preferred_element_type=F32))
    x = x_ref[0] + (1.0 + mmod_ref[0][2:3]) * y
    _ffn_body(x, mod_ref, g_ref, w1_ref, w2_ref, w3_ref, o_ref, act_ref)


def _ffn(x, mod3, g, w1, w2, w3, mix=None):
    bsz, s, d = x.shape
    tm = FFN_TM
    tok_spec = lambda width: pl.BlockSpec((1, tm, width), lambda b, i: (b, i, 0))
    mod_spec = pl.BlockSpec((1, 3, d), lambda b, i: (b, 0, 0))
    ffn_specs = [mod_spec, _const_spec((1, d)),
                 _const_spec(w1.shape), _const_spec(w2.shape), _const_spec(w3.shape)]
    ffn_args = (mod3, g.reshape(1, d), w1, w2, w3)
    if mix is None:
        body, in_specs, args = _ffn_kernel, [tok_spec(d)] + ffn_specs, (x,) + ffn_args
    else:
        mmod3, y_da, y_ml, w_a, w_m = mix
        body = _mix_ffn_kernel
        in_specs = [tok_spec(d), mod_spec, tok_spec(DA_WIDTH), tok_spec(ML_WIDTH),
                    _const_spec(w_a.shape), _const_spec(w_m.shape)] + ffn_specs
        args = (x, mmod3, y_da, y_ml, w_a, w_m) + ffn_args
    return pl.pallas_call(
        body,
        grid=(bsz, s // tm),
        in_specs=in_specs,
        out_specs=tok_spec(d),
        out_shape=jax.ShapeDtypeStruct(x.shape, F32),
        scratch_shapes=[pltpu.VMEM((tm, D_FF), BF16)],
        compiler_params=pltpu.CompilerParams(
            dimension_semantics=("arbitrary", "arbitrary"), vmem_limit_bytes=VMEM_LIMIT),
        name="ffn" if mix is None else "mix_ffn",
    )(*args)


def _group_norm_rope(u, gvec, bd, cos, sina, sinb):
    x2 = u * u
    hi = x2.astype(BF16)
    lo = (x2 - hi.astype(F32)).astype(BF16)
    ssq = (jnp.dot(hi, bd, preferred_element_type=F32) + jnp.dot(lo, bd, preferred_element_type=F32))
    xn = (u * lax.rsqrt(ssq * (1.0 / DA_QK_DIM) + EPS)) * gvec
    outs = []
    for h in range(DA_HEADS):
        xh = xn[:, h * LANES:(h + 1) * LANES]
        up = pltpu.roll(xh, LANES - ROPE_DIM // 2, 1)
        dn = pltpu.roll(xh, ROPE_DIM // 2, 1)
        outs.append(xh * cos + up * sina + dn * sinb)
    return jnp.concatenate(outs, axis=1)


def _chunk_scan(x, op, identity):
    pos = lax.broadcasted_iota(jnp.int32, x.shape, 1) & (ML_CHUNK - 1)
    d = 1
    while d < ML_CHUNK:
        x = op(x, jnp.where(pos >= d, pltpu.roll(x, d, 1), identity))
        d *= 2
    return x


def _inproj_kernel(x_ref, mod_ref, g_ref, wq_ref, wk_ref, wv_ref, wmqk_ref, wmv_ref, wmo_ref, wif_ref,
                   gq_ref, gk_ref, cos_ref, sina_ref, sinb_ref, bd_ref, cw_ref, cb_ref, bif_ref,
                   q_out, k_out, v_out, mq_out, mk_out, mv_out, mo_out, gate_out, ext_ref):
    tm = x_ref.shape[1]
    si = pl.program_id(1)
    mod = mod_ref[0]
    hb = _mod_norm(x_ref[0], g_ref[...], mod[0:1], mod[1:2]).astype(BF16)

    nh = ML_HEADS
    pre = lax.dot_general(wif_ref[...], hb, (((1,), (1,)), ((), ())),
                          preferred_element_type=F32)[0:2 * nh] + bif_ref[...]
    zf = pre[nh:2 * nh]
    log_f = jnp.minimum(zf, 0.0) - jnp.log1p(jnp.exp(-jnp.abs(zf)))
    b = _chunk_scan(log_f, jnp.add, 0.0)
    c = pre[0:nh] - b
    gate_out[0] = jnp.concatenate([c, b, _chunk_scan(c, jnp.maximum, -jnp.inf), jnp.zeros_like(c)], axis=0)

    cos, sina, sinb = cos_ref[...], sina_ref[...], sinb_ref[...]
    bd = bd_ref[...]
    uq = jnp.dot(hb, wq_ref[...], preferred_element_type=F32)
    q_out[0] = _group_norm_rope(uq, gq_ref[...], bd, cos, sina, sinb).astype(BF16)
    uk = jnp.dot(hb, wk_ref[...], preferred_element_type=F32)
    k_out[0] = _group_norm_rope(uk, gk_ref[...], bd, cos, sina, sinb).astype(BF16)
    v_out[0] = jnp.dot(hb, wv_ref[...], preferred_element_type=F32).astype(BF16)

    ext_ref[0:CONV_HALO, :] = jnp.where(si == 0, 0.0, ext_ref[0:CONV_HALO, :])
    uqk = jnp.dot(hb, wmqk_ref[...], preferred_element_type=F32)
    ext_ref[CONV_HALO:CONV_HALO + tm, :] = uqk
    cw = cw_ref[...]
    acc = cb_ref[...]
    for j in range(CONV_K - 1):
        off = CONV_HALO - (CONV_K - 1) + j
        acc = acc + ext_ref[off:off + tm, :] * cw[j:j + 1]
    acc = acc + uqk * cw[CONV_K - 1:CONV_K]
    ext_ref[0:CONV_HALO, :] = uqk[tm - CONV_HALO:tm, :]
    qk = acc * _sigmoid(acc)
    mq_out[0] = (qk[:, :ML_WIDTH] * (ML_DIM ** -0.5)).astype(BF16)
    mk_out[0] = qk[:, ML_WIDTH:].astype(BF16)

    mv_out[0] = jnp.dot(hb, wmv_ref[...], preferred_element_type=F32).astype(BF16)
    mo_out[0] = _sigmoid(jnp.dot(hb, wmo_ref[...], preferred_element_type=F32)).astype(BF16)


def _inproj(x, mod3, g, wq, wk, wv, wmqk, wmv, wmo, wif, gq, gk, cos, sina, sinb, bd, cw, cb, bif):
    bsz, s, d = x.shape
    tm = PROJ_TM
    tok = lambda width, dt: jax.ShapeDtypeStruct((bsz, s, width), dt)
    tok_spec = lambda width: pl.BlockSpec((1, tm, width), lambda b, i: (b, i, 0))
    tab_spec = pl.BlockSpec((tm, LANES), lambda b, i: (i, 0))
    return pl.pallas_call(
        _inproj_kernel,
        grid=(bsz, s // tm),
        in_specs=[tok_spec(d),
                  pl.BlockSpec((1, 3, d), lambda b, i: (b, 0, 0)),
                  _const_spec((1, d)),
                  _const_spec(wq.shape), _const_spec(wk.shape), _const_spec(wv.shape),
                  _const_spec(wmqk.shape), _const_spec(wmv.shape), _const_spec(wmo.shape),
                  _const_spec(wif.shape),
                  _const_spec(gq.shape), _const_spec(gk.shape),
                  tab_spec, tab_spec, tab_spec,
                  _const_spec(bd.shape), _const_spec(cw.shape), _const_spec(cb.shape),
                  _const_spec(bif.shape)],
        out_specs=[tok_spec(DA_WIDTH), tok_spec(DA_WIDTH), tok_spec(DA_WIDTH),
                   tok_spec(ML_WIDTH), tok_spec(ML_WIDTH), tok_spec(ML_WIDTH), tok_spec(ML_WIDTH),
                   pl.BlockSpec((1, GATE_ROWS, tm), lambda b, i: (b, 0, i))],
        out_shape=[tok(DA_WIDTH, BF16), tok(DA_WIDTH, BF16), tok(DA_WIDTH, BF16),
                   tok(ML_WIDTH, BF16), tok(ML_WIDTH, BF16), tok(ML_WIDTH, BF16), tok(ML_WIDTH, BF16),
                   jax.ShapeDtypeStruct((bsz, GATE_ROWS, s), F32)],
        scratch_shapes=[pltpu.VMEM((tm + CONV_HALO, 2 * ML_WIDTH), F32)],
        compiler_params=pltpu.CompilerParams(
            dimension_semantics=("arbitrary", "arbitrary"), vmem_limit_bytes=VMEM_LIMIT),
        name="in_proj",
    )(x, mod3, g.reshape(1, d), wq, wk, wv, wmqk, wmv, wmo, wif, gq, gk, cos, sina, sinb, bd, cw, cb, bif)


def _attn_kernel(bounded_ref, q_ref, k_ref, v_ref, lam_ref, gout_ref, o_ref, qs_ref, m_ref, l_ref, acc_ref):
    t = ATT_T
    qi = pl.program_id(2)

    q = q_ref[0]
    lane = lax.broadcasted_iota(jnp.int32, q.shape, 1)
    zero = jnp.zeros_like(q)
    qs_ref[0:t, :] = jnp.where(lane < DA_QK_DIM, q, zero)
    qs_ref[t:2 * t, :] = jnp.where(lane >= DA_QK_DIM, q, zero)
    l_ref[...] = jnp.zeros(l_ref.shape, F32)
    acc_ref[...] = jnp.zeros(acc_ref.shape, F32)

    def block(kb, diagonal, stabilised):
        start = pl.multiple_of(kb * t, t)
        ks = k_ref[0, pl.ds(start, t), :]
        vs = v_ref[0, pl.ds(start, t), :]
        s = lax.dot_general(qs_ref[...], ks, (((1,), (1,)), ((), ())), preferred_element_type=F32)
        if diagonal:
            row = lax.broadcasted_iota(jnp.int32, s.shape, 0) & (t - 1)
            col = lax.broadcasted_iota(jnp.int32, s.shape, 1)
            s = jnp.where(col <= row, s, -jnp.inf)
        if stabilised:
            m_prev = m_ref[...]
            m_new = jnp.maximum(m_prev, jnp.max(s, axis=1, keepdims=True))
            alpha = jnp.exp2(m_prev - m_new)
            m_ref[...] = m_new
            s = s - m_new
        p = jnp.exp2(s)
        lp = p[:, 0:LANES]
        for c in range(1, t // LANES):
            lp = lp + p[:, c * LANES:(c + 1) * LANES]
        pv = jnp.dot(p.astype(BF16), vs, preferred_element_type=F32)
        if stabilised:
            l_ref[...] = alpha * l_ref[...] + lp
            acc_ref[...] = alpha * acc_ref[...] + pv
        else:
            l_ref[...] += lp
            acc_ref[...] += pv

    def run(stabilised):
        def body(kb, carry):
            block(kb, False, stabilised)
            return carry

        lax.fori_loop(0, qi, body, 0)
        block(qi, True, stabilised)

    @pl.when(bounded_ref[0] != 0)
    def _():
        run(False)

    @pl.when(bounded_ref[0] == 0)
    def _():
        m_ref[...] = jnp.full(m_ref.shape, -jnp.inf, F32)
        run(True)

    o = acc_ref[...] / jnp.sum(l_ref[...], axis=1, keepdims=True)
    lv = lam_ref[...]
    lam = (jnp.exp(jnp.sum(lv[0:1] * lv[1:2], axis=1, keepdims=True))
           - jnp.exp(jnp.sum(lv[2:3] * lv[3:4], axis=1, keepdims=True)) + LAMBDA_INIT)
    od = o[0:t] - lam * o[t:2 * t]
    ms = jnp.mean(od * od, axis=-1, keepdims=True)
    o_ref[0] = ((od * lax.rsqrt(ms + EPS)) * gout_ref[...] * (1.0 - LAMBDA_INIT)).astype(BF16)


def _attention(bounded, q, k, v, lam_vecs, g_out):
    bsz, s, _ = q.shape
    t = ATT_T
    blk = pl.BlockSpec((1, t, DA_V_DIM), lambda b, h, i: (b, i, h))
    full = pl.BlockSpec((1, s, DA_V_DIM), lambda b, h, i: (b, 0, h))
    return pl.pallas_call(
        _attn_kernel,
        grid=(bsz, DA_HEADS, s // t),
        in_specs=[pl.BlockSpec(memory_space=pltpu.SMEM),
                  blk, full, full, _const_spec(lam_vecs.shape), _const_spec((1, DA_V_DIM))],
        out_specs=blk,
        out_shape=jax.ShapeDtypeStruct((bsz, s, DA_WIDTH), BF16),
        scratch_shapes=[pltpu.VMEM((2 * t, DA_V_DIM), BF16),
                        pltpu.VMEM((2 * t, 1), F32),
                        pltpu.VMEM((2 * t, LANES), F32),
                        pltpu.VMEM((2 * t, DA_V_DIM), F32)],
        compiler_params=pltpu.CompilerParams(
            dimension_semantics=("arbitrary", "arbitrary", "arbitrary"), vmem_limit_bytes=VMEM_LIMIT),
        name="diff_attention",
    )(bounded, q, k, v, lam_vecs, g_out.reshape(1, DA_V_DIM))


def _mlstm_kernel(q_ref, k_ref, v_ref, o_ref, gr_ref, gout_ref, y_ref, c_ref, m_ref):
    ln = ML_CHUNK
    nh = ML_HEADS

    @pl.when(pl.program_id(1) == 0)
    def _():
        c_ref[...] = jnp.zeros(c_ref.shape, F32)
        m_ref[...] = jnp.zeros(m_ref.shape, F32)

    gr = gr_ref[0]
    c, b = gr[0:nh], gr[nh:2 * nh]
    m_prev = jnp.concatenate([m_ref[0:nh, :]] * (ln // LANES), axis=1)
    m_run = jnp.maximum(gr[2 * nh:3 * nh], m_prev)
    m_last = m_run[:, ln - 1:ln]
    e_inter = jnp.exp(m_prev - m_run)
    e_floor = jnp.exp(-(b + m_run))
    w_in = jnp.exp(c - m_last)
    decay = e_inter[:, ln - 1:ln]
    m_ref[0:nh, :] = jnp.broadcast_to(b[:, ln - 1:ln] + m_last, (nh, LANES))
    cols = jnp.concatenate([m_run, e_inter, e_floor, w_in], axis=0).T

    row = lax.broadcasted_iota(jnp.int32, (ln, ln), 0)
    col = lax.broadcasted_iota(jnp.int32, (ln, ln), 1)
    lower = col <= row
    ones_col = (lax.broadcasted_iota(jnp.int32, (ln, ML_DIM), 1) == 0).astype(BF16)

    for h in range(nh):
        lanes = slice(h * ML_DIM, (h + 1) * ML_DIM)
        q = q_ref[0, :, lanes]
        k = k_ref[0, :, lanes]
        v_aug = jnp.concatenate([v_ref[0, :, lanes], ones_col], axis=1)
        m_run_col = cols[:, h:h + 1]
        e_inter_col = cols[:, nh + h:nh + h + 1]
        e_floor_col = cols[:, 2 * nh + h:2 * nh + h + 1]
        w_in_col = cols[:, 3 * nh + h:3 * nh + h + 1]

        s = lax.dot_general(q, k, (((1,), (1,)), ((), ())), preferred_element_type=F32)
        w = (s * jnp.exp(jnp.where(lower, c[h:h + 1, :] - m_run_col, -jnp.inf))).astype(BF16)
        state = c_ref[h]
        full = (e_inter_col * jnp.dot(q, state.astype(BF16), preferred_element_type=F32)
                + jnp.dot(w, v_aug, preferred_element_type=F32))
        num = full[:, :ML_DIM]
        den = full[:, ML_DIM:ML_DIM + 1]
        hh = num / jnp.maximum(jnp.abs(den), e_floor_col)
        hn = hh * lax.rsqrt(jnp.mean(hh * hh, axis=-1, keepdims=True) + EPS)
        y_ref[0, :, lanes] = (o_ref[0, :, lanes].astype(F32) * (hn * gout_ref[:, lanes])).astype(BF16)

        kw = (k.astype(F32) * w_in_col).astype(BF16)
        upd = lax.dot_general(kw, v_aug, (((0,), (0,)), ((), ())), preferred_element_type=F32)
        c_ref[h] = decay[h:h + 1, :] * state + upd


def _mlstm(q, k, v, o, gates_row, g_out):
    bsz, s, _ = q.shape
    ln = ML_CHUNK
    tok = pl.BlockSpec((1, ln, ML_WIDTH), lambda b, c: (b, c, 0))
    return pl.pallas_call(
        _mlstm_kernel,
        grid=(bsz, s // ln),
        in_specs=[tok, tok, tok, tok,
                  pl.BlockSpec((1, GATE_ROWS, ln), lambda b, c: (b, 0, c)),
                  _const_spec((1, ML_WIDTH))],
        out_specs=tok,
        out_shape=jax.ShapeDtypeStruct((bsz, s, ML_WIDTH), BF16),
        scratch_shapes=[pltpu.VMEM((ML_HEADS, ML_DIM, 2 * ML_DIM), F32),
                        pltpu.VMEM((8, LANES), F32)],
        compiler_params=pltpu.CompilerParams(
            dimension_semantics=("arbitrary", "arbitrary"), vmem_limit_bytes=VMEM_LIMIT),
        name="mlstm",
    )(q, k, v, o, gates_row, g_out.reshape(1, ML_WIDTH))


def _rope_tables(s):
    half = ROPE_DIM // 2
    pos = jnp.arange(s, dtype=F32)
    inv_freq = ROPE_THETA ** (-jnp.arange(0, ROPE_DIM, 2, dtype=F32) / ROPE_DIM)
    ang = pos[:, None] * inv_freq[None, :]
    cos, sin = jnp.cos(ang), jnp.sin(ang)
    rest = DA_QK_DIM - ROPE_DIM
    cos_g = jnp.concatenate([cos, cos, jnp.ones((s, rest), F32)], axis=1)
    sina_g = jnp.concatenate([-sin, jnp.zeros((s, DA_QK_DIM - half), F32)], axis=1)
    sinb_g = jnp.concatenate([jnp.zeros((s, half), F32), sin, jnp.zeros((s, rest), F32)], axis=1)
    rep = LANES // DA_QK_DIM
    return jnp.tile(cos_g, (1, rep)), jnp.tile(sina_g, (1, rep)), jnp.tile(sinb_g, (1, rep))


def kernel(x, c, w_ada, b_ada, g_norm, ffn1_w12, ffn1_w3, w_in, conv_w, conv_b, b_igate, b_fgate,
           g_qnorm, g_knorm, lambda_qk, g_da_out, g_ml_out, w_out, ffn2_w12, ffn2_w3):
    bsz, s, d = x.shape
    l = 0
    mod = _ada(c, w_ada[l], b_ada[l]).reshape(bsz, 3, 3, d)

    def ffn(xin, sub, w12, w3, mix=None):
        w12b = w12.astype(BF16)
        return _ffn(xin, mod[:, sub], g_norm[l, sub], w12b[:, :D_FF], w12b[:, D_FF:], w3.astype(BF16), mix)

    x = ffn(x, 0, ffn1_w12[l], ffn1_w3[l])

    wb = w_in[l].astype(BF16)
    o0 = 0
    parts = []
    for width in (DA_WIDTH, DA_WIDTH, DA_WIDTH, 2 * ML_WIDTH, ML_WIDTH, ML_WIDTH, 2 * ML_HEADS):
        parts.append(wb[:, o0:o0 + width])
        o0 += width
    wq, wk, wv, wmqk, wmv, wmo, wif = parts
    wif = jnp.pad(wif.T, ((0, 2 * ML_HEADS), (0, 0)))
    groups = DA_WIDTH // DA_QK_DIM
    q_gain = g_qnorm[l] * (DA_QK_DIM ** -0.5)
    gq = jnp.tile(q_gain * LOG2E, groups).reshape(1, DA_WIDTH)
    score_bound = 1.05 * DA_QK_DIM * jnp.max(jnp.abs(q_gain)) * jnp.max(jnp.abs(g_knorm[l]))
    bounded = (score_bound <= SAFE_SCORE_BOUND).astype(jnp.int32).reshape(1)
    gk = jnp.tile(g_knorm[l], groups).reshape(1, DA_WIDTH)
    cos, sina, sinb = _rope_tables(s)
    gid = jnp.arange(DA_WIDTH) // DA_QK_DIM
    bd = (gid[:, None] == gid[None, :]).astype(BF16)
    bif = jnp.concatenate([b_igate[l], b_fgate[l]]).reshape(2 * ML_HEADS, 1)
    da_q, da_k, da_v, ml_q, ml_k, ml_v, ml_o, gates = _inproj(
        x, mod[:, 1], g_norm[l, 1], wq, wk, wv, wmqk, wmv, wmo, wif, gq, gk, cos, sina, sinb, bd,
        conv_w[l], conv_b[l].reshape(1, 2 * ML_WIDTH), bif)

    y_da = _attention(bounded, da_q, da_k, da_v, lambda_qk[l], g_da_out[l])
    y_ml = _mlstm(ml_q, ml_k, ml_v, ml_o, gates, g_ml_out[l])

    wo = w_out[l].astype(BF16)
    return ffn(x, 2, ffn2_w12[l], ffn2_w3[l], mix=(mod[:, 1], y_da, y_ml, wo[:DA_WIDTH], wo[DA_WIDTH:]))
```

```python
import functools

import jax
import jax.numpy as jnp
from jax import lax
from jax.experimental import pallas as pl
from jax.experimental.pallas import tpu as pltpu

F32 = jnp.float32
BF16 = jnp.bfloat16

D_MODEL = 1024
DA_HEADS = 4
DA_QK_DIM = 64
DA_V_DIM = 2 * DA_QK_DIM
DA_WIDTH = DA_HEADS * DA_V_DIM
ML_HEADS = 4
ML_DIM = 128
ML_WIDTH = ML_HEADS * ML_DIM
ROPE_THETA = 500000.0
ROPE_DIM = DA_QK_DIM // 4
D_FF = 2816
CONV_K = 4
EPS = 1e-6
LAMBDA_INIT = 0.8 - 0.6 * 1.0
LOG2E = 1.4426950408889634

LANES = 128
VMEM_LIMIT = 56 * 1024 * 1024

FFN_TM = 512
FFN_SUB = 256
PROJ_TM = 512
ATT_T = 512
SAFE_SCORE_BOUND = 40.0
ML_CHUNK = 256
GATE_ROWS = 16
CONV_HALO = 8


def _const_spec(shape):
    nd = len(shape)
    return pl.BlockSpec(shape, lambda *_: (0,) * nd, pipeline_mode=pl.Buffered(1))


def _sigmoid(x):
    return 1.0 / (1.0 + jnp.exp(-x))


def _mod_norm(x, g, shift, scale):
    ms = jnp.mean(x * x, axis=-1, keepdims=True)
    return (x * lax.rsqrt(ms + EPS)) * (g * (1.0 + scale)) + shift


def _ada_kernel(c_ref, w_ref, b_ref, o_ref):
    c = c_ref[...]
    cs = (c * _sigmoid(c)).astype(BF16)
    o_ref[...] = jnp.dot(cs, w_ref[...].astype(BF16), preferred_element_type=F32) + b_ref[...]


def _ada(c, w_ada, b_ada):
    bsz, d = c.shape
    n = w_ada.shape[1]
    tn = 1024
    return pl.pallas_call(
        _ada_kernel,
        grid=(n // tn,),
        in_specs=[pl.BlockSpec((bsz, d), lambda j: (0, 0)),
                  pl.BlockSpec((d, tn), lambda j: (0, j)),
                  pl.BlockSpec((1, tn), lambda j: (0, j))],
        out_specs=pl.BlockSpec((bsz, tn), lambda j: (0, j)),
        out_shape=jax.ShapeDtypeStruct((bsz, n), F32),
        compiler_params=pltpu.CompilerParams(dimension_semantics=("arbitrary",)),
        name="adaln_mod",
    )(c, w_ada, b_ada.reshape(1, n))


def _ffn_body(x, mod_ref, g_ref, w1_ref, w2_ref, w3_ref, o_ref, act_ref):
    mod = mod_ref[0]
    hb = _mod_norm(x, g_ref[...], mod[0:1], mod[1:2]).astype(BF16)
    for c in range(D_FF // FFN_SUB):
        cols = slice(c * FFN_SUB, (c + 1) * FFN_SUB)
        a = jnp.dot(hb, w1_ref[:, cols], preferred_element_type=F32)
        b = jnp.dot(hb, w2_ref[:, cols], preferred_element_type=F32)
        act_ref[:, cols] = (a * _sigmoid(a) * b).astype(BF16)
    y = jnp.dot(act_ref[...], w3_ref[...], preferred_element_type=F32)
    o_ref[0] = x + (0.5 * (1.0 + mod[2:3])) * y


def _ffn_kernel(x_ref, mod_ref, g_ref, w1_ref, w2_ref, w3_ref, o_ref, act_ref):
    _ffn_body(x_ref[0], mod_ref, g_ref, w1_ref, w2_ref, w3_ref, o_ref, act_ref)


def _mix_ffn_kernel(x_ref, mmod_ref, ya_ref, ym_ref, wa_ref, wm_ref,
                    mod_ref, g_ref, w1_ref, w2_ref, w3_ref, o_ref, act_ref):
    y = (jnp.dot(ya_ref[0], wa_ref[...], preferred_element_type=F32)
         + jnp.dot(ym_ref[0], wm_ref[...], preferred_element_type=F32))
    x = x_ref[0] + (1.0 + mmod_ref[0][2:3]) * y
    _ffn_body(x, mod_ref, g_ref, w1_ref, w2_ref, w3_ref, o_ref, act_ref)


def _ffn(x, mod3, g, w1, w2, w3, mix=None):
    bsz, s, d = x.shape
    tm = FFN_TM
    tok_spec = lambda width: pl.BlockSpec((1, tm, width), lambda b, i: (b, i, 0))
    mod_spec = pl.BlockSpec((1, 3, d), lambda b, i: (b, 0, 0))
    ffn_specs = [mod_spec, _const_spec((1, d)),
                 _const_spec(w1.shape), _const_spec(w2.shape), _const_spec(w3.shape)]
    ffn_args = (mod3, g.reshape(1, d), w1, w2, w3)
    if mix is None:
        body, in_specs, args = _ffn_kernel, [tok_spec(d)] + ffn_specs, (x,) + ffn_args
    else:
        mmod3, y_da, y_ml, w_a, w_m = mix
        body = _mix_ffn_kernel
        in_specs = [tok_spec(d), mod_spec, tok_spec(DA_WIDTH), tok_spec(ML_WIDTH),
                    _const_spec(w_a.shape), _const_spec(w_m.shape)] + ffn_specs
        args = (x, mmod3, y_da, y_ml, w_a, w_m) + ffn_args
    return pl.pallas_call(
        body,
        grid=(bsz, s // tm),
        in_specs=in_specs,
        out_specs=tok_spec(d),
        out_shape=jax.ShapeDtypeStruct(x.shape, F32),
        scratch_shapes=[pltpu.VMEM((tm, D_FF), BF16)],
        compiler_params=pltpu.CompilerParams(
            dimension_semantics=("arbitrary", "arbitrary"), vmem_limit_bytes=VMEM_LIMIT),
        name="ffn" if mix is None else "mix_ffn",
    )(*args)


def _group_norm_rope(u, gvec, bd, cos, sina, sinb):
    ssq = jnp.dot((u * u).astype(BF16), bd, preferred_element_type=F32)
    xn = (u * lax.rsqrt(ssq * (1.0 / DA_QK_DIM) + EPS)) * gvec
    outs = []
    for h in range(DA_HEADS):
        xh = xn[:, h * LANES:(h + 1) * LANES]
        up = pltpu.roll(xh, LANES - ROPE_DIM // 2, 1)
        dn = pltpu.roll(xh, ROPE_DIM // 2, 1)
        outs.append(xh * cos + up * sina + dn * sinb)
    return jnp.concatenate(outs, axis=1)


def _chunk_scan(x, op, identity):
    pos = lax.broadcasted_iota(jnp.int32, x.shape, 1) & (ML_CHUNK - 1)
    d = 1
    while d < ML_CHUNK:
        x = op(x, jnp.where(pos >= d, pltpu.roll(x, d, 1), identity))
        d *= 2
    return x


def _inproj_kernel(x_ref, mod_ref, g_ref, wq_ref, wk_ref, wv_ref, wmqk_ref, wmv_ref, wmo_ref, wif_ref,
                   gq_ref, gk_ref, cos_ref, sina_ref, sinb_ref, bd_ref, cw_ref, cb_ref, bif_ref,
                   q_out, k_out, v_out, mq_out, mk_out, mv_out, mo_out, gate_out, ext_ref):
    tm = x_ref.shape[1]
    si = pl.program_id(1)
    mod = mod_ref[0]
    hb = _mod_norm(x_ref[0], g_ref[...], mod[0:1], mod[1:2]).astype(BF16)

    nh = ML_HEADS
    pre = lax.dot_general(wif_ref[...], hb, (((1,), (1,)), ((), ())),
                          preferred_element_type=F32)[0:2 * nh] + bif_ref[...]
    zf = pre[nh:2 * nh]
    log_f = jnp.minimum(zf, 0.0) - jnp.log1p(jnp.exp(-jnp.abs(zf)))
    b = _chunk_scan(log_f, jnp.add, 0.0)
    c = pre[0:nh] - b
    gate_out[0] = jnp.concatenate([c, b, _chunk_scan(c, jnp.maximum, -jnp.inf), jnp.zeros_like(c)], axis=0)

    cos, sina, sinb = cos_ref[...], sina_ref[...], sinb_ref[...]
    bd = bd_ref[...]
    uq = jnp.dot(hb, wq_ref[...], preferred_element_type=F32)
    q_out[0] = _group_norm_rope(uq, gq_ref[...], bd, cos, sina, sinb).astype(BF16)
    uk = jnp.dot(hb, wk_ref[...], preferred_element_type=F32)
    k_out[0] = _group_norm_rope(uk, gk_ref[...], bd, cos, sina, sinb).astype(BF16)
    v_out[0] = jnp.dot(hb, wv_ref[...], preferred_element_type=F32).astype(BF16)

    ext_ref[0:CONV_HALO, :] = jnp.where(si == 0, 0.0, ext_ref[0:CONV_HALO, :])
    uqk = jnp.dot(hb, wmqk_ref[...], preferred_element_type=F32)
    ext_ref[CONV_HALO:CONV_HALO + tm, :] = uqk
    cw = cw_ref[...]
    acc = cb_ref[...]
    for j in range(CONV_K - 1):
        off = CONV_HALO - (CONV_K - 1) + j
        acc = acc + ext_ref[off:off + tm, :] * cw[j:j + 1]
    acc = acc + uqk * cw[CONV_K - 1:CONV_K]
    ext_ref[0:CONV_HALO, :] = uqk[tm - CONV_HALO:tm, :]
    qk = acc * _sigmoid(acc)
    mq_out[0] = (qk[:, :ML_WIDTH] * (ML_DIM ** -0.5)).astype(BF16)
    mk_out[0] = qk[:, ML_WIDTH:].astype(BF16)

    mv_out[0] = lax.dot_general(wmv_ref[...], hb, (((1,), (1,)), ((), ())),
                                preferred_element_type=F32).astype(BF16)
    mo_out[0] = _sigmoid(jnp.dot(hb, wmo_ref[...], preferred_element_type=F32)).astype(BF16)


def _inproj(x, mod3, g, wq, wk, wv, wmqk, wmv, wmo, wif, gq, gk, cos, sina, sinb, bd, cw, cb, bif):
    bsz, s, d = x.shape
    tm = PROJ_TM
    tok = lambda width, dt: jax.ShapeDtypeStruct((bsz, s, width), dt)
    tok_spec = lambda width: pl.BlockSpec((1, tm, width), lambda b, i: (b, i, 0))
    tab_spec = pl.BlockSpec((tm, LANES), lambda b, i: (i, 0))
    return pl.pallas_call(
        _inproj_kernel,
        grid=(bsz, s // tm),
        in_specs=[tok_spec(d),
                  pl.BlockSpec((1, 3, d), lambda b, i: (b, 0, 0)),
                  _const_spec((1, d)),
                  _const_spec(wq.shape), _const_spec(wk.shape), _const_spec(wv.shape),
                  _const_spec(wmqk.shape), _const_spec(wmv.shape), _const_spec(wmo.shape),
                  _const_spec(wif.shape),
                  _const_spec(gq.shape), _const_spec(gk.shape),
                  tab_spec, tab_spec, tab_spec,
                  _const_spec(bd.shape), _const_spec(cw.shape), _const_spec(cb.shape),
                  _const_spec(bif.shape)],
        out_specs=[tok_spec(DA_WIDTH), tok_spec(DA_WIDTH), tok_spec(DA_WIDTH),
                   tok_spec(ML_WIDTH), tok_spec(ML_WIDTH),
                   pl.BlockSpec((1, ML_WIDTH, tm), lambda b, i: (b, 0, i)),
                   tok_spec(ML_WIDTH),
                   pl.BlockSpec((1, GATE_ROWS, tm), lambda b, i: (b, 0, i))],
        out_shape=[tok(DA_WIDTH, BF16), tok(DA_WIDTH, BF16), tok(DA_WIDTH, BF16),
                   tok(ML_WIDTH, BF16), tok(ML_WIDTH, BF16),
                   jax.ShapeDtypeStruct((bsz, ML_WIDTH, s), BF16),
                   tok(ML_WIDTH, BF16),
                   jax.ShapeDtypeStruct((bsz, GATE_ROWS, s), F32)],
        scratch_shapes=[pltpu.VMEM((tm + CONV_HALO, 2 * ML_WIDTH), F32)],
        compiler_params=pltpu.CompilerParams(
            dimension_semantics=("arbitrary", "arbitrary"), vmem_limit_bytes=VMEM_LIMIT),
        name="in_proj",
    )(x, mod3, g.reshape(1, d), wq, wk, wv, wmqk, wmv, wmo, wif, gq, gk, cos, sina, sinb, bd, cw, cb, bif)


def _attn_kernel(bounded_ref, q_ref, k_ref, v_ref, lam_ref, gout_ref, o_ref, qs_ref, m_ref, l_ref, acc_ref):
    t = ATT_T
    qi = pl.program_id(2)

    q = q_ref[0]
    lane = lax.broadcasted_iota(jnp.int32, q.shape, 1)
    zero = jnp.zeros_like(q)
    qs_ref[0:t, :] = jnp.where(lane < DA_QK_DIM, q, zero)
    qs_ref[t:2 * t, :] = jnp.where(lane >= DA_QK_DIM, q, zero)
    l_ref[...] = jnp.zeros(l_ref.shape, F32)
    acc_ref[...] = jnp.zeros(acc_ref.shape, F32)

    def block(kb, diagonal, stabilised):
        start = pl.multiple_of(kb * t, t)
        ks = k_ref[0, pl.ds(start, t), :]
        vs = v_ref[0, pl.ds(start, t), :]
        s = lax.dot_general(qs_ref[...], ks, (((1,), (1,)), ((), ())), preferred_element_type=F32)
        if diagonal:
            row = lax.broadcasted_iota(jnp.int32, s.shape, 0) & (t - 1)
            col = lax.broadcasted_iota(jnp.int32, s.shape, 1)
            s = jnp.where(col <= row, s, -jnp.inf)
        if stabilised:
            m_prev = m_ref[...]
            m_new = jnp.maximum(m_prev, jnp.max(s, axis=1, keepdims=True))
            alpha = jnp.exp2(m_prev - m_new)
            m_ref[...] = m_new
            s = s - m_new
        p = jnp.exp2(s)
        lp = p[:, 0:LANES]
        for c in range(1, t // LANES):
            lp = lp + p[:, c * LANES:(c + 1) * LANES]
        pv = jnp.dot(p.astype(BF16), vs, preferred_element_type=F32)
        if stabilised:
            l_ref[...] = alpha * l_ref[...] + lp
            acc_ref[...] = alpha * acc_ref[...] + pv
        else:
            l_ref[...] += lp
            acc_ref[...] += pv

    def run(stabilised):
        def body(kb, carry):
            block(kb, False, stabilised)
            return carry

        lax.fori_loop(0, qi, body, 0)
        block(qi, True, stabilised)

    @pl.when(bounded_ref[0] != 0)
    def _():
        run(False)

    @pl.when(bounded_ref[0] == 0)
    def _():
        m_ref[...] = jnp.full(m_ref.shape, -jnp.inf, F32)
        run(True)

    o = acc_ref[...] / jnp.sum(l_ref[...], axis=1, keepdims=True)
    lv = lam_ref[...]
    lam = (jnp.exp(jnp.sum(lv[0:1] * lv[1:2], axis=1, keepdims=True))
           - jnp.exp(jnp.sum(lv[2:3] * lv[3:4], axis=1, keepdims=True)) + LAMBDA_INIT)
    od = o[0:t] - lam * o[t:2 * t]
    ms = jnp.mean(od * od, axis=-1, keepdims=True)
    o_ref[0] = ((od * lax.rsqrt(ms + EPS)) * gout_ref[...] * (1.0 - LAMBDA_INIT)).astype(BF16)


def _attention(bounded, q, k, v, lam_vecs, g_out):
    bsz, s, _ = q.shape
    t = ATT_T
    blk = pl.BlockSpec((1, t, DA_V_DIM), lambda b, h, i: (b, i, h))
    full = pl.BlockSpec((1, s, DA_V_DIM), lambda b, h, i: (b, 0, h))
    return pl.pallas_call(
        _attn_kernel,
        grid=(bsz, DA_HEADS, s // t),
        in_specs=[pl.BlockSpec(memory_space=pltpu.SMEM),
                  blk, full, full, _const_spec(lam_vecs.shape), _const_spec((1, DA_V_DIM))],
        out_specs=blk,
        out_shape=jax.ShapeDtypeStruct((bsz, s, DA_WIDTH), BF16),
        scratch_shapes=[pltpu.VMEM((2 * t, DA_V_DIM), BF16),
                        pltpu.VMEM((2 * t, 1), F32),
                        pltpu.VMEM((2 * t, LANES), F32),
                        pltpu.VMEM((2 * t, DA_V_DIM), F32)],
        compiler_params=pltpu.CompilerParams(
            dimension_semantics=("arbitrary", "arbitrary", "arbitrary"), vmem_limit_bytes=VMEM_LIMIT),
        name="diff_attention",
    )(bounded, q, k, v, lam_vecs, g_out.reshape(1, DA_V_DIM))


def _mlstm_kernel(q_ref, k_ref, vt_ref, o_ref, gr_ref, gout_ref, y_ref, ct_ref, m_ref):
    ln = ML_CHUNK
    nh = ML_HEADS
    nt = (((1,), (1,)), ((), ()))

    @pl.when(pl.program_id(1) == 0)
    def _():
        ct_ref[...] = jnp.zeros(ct_ref.shape, F32)
        m_ref[...] = jnp.zeros(m_ref.shape, F32)

    gr = gr_ref[0]
    c, b = gr[0:nh], gr[nh:2 * nh]
    m_prev = jnp.concatenate([m_ref[0:nh, :]] * (ln // LANES), axis=1)
    m_run = jnp.maximum(gr[2 * nh:3 * nh], m_prev)
    m_last = m_run[:, ln - 1:ln]
    e_inter = jnp.exp(m_prev - m_run)
    e_floor = jnp.exp(-(b + m_run))
    w_in = jnp.exp(c - m_last)
    decay = e_inter[:, ln - 1:ln]
    m_ref[0:nh, :] = jnp.broadcast_to(b[:, ln - 1:ln] + m_last, (nh, LANES))
    m_run_cols = jnp.concatenate([m_run, m_run], axis=0).T

    row = lax.broadcasted_iota(jnp.int32, (ln, ln), 0)
    col = lax.broadcasted_iota(jnp.int32, (ln, ln), 1)
    lower = col <= row
    ones_row = (lax.broadcasted_iota(jnp.int32, (ML_DIM, ln), 0) == 0).astype(BF16)

    for h in range(nh):
        lanes = slice(h * ML_DIM, (h + 1) * ML_DIM)
        q = q_ref[0, :, lanes]
        k = k_ref[0, :, lanes]
        vt_aug = jnp.concatenate([vt_ref[0, lanes, :], ones_row], axis=0)
        s = lax.dot_general(q, k, nt, preferred_element_type=F32)
        gate = jnp.exp(jnp.where(lower, c[h:h + 1, :] - m_run_cols[:, h:h + 1], -jnp.inf))
        w = (s * gate).astype(BF16)
        state_t = ct_ref[h]
        full_t = (e_inter[h:h + 1, :] * lax.dot_general(state_t.astype(BF16), q, nt, preferred_element_type=F32)
                  + lax.dot_general(vt_aug, w, nt, preferred_element_type=F32))
        den = full_t[ML_DIM:ML_DIM + 1, :]
        hh_t = full_t[0:ML_DIM, :] / jnp.maximum(jnp.abs(den), e_floor[h:h + 1, :])
        hn_t = hh_t * lax.rsqrt(jnp.mean(hh_t * hh_t, axis=0, keepdims=True) + EPS)
        y_ref[0, :, lanes] = (o_ref[0, :, lanes].astype(F32) * (hn_t.T * gout_ref[:, lanes])).astype(BF16)

        vw = (vt_aug.astype(F32) * w_in[h:h + 1, :]).astype(BF16)
        ct_ref[h] = decay[h:h + 1, :] * state_t + jnp.dot(vw, k, preferred_element_type=F32)


def _mlstm(q, k, vt, o, gates_row, g_out):
    bsz, s, _ = q.shape
    ln = ML_CHUNK
    tok = pl.BlockSpec((1, ln, ML_WIDTH), lambda b, c: (b, c, 0))
    return pl.pallas_call(
        _mlstm_kernel,
        grid=(bsz, s // ln),
        in_specs=[tok, tok, pl.BlockSpec((1, ML_WIDTH, ln), lambda b, c: (b, 0, c)), tok,
                  pl.BlockSpec((1, GATE_ROWS, ln), lambda b, c: (b, 0, c)),
                  _const_spec((1, ML_WIDTH))],
        out_specs=tok,
        out_shape=jax.ShapeDtypeStruct((bsz, s, ML_WIDTH), BF16),
        scratch_shapes=[pltpu.VMEM((ML_HEADS, 2 * ML_DIM, ML_DIM), F32),
                        pltpu.VMEM((8, LANES), F32)],
        compiler_params=pltpu.CompilerParams(
            dimension_semantics=("arbitrary", "arbitrary"), vmem_limit_bytes=VMEM_LIMIT),
        name="mlstm",
    )(q, k, vt, o, gates_row, g_out.reshape(1, ML_WIDTH))


def _rope_tables(s):
    half = ROPE_DIM // 2
    pos = jnp.arange(s, dtype=F32)
    inv_freq = ROPE_THETA ** (-jnp.arange(0, ROPE_DIM, 2, dtype=F32) / ROPE_DIM)
    ang = pos[:, None] * inv_freq[None, :]
    cos, sin = jnp.cos(ang), jnp.sin(ang)
    rest = DA_QK_DIM - ROPE_DIM
    cos_g = jnp.concatenate([cos, cos, jnp.ones((s, rest), F32)], axis=1)
    sina_g = jnp.concatenate([-sin, jnp.zeros((s, DA_QK_DIM - half), F32)], axis=1)
    sinb_g = jnp.concatenate([jnp.zeros((s, half), F32), sin, jnp.zeros((s, rest), F32)], axis=1)
    rep = LANES // DA_QK_DIM
    return jnp.tile(cos_g, (1, rep)), jnp.tile(sina_g, (1, rep)), jnp.tile(sinb_g, (1, rep))


def kernel(x, c, w_ada, b_ada, g_norm, ffn1_w12, ffn1_w3, w_in, conv_w, conv_b, b_igate, b_fgate,
           g_qnorm, g_knorm, lambda_qk, g_da_out, g_ml_out, w_out, ffn2_w12, ffn2_w3):
    bsz, s, d = x.shape
    l = 0
    mod = _ada(c, w_ada[l], b_ada[l]).reshape(bsz, 3, 3, d)

    def ffn(xin, sub, w12, w3, mix=None):
        w12b = w12.astype(BF16)
        return _ffn(xin, mod[:, sub], g_norm[l, sub], w12b[:, :D_FF], w12b[:, D_FF:], w3.astype(BF16), mix)

    x = ffn(x, 0, ffn1_w12[l], ffn1_w3[l])

    wb = w_in[l].astype(BF16)
    o0 = 0
    parts = []
    for width in (DA_WIDTH, DA_WIDTH, DA_WIDTH, 2 * ML_WIDTH, ML_WIDTH, ML_WIDTH, 2 * ML_HEADS):
        parts.append(wb[:, o0:o0 + width])
        o0 += width
    wq, wk, wv, wmqk, wmv, wmo, wif = parts
    wif = jnp.pad(wif.T, ((0, 2 * ML_HEADS), (0, 0)))
    wmv = wmv.T
    groups = DA_WIDTH // DA_QK_DIM
    q_gain = g_qnorm[l] * (DA_QK_DIM ** -0.5)
    gq = jnp.tile(q_gain * LOG2E, groups).reshape(1, DA_WIDTH)
    score_bound = 1.05 * DA_QK_DIM * jnp.max(jnp.abs(q_gain)) * jnp.max(jnp.abs(g_knorm[l]))
    bounded = (score_bound <= SAFE_SCORE_BOUND).astype(jnp.int32).reshape(1)
    gk = jnp.tile(g_knorm[l], groups).reshape(1, DA_WIDTH)
    cos, sina, sinb = _rope_tables(s)
    gid = jnp.arange(DA_WIDTH) // DA_QK_DIM
    bd = (gid[:, None] == gid[None, :]).astype(BF16)
    bif = jnp.concatenate([b_igate[l], b_fgate[l]]).reshape(2 * ML_HEADS, 1)
    da_q, da_k, da_v, ml_q, ml_k, ml_v, ml_o, gates = _inproj(
        x, mod[:, 1], g_norm[l, 1], wq, wk, wv, wmqk, wmv, wmo, wif, gq, gk, cos, sina, sinb, bd,
        conv_w[l], conv_b[l].reshape(1, 2 * ML_WIDTH), bif)

    y_da = _attention(bounded, da_q, da_k, da_v, lambda_qk[l], g_da_out[l])
    y_ml = _mlstm(ml_q, ml_k, ml_v, ml_o, gates, g_ml_out[l])

    wo = w_out[l].astype(BF16)
    return ffn(x, 2, ffn2_w12[l], ffn2_w3[l], mix=(mod[:, 1], y_da, y_ml, wo[:DA_WIDTH], wo[DA_WIDTH:]))
```

```python
import functools

import jax
import jax.numpy as jnp
from jax import lax
from jax.experimental import pallas as pl
from jax.experimental.pallas import tpu as pltpu

F32 = jnp.float32
BF16 = jnp.bfloat16

D_MODEL = 1024
DA_HEADS = 4
DA_QK_DIM = 64
DA_V_DIM = 2 * DA_QK_DIM
DA_WIDTH = DA_HEADS * DA_V_DIM
ML_HEADS = 4
ML_DIM = 128
ML_WIDTH = ML_HEADS * ML_DIM
ROPE_THETA = 500000.0
ROPE_DIM = DA_QK_DIM // 4
D_FF = 2816
CONV_K = 4
EPS = 1e-6
LAMBDA_INIT = 0.8 - 0.6 * 1.0
LOG2E = 1.4426950408889634

LANES = 128
VMEM_LIMIT = 56 * 1024 * 1024

FFN_TM = 512
FFN_SUB = 256
PROJ_TM = 512
ATT_T = 512
SAFE_SCORE_BOUND = 40.0
ML_CHUNK = 256
GATE_ROWS = 16
CONV_HALO = 8


def _const_spec(shape):
    nd = len(shape)
    return pl.BlockSpec(shape, lambda *_: (0,) * nd, pipeline_mode=pl.Buffered(1))


def _sigmoid(x):
    return 1.0 / (1.0 + jnp.exp(-x))


def _mod_norm(x, g, shift, scale):
    ms = jnp.mean(x * x, axis=-1, keepdims=True)
    return (x * lax.rsqrt(ms + EPS)) * (g * (1.0 + scale)) + shift


def _ada_kernel(c_ref, w_ref, b_ref, o_ref):
    c = c_ref[...]
    cs = (c * _sigmoid(c)).astype(BF16)
    o_ref[...] = jnp.dot(cs, w_ref[...].astype(BF16), preferred_element_type=F32) + b_ref[...]


def _ada(c, w_ada, b_ada):
    bsz, d = c.shape
    n = w_ada.shape[1]
    tn = 1024
    return pl.pallas_call(
        _ada_kernel,
        grid=(n // tn,),
        in_specs=[pl.BlockSpec((bsz, d), lambda j: (0, 0)),
                  pl.BlockSpec((d, tn), lambda j: (0, j)),
                  pl.BlockSpec((1, tn), lambda j: (0, j))],
        out_specs=pl.BlockSpec((bsz, tn), lambda j: (0, j)),
        out_shape=jax.ShapeDtypeStruct((bsz, n), F32),
        compiler_params=pltpu.CompilerParams(dimension_semantics=("arbitrary",)),
        name="adaln_mod",
    )(c, w_ada, b_ada.reshape(1, n))


def _ffn_body(x, mod_ref, g_ref, w1_ref, w2_ref, w3_ref, o_ref, act_ref):
    mod = mod_ref[0]
    hb = _mod_norm(x, g_ref[...], mod[0:1], mod[1:2]).astype(BF16)
    for c in range(D_FF // FFN_SUB):
        cols = slice(c * FFN_SUB, (c + 1) * FFN_SUB)
        a = jnp.dot(hb, w1_ref[:, cols], preferred_element_type=F32)
        b = jnp.dot(hb, w2_ref[:, cols], preferred_element_type=F32)
        act_ref[:, cols] = (a * _sigmoid(a) * b).astype(BF16)
    y = jnp.dot(act_ref[...], w3_ref[...], preferred_element_type=F32)
    o_ref[0] = x + (0.5 * (1.0 + mod[2:3])) * y


def _ffn_kernel(x_ref, mod_ref, g_ref, w1_ref, w2_ref, w3_ref, o_ref, act_ref):
    _ffn_body(x_ref[0], mod_ref, g_ref, w1_ref, w2_ref, w3_ref, o_ref, act_ref)


def _mix_ffn_kernel(x_ref, mmod_ref, ya_ref, ym_ref, wa_ref, wm_ref,
                    mod_ref, g_ref, w1_ref, w2_ref, w3_ref, o_ref, act_ref):
    y = (jnp.dot(ya_ref[0], wa_ref[...], preferred_element_type=F32)
         + jnp.dot(ym_ref[0], wm_ref[...], preferred_element_type=F32))
    x = x_ref[0] + (1.0 + mmod_ref[0][2:3]) * y
    _ffn_body(x, mod_ref, g_ref, w1_ref, w2_ref, w3_ref, o_ref, act_ref)


def _ffn(x, mod3, g, w1, w2, w3, mix=None):
    bsz, s, d = x.shape
    tm = FFN_TM
    tok_spec = lambda width: pl.BlockSpec((1, tm, width), lambda b, i: (b, i, 0))
    mod_spec = pl.BlockSpec((1, 3, d), lambda b, i: (b, 0, 0))
    ffn_specs = [mod_spec, _const_spec((1, d)),
                 _const_spec(w1.shape), _const_spec(w2.shape), _const_spec(w3.shape)]
    ffn_args = (mod3, g.reshape(1, d), w1, w2, w3)
    if mix is None:
        body, in_specs, args = _ffn_kernel, [tok_spec(d)] + ffn_specs, (x,) + ffn_args
    else:
        mmod3, y_da, y_ml, w_a, w_m = mix
        body = _mix_ffn_kernel
        in_specs = [tok_spec(d), mod_spec, tok_spec(DA_WIDTH), tok_spec(ML_WIDTH),
                    _const_spec(w_a.shape), _const_spec(w_m.shape)] + ffn_specs
        args = (x, mmod3, y_da, y_ml, w_a, w_m) + ffn_args
    return pl.pallas_call(
        body,
        grid=(bsz, s // tm),
        in_specs=in_specs,
        out_specs=tok_spec(d),
        out_shape=jax.ShapeDtypeStruct(x.shape, F32),
        scratch_shapes=[pltpu.VMEM((tm, D_FF), BF16)],
        compiler_params=pltpu.CompilerParams(
            dimension_semantics=("arbitrary", "arbitrary"), vmem_limit_bytes=VMEM_LIMIT),
        name="ffn" if mix is None else "mix_ffn",
    )(*args)


def _group_norm_rope(u, gvec, bd, cos, sina, sinb):
    ssq = jnp.dot((u * u).astype(BF16), bd, preferred_element_type=F32)
    xn = (u * lax.rsqrt(ssq * (1.0 / DA_QK_DIM) + EPS)) * gvec
    outs = []
    for h in range(DA_HEADS):
        xh = xn[:, h * LANES:(h + 1) * LANES]
        up = pltpu.roll(xh, LANES - ROPE_DIM // 2, 1)
        dn = pltpu.roll(xh, ROPE_DIM // 2, 1)
        outs.append(xh * cos + up * sina + dn * sinb)
    return jnp.concatenate(outs, axis=1)


def _chunk_scan(x, op, identity):
    pos = lax.broadcasted_iota(jnp.int32, x.shape, 1) & (ML_CHUNK - 1)
    d = 1
    while d < ML_CHUNK:
        x = op(x, jnp.where(pos >= d, pltpu.roll(x, d, 1), identity))
        d *= 2
    return x


def _inproj_kernel(x_ref, mod_ref, g_ref, wq_ref, wk_ref, wv_ref, wmqk_ref, wmv_ref, wmo_ref, wif_ref,
                   gq_ref, gk_ref, cos_ref, sina_ref, sinb_ref, bd_ref, cw_ref, cb_ref, bif_ref,
                   q_out, k_out, v_out, mq_out, mk_out, mv_out, mo_out, gate_out, ext_ref):
    tm = x_ref.shape[1]
    si = pl.program_id(1)
    mod = mod_ref[0]
    hb = _mod_norm(x_ref[0], g_ref[...], mod[0:1], mod[1:2]).astype(BF16)

    nh = ML_HEADS
    pre = lax.dot_general(wif_ref[...], hb, (((1,), (1,)), ((), ())),
                          preferred_element_type=F32)[0:2 * nh] + bif_ref[...]
    zf = pre[nh:2 * nh]
    log_f = jnp.minimum(zf, 0.0) - jnp.log1p(jnp.exp(-jnp.abs(zf)))
    b = _chunk_scan(log_f, jnp.add, 0.0)
    c = pre[0:nh] - b
    gate_out[0] = jnp.concatenate([c, b, _chunk_scan(c, jnp.maximum, -jnp.inf), jnp.zeros_like(c)], axis=0)

    cos, sina, sinb = cos_ref[...], sina_ref[...], sinb_ref[...]
    bd = bd_ref[...]
    uq = jnp.dot(hb, wq_ref[...], preferred_element_type=F32)
    q_out[0] = _group_norm_rope(uq, gq_ref[...], bd, cos, sina, sinb).astype(BF16)
    uk = jnp.dot(hb, wk_ref[...], preferred_element_type=F32)
    k_out[0] = _group_norm_rope(uk, gk_ref[...], bd, cos, sina, sinb).astype(BF16)
    v_out[0] = jnp.dot(hb, wv_ref[...], preferred_element_type=F32).astype(BF16)

    ext_ref[0:CONV_HALO, :] = jnp.where(si == 0, 0.0, ext_ref[0:CONV_HALO, :])
    uqk = jnp.dot(hb, wmqk_ref[...], preferred_element_type=F32)
    ext_ref[CONV_HALO:CONV_HALO + tm, :] = uqk
    cw = cw_ref[...]
    acc = cb_ref[...]
    for j in range(CONV_K - 1):
        off = CONV_HALO - (CONV_K - 1) + j
        acc = acc + ext_ref[off:off + tm, :] * cw[j:j + 1]
    acc = acc + uqk * cw[CONV_K - 1:CONV_K]
    ext_ref[0:CONV_HALO, :] = uqk[tm - CONV_HALO:tm, :]
    qk = acc * _sigmoid(acc)
    mq_out[0] = (qk[:, :ML_WIDTH] * (ML_DIM ** -0.5)).astype(BF16)
    mk_out[0] = qk[:, ML_WIDTH:].astype(BF16)

    mv_out[0] = lax.dot_general(wmv_ref[...], hb, (((1,), (1,)), ((), ())),
                                preferred_element_type=F32).astype(BF16)
    mo_out[0] = _sigmoid(jnp.dot(hb, wmo_ref[...], preferred_element_type=F32)).astype(BF16)


def _inproj(x, mod3, g, wq, wk, wv, wmqk, wmv, wmo, wif, gq, gk, cos, sina, sinb, bd, cw, cb, bif):
    bsz, s, d = x.shape
    tm = PROJ_TM
    tok = lambda width, dt: jax.ShapeDtypeStruct((bsz, s, width), dt)
    tok_spec = lambda width: pl.BlockSpec((1, tm, width), lambda b, i: (b, i, 0))
    tab_spec = pl.BlockSpec((tm, LANES), lambda b, i: (i, 0))
    return pl.pallas_call(
        _inproj_kernel,
        grid=(bsz, s // tm),
        in_specs=[tok_spec(d),
                  pl.BlockSpec((1, 3, d), lambda b, i: (b, 0, 0)),
                  _const_spec((1, d)),
                  _const_spec(wq.shape), _const_spec(wk.shape), _const_spec(wv.shape),
                  _const_spec(wmqk.shape), _const_spec(wmv.shape), _const_spec(wmo.shape),
                  _const_spec(wif.shape),
                  _const_spec(gq.shape), _const_spec(gk.shape),
                  tab_spec, tab_spec, tab_spec,
                  _const_spec(bd.shape), _const_spec(cw.shape), _const_spec(cb.shape),
                  _const_spec(bif.shape)],
        out_specs=[tok_spec(DA_WIDTH), tok_spec(DA_WIDTH), tok_spec(DA_WIDTH),
                   tok_spec(ML_WIDTH), tok_spec(ML_WIDTH),
                   pl.BlockSpec((1, ML_WIDTH, tm), lambda b, i: (b, 0, i)),
                   tok_spec(ML_WIDTH),
                   pl.BlockSpec((1, GATE_ROWS, tm), lambda b, i: (b, 0, i))],
        out_shape=[tok(DA_WIDTH, BF16), tok(DA_WIDTH, BF16), tok(DA_WIDTH, BF16),
                   tok(ML_WIDTH, BF16), tok(ML_WIDTH, BF16),
                   jax.ShapeDtypeStruct((bsz, ML_WIDTH, s), BF16),
                   tok(ML_WIDTH, BF16),
                   jax.ShapeDtypeStruct((bsz, GATE_ROWS, s), F32)],
        scratch_shapes=[pltpu.VMEM((tm + CONV_HALO, 2 * ML_WIDTH), F32)],
        compiler_params=pltpu.CompilerParams(
            dimension_semantics=("arbitrary", "arbitrary"), vmem_limit_bytes=VMEM_LIMIT),
        name="in_proj",
    )(x, mod3, g.reshape(1, d), wq, wk, wv, wmqk, wmv, wmo, wif, gq, gk, cos, sina, sinb, bd, cw, cb, bif)


def _attn_kernel(bounded_ref, q_ref, k_ref, v_ref, lam_ref, gout_ref, o_ref, qs_ref, m_ref, l_ref, acc_ref):
    t = ATT_T
    qi = pl.program_id(2)

    lane = lax.broadcasted_iota(jnp.int32, (t, DA_V_DIM), 1)
    for half in range(2):
        q = q_ref[0, half * t:(half + 1) * t, :]
        zero = jnp.zeros_like(q)
        qs_ref[half, 0:t, :] = jnp.where(lane < DA_QK_DIM, q, zero)
        qs_ref[half, t:2 * t, :] = jnp.where(lane >= DA_QK_DIM, q, zero)

    def keys(kb):
        start = pl.multiple_of(kb * t, t)
        return k_ref[0, pl.ds(start, t), :], v_ref[0, pl.ds(start, t), :]

    def block(half, kv, diagonal, first, stabilised):
        ks, vs = kv
        s = lax.dot_general(qs_ref[half], ks, (((1,), (1,)), ((), ())), preferred_element_type=F32)
        if diagonal:
            row = lax.broadcasted_iota(jnp.int32, s.shape, 0) & (t - 1)
            col = lax.broadcasted_iota(jnp.int32, s.shape, 1)
            s = jnp.where(col <= row, s, -jnp.inf)
        if stabilised:
            m_new = jnp.max(s, axis=1, keepdims=True)
            if not first:
                m_prev = m_ref[half]
                m_new = jnp.maximum(m_prev, m_new)
                alpha = jnp.exp2(m_prev - m_new)
            m_ref[half] = m_new
            s = s - m_new
        p = jnp.exp2(s)
        lp = p[:, 0:LANES]
        for c in range(1, t // LANES):
            lp = lp + p[:, c * LANES:(c + 1) * LANES]
        pv = jnp.dot(p.astype(BF16), vs, preferred_element_type=F32)
        if first:
            l_ref[half] = lp
            acc_ref[half] = pv
        elif stabilised:
            l_ref[half] = alpha * l_ref[half] + lp
            acc_ref[half] = alpha * acc_ref[half] + pv
        else:
            l_ref[half] += lp
            acc_ref[half] += pv

    def run(stabilised):
        kv = keys(2 * qi)
        block(0, kv, True, True, stabilised)
        block(1, kv, False, True, stabilised)
        block(1, keys(2 * qi + 1), True, False, stabilised)

        def body(kb, carry):
            kv = keys(kb)
            block(0, kv, False, False, stabilised)
            block(1, kv, False, False, stabilised)
            return carry

        lax.fori_loop(0, 2 * qi, body, 0)

    @pl.when(bounded_ref[0] != 0)
    def _():
        run(False)

    @pl.when(bounded_ref[0] == 0)
    def _():
        run(True)

    lv = lam_ref[...]
    lam = (jnp.exp(jnp.sum(lv[0:1] * lv[1:2], axis=1, keepdims=True))
           - jnp.exp(jnp.sum(lv[2:3] * lv[3:4], axis=1, keepdims=True)) + LAMBDA_INIT)
    for half in range(2):
        o = acc_ref[half] / jnp.sum(l_ref[half], axis=1, keepdims=True)
        od = o[0:t] - lam * o[t:2 * t]
        ms = jnp.mean(od * od, axis=-1, keepdims=True)
        o_ref[0, half * t:(half + 1) * t, :] = (
            (od * lax.rsqrt(ms + EPS)) * gout_ref[...] * (1.0 - LAMBDA_INIT)).astype(BF16)


def _attention(bounded, q, k, v, lam_vecs, g_out):
    bsz, s, _ = q.shape
    t = ATT_T
    blk = pl.BlockSpec((1, 2 * t, DA_V_DIM), lambda b, h, i: (b, i, h))
    full = pl.BlockSpec((1, s, DA_V_DIM), lambda b, h, i: (b, 0, h))
    return pl.pallas_call(
        _attn_kernel,
        grid=(bsz, DA_HEADS, s // (2 * t)),
        in_specs=[pl.BlockSpec(memory_space=pltpu.SMEM),
                  blk, full, full, _const_spec(lam_vecs.shape), _const_spec((1, DA_V_DIM))],
        out_specs=blk,
        out_shape=jax.ShapeDtypeStruct((bsz, s, DA_WIDTH), BF16),
        scratch_shapes=[pltpu.VMEM((2, 2 * t, DA_V_DIM), BF16),
                        pltpu.VMEM((2, 2 * t, 1), F32),
                        pltpu.VMEM((2, 2 * t, LANES), F32),
                        pltpu.VMEM((2, 2 * t, DA_V_DIM), F32)],
        compiler_params=pltpu.CompilerParams(
            dimension_semantics=("arbitrary", "arbitrary", "arbitrary"), vmem_limit_bytes=VMEM_LIMIT),
        name="diff_attention",
    )(bounded, q, k, v, lam_vecs, g_out.reshape(1, DA_V_DIM))


def _mlstm_kernel(q_ref, k_ref, vt_ref, o_ref, gr_ref, gout_ref, y_ref, ct_ref, m_ref):
    ln = ML_CHUNK
    nh = ML_HEADS
    nt = (((1,), (1,)), ((), ()))

    @pl.when(pl.program_id(1) == 0)
    def _():
        ct_ref[...] = jnp.zeros(ct_ref.shape, F32)
        m_ref[...] = jnp.zeros(m_ref.shape, F32)

    gr = gr_ref[0]
    c, b = gr[0:nh], gr[nh:2 * nh]
    m_prev = jnp.concatenate([m_ref[0:nh, :]] * (ln // LANES), axis=1)
    m_run = jnp.maximum(gr[2 * nh:3 * nh], m_prev)
    m_last = m_run[:, ln - 1:ln]
    e_inter = jnp.exp(m_prev - m_run)
    e_floor = jnp.exp(-(b + m_run))
    w_in = jnp.exp(c - m_last)
    decay = e_inter[:, ln - 1:ln]
    m_ref[0:nh, :] = jnp.broadcast_to(b[:, ln - 1:ln] + m_last, (nh, LANES))
    m_run_cols = jnp.concatenate([m_run, m_run], axis=0).T

    row = lax.broadcasted_iota(jnp.int32, (ln, ln), 0)
    col = lax.broadcasted_iota(jnp.int32, (ln, ln), 1)
    lower = col <= row
    ones_row = (lax.broadcasted_iota(jnp.int32, (ML_DIM, ln), 0) == 0).astype(BF16)

    for h in range(nh):
        lanes = slice(h * ML_DIM, (h + 1) * ML_DIM)
        q = q_ref[0, :, lanes]
        k = k_ref[0, :, lanes]
        vt_aug = jnp.concatenate([vt_ref[0, lanes, :], ones_row], axis=0)
        s = lax.dot_general(q, k, nt, preferred_element_type=F32)
        gate = jnp.exp(jnp.where(lower, c[h:h + 1, :] - m_run_cols[:, h:h + 1], -jnp.inf))
        w = (s * gate).astype(BF16)
        state_t = ct_ref[h]
        full_t = (e_inter[h:h + 1, :] * lax.dot_general(state_t.astype(BF16), q, nt, preferred_element_type=F32)
                  + lax.dot_general(vt_aug, w, nt, preferred_element_type=F32))
        den = full_t[ML_DIM:ML_DIM + 1, :]
        hh_t = full_t[0:ML_DIM, :] / jnp.maximum(jnp.abs(den), e_floor[h:h + 1, :])
        hn_t = hh_t * lax.rsqrt(jnp.mean(hh_t * hh_t, axis=0, keepdims=True) + EPS)
        y_ref[0, :, lanes] = (o_ref[0, :, lanes].astype(F32) * (hn_t.T * gout_ref[:, lanes])).astype(BF16)

        vw = (vt_aug.astype(F32) * w_in[h:h + 1, :]).astype(BF16)
        ct_ref[h] = decay[h:h + 1, :] * state_t + jnp.dot(vw, k, preferred_element_type=F32)


def _mlstm(q, k, vt, o, gates_row, g_out):
    bsz, s, _ = q.shape
    ln = ML_CHUNK
    tok = pl.BlockSpec((1, ln, ML_WIDTH), lambda b, c: (b, c, 0))
    return pl.pallas_call(
        _mlstm_kernel,
        grid=(bsz, s // ln),
        in_specs=[tok, tok, pl.BlockSpec((1, ML_WIDTH, ln), lambda b, c: (b, 0, c)), tok,
                  pl.BlockSpec((1, GATE_ROWS, ln), lambda b, c: (b, 0, c)),
                  _const_spec((1, ML_WIDTH))],
        out_specs=tok,
        out_shape=jax.ShapeDtypeStruct((bsz, s, ML_WIDTH), BF16),
        scratch_shapes=[pltpu.VMEM((ML_HEADS, 2 * ML_DIM, ML_DIM), F32),
                        pltpu.VMEM((8, LANES), F32)],
        compiler_params=pltpu.CompilerParams(
            dimension_semantics=("arbitrary", "arbitrary"), vmem_limit_bytes=VMEM_LIMIT),
        name="mlstm",
    )(q, k, vt, o, gates_row, g_out.reshape(1, ML_WIDTH))


def _rope_tables(s):
    half = ROPE_DIM // 2
    pos = jnp.arange(s, dtype=F32)
    inv_freq = ROPE_THETA ** (-jnp.arange(0, ROPE_DIM, 2, dtype=F32) / ROPE_DIM)
    ang = pos[:, None] * inv_freq[None, :]
    cos, sin = jnp.cos(ang), jnp.sin(ang)
    rest = DA_QK_DIM - ROPE_DIM
    cos_g = jnp.concatenate([cos, cos, jnp.ones((s, rest), F32)], axis=1)
    sina_g = jnp.concatenate([-sin, jnp.zeros((s, DA_QK_DIM - half), F32)], axis=1)
    sinb_g = jnp.concatenate([jnp.zeros((s, half), F32), sin, jnp.zeros((s, rest), F32)], axis=1)
    rep = LANES // DA_QK_DIM
    return jnp.tile(cos_g, (1, rep)), jnp.tile(sina_g, (1, rep)), jnp.tile(sinb_g, (1, rep))


def kernel(x, c, w_ada, b_ada, g_norm, ffn1_w12, ffn1_w3, w_in, conv_w, conv_b, b_igate, b_fgate,
           g_qnorm, g_knorm, lambda_qk, g_da_out, g_ml_out, w_out, ffn2_w12, ffn2_w3):
    bsz, s, d = x.shape
    l = 0
    mod = _ada(c, w_ada[l], b_ada[l]).reshape(bsz, 3, 3, d)

    def ffn(xin, sub, w12, w3, mix=None):
        w12b = w12.astype(BF16)
        return _ffn(xin, mod[:, sub], g_norm[l, sub], w12b[:, :D_FF], w12b[:, D_FF:], w3.astype(BF16), mix)

    x = ffn(x, 0, ffn1_w12[l], ffn1_w3[l])

    wb = w_in[l].astype(BF16)
    o0 = 0
    parts = []
    for width in (DA_WIDTH, DA_WIDTH, DA_WIDTH, 2 * ML_WIDTH, ML_WIDTH, ML_WIDTH, 2 * ML_HEADS):
        parts.append(wb[:, o0:o0 + width])
        o0 += width
    wq, wk, wv, wmqk, wmv, wmo, wif = parts
    wif = jnp.pad(wif.T, ((0, 2 * ML_HEADS), (0, 0)))
    wmv = wmv.T
    groups = DA_WIDTH // DA_QK_DIM
    q_gain = g_qnorm[l] * (DA_QK_DIM ** -0.5)
    gq = jnp.tile(q_gain * LOG2E, groups).reshape(1, DA_WIDTH)
    score_bound = 1.05 * DA_QK_DIM * jnp.max(jnp.abs(q_gain)) * jnp.max(jnp.abs(g_knorm[l]))
    bounded = (score_bound <= SAFE_SCORE_BOUND).astype(jnp.int32).reshape(1)
    gk = jnp.tile(g_knorm[l], groups).reshape(1, DA_WIDTH)
    cos, sina, sinb = _rope_tables(s)
    gid = jnp.arange(DA_WIDTH) // DA_QK_DIM
    bd = (gid[:, None] == gid[None, :]).astype(BF16)
    bif = jnp.concatenate([b_igate[l], b_fgate[l]]).reshape(2 * ML_HEADS, 1)
    da_q, da_k, da_v, ml_q, ml_k, ml_v, ml_o, gates = _inproj(
        x, mod[:, 1], g_norm[l, 1], wq, wk, wv, wmqk, wmv, wmo, wif, gq, gk, cos, sina, sinb, bd,
        conv_w[l], conv_b[l].reshape(1, 2 * ML_WIDTH), bif)

    y_da = _attention(bounded, da_q, da_k, da_v, lambda_qk[l], g_da_out[l])
    y_ml = _mlstm(ml_q, ml_k, ml_v, ml_o, gates, g_ml_out[l])

    wo = w_out[l].astype(BF16)
    return ffn(x, 2, ffn2_w12[l], ffn2_w3[l], mix=(mod[:, 1], y_da, y_ml, wo[:DA_WIDTH], wo[DA_WIDTH:]))
```

```python
import functools

import jax
import jax.numpy as jnp
from jax import lax
from jax.experimental import pallas as pl
from jax.experimental.pallas import tpu as pltpu

F32 = jnp.float32
BF16 = jnp.bfloat16

D_MODEL = 1024
DA_HEADS = 4
DA_QK_DIM = 64
DA_V_DIM = 2 * DA_QK_DIM
DA_WIDTH = DA_HEADS * DA_V_DIM
ML_HEADS = 4
ML_DIM = 128
ML_WIDTH = ML_HEADS * ML_DIM
ROPE_THETA = 500000.0
ROPE_DIM = DA_QK_DIM // 4
D_FF = 2816
CONV_K = 4
EPS = 1e-6
LAMBDA_INIT = 0.8 - 0.6 * 1.0
LOG2E = 1.4426950408889634

LANES = 128
VMEM_LIMIT = 56 * 1024 * 1024

FFN_TM = 1024
FFN_SUB = 256
PROJ_TM = 1024
ATT_T = 512
SAFE_SCORE_BOUND = 40.0
ML_CHUNK = 256
ML_ROWS = 4
GATE_ROWS = 16
CONV_HALO = 8


def _const_spec(shape):
    nd = len(shape)
    return pl.BlockSpec(shape, lambda *_: (0,) * nd, pipeline_mode=pl.Buffered(1))


def _sigmoid(x):
    return 1.0 / (1.0 + jnp.exp(-x))


def _mod_norm(x, g, shift, scale):
    ms = jnp.mean(x * x, axis=-1, keepdims=True)
    return (x * lax.rsqrt(ms + EPS)) * (g * (1.0 + scale)) + shift


def _ada_kernel(c_ref, w_ref, b_ref, o_ref):
    c = c_ref[...]
    cs = (c * _sigmoid(c)).astype(BF16)
    o_ref[...] = jnp.dot(cs, w_ref[...].astype(BF16), preferred_element_type=F32) + b_ref[...]


def _ada(c, w_ada, b_ada):
    bsz, d = c.shape
    n = w_ada.shape[1]
    tn = 1024
    return pl.pallas_call(
        _ada_kernel,
        grid=(n // tn,),
        in_specs=[pl.BlockSpec((bsz, d), lambda j: (0, 0)),
                  pl.BlockSpec((d, tn), lambda j: (0, j)),
                  pl.BlockSpec((1, tn), lambda j: (0, j))],
        out_specs=pl.BlockSpec((bsz, tn), lambda j: (0, j)),
        out_shape=jax.ShapeDtypeStruct((bsz, n), F32),
        compiler_params=pltpu.CompilerParams(dimension_semantics=("arbitrary",)),
        name="adaln_mod",
    )(c, w_ada, b_ada.reshape(1, n))


def _ffn_body(x, mod_ref, g_ref, w1_ref, w2_ref, w3_ref, o_ref, act_ref):
    mod = mod_ref[0]
    hb = _mod_norm(x, g_ref[...], mod[0:1], mod[1:2]).astype(BF16)
    for c in range(D_FF // FFN_SUB):
        cols = slice(c * FFN_SUB, (c + 1) * FFN_SUB)
        a = jnp.dot(hb, w1_ref[:, cols], preferred_element_type=F32)
        b = jnp.dot(hb, w2_ref[:, cols], preferred_element_type=F32)
        act_ref[:, cols] = (a * _sigmoid(a) * b).astype(BF16)
    y = jnp.dot(act_ref[...], w3_ref[...], preferred_element_type=F32)
    o_ref[0] = x + (0.5 * (1.0 + mod[2:3])) * y


def _ffn_kernel(x_ref, mod_ref, g_ref, w1_ref, w2_ref, w3_ref, o_ref, act_ref):
    _ffn_body(x_ref[0], mod_ref, g_ref, w1_ref, w2_ref, w3_ref, o_ref, act_ref)


def _mix_ffn_kernel(x_ref, mmod_ref, ya_ref, ym_ref, wa_ref, wm_ref,
                    mod_ref, g_ref, w1_ref, w2_ref, w3_ref, o_ref, act_ref):
    y = (jnp.dot(ya_ref[0], wa_ref[...], preferred_element_type=F32)
         + jnp.dot(ym_ref[0], wm_ref[...], preferred_element_type=F32))
    x = x_ref[0] + (1.0 + mmod_ref[0][2:3]) * y
    _ffn_body(x, mod_ref, g_ref, w1_ref, w2_ref, w3_ref, o_ref, act_ref)


def _ffn(x, mod3, g, w1, w2, w3, mix=None):
    bsz, s, d = x.shape
    tm = FFN_TM
    tok_spec = lambda width: pl.BlockSpec((1, tm, width), lambda b, i: (b, i, 0))
    mod_spec = pl.BlockSpec((1, 3, d), lambda b, i: (b, 0, 0))
    ffn_specs = [mod_spec, _const_spec((1, d)),
                 _const_spec(w1.shape), _const_spec(w2.shape), _const_spec(w3.shape)]
    ffn_args = (mod3, g.reshape(1, d), w1, w2, w3)
    if mix is None:
        body, in_specs, args = _ffn_kernel, [tok_spec(d)] + ffn_specs, (x,) + ffn_args
    else:
        mmod3, y_da, y_ml, w_a, w_m = mix
        body = _mix_ffn_kernel
        in_specs = [tok_spec(d), mod_spec, tok_spec(DA_WIDTH), tok_spec(ML_WIDTH),
                    _const_spec(w_a.shape), _const_spec(w_m.shape)] + ffn_specs
        args = (x, mmod3, y_da, y_ml, w_a, w_m) + ffn_args
    return pl.pallas_call(
        body,
        grid=(bsz, s // tm),
        in_specs=in_specs,
        out_specs=tok_spec(d),
        out_shape=jax.ShapeDtypeStruct(x.shape, F32),
        scratch_shapes=[pltpu.VMEM((tm, D_FF), BF16)],
        compiler_params=pltpu.CompilerParams(
            dimension_semantics=("arbitrary", "arbitrary"), vmem_limit_bytes=VMEM_LIMIT),
        name="ffn" if mix is None else "mix_ffn",
    )(*args)


def _group_norm_rope(u, gvec, bd, cos, sina, sinb):
    ssq = jnp.dot((u * u).astype(BF16), bd, preferred_element_type=F32)
    xn = (u * lax.rsqrt(ssq * (1.0 / DA_QK_DIM) + EPS)) * gvec
    outs = []
    for h in range(DA_HEADS):
        xh = xn[:, h * LANES:(h + 1) * LANES]
        up = pltpu.roll(xh, LANES - ROPE_DIM // 2, 1)
        dn = pltpu.roll(xh, ROPE_DIM // 2, 1)
        outs.append(xh * cos + up * sina + dn * sinb)
    return jnp.concatenate(outs, axis=1)


def _chunk_scan(x, op, identity):
    pos = lax.broadcasted_iota(jnp.int32, x.shape, 1) & (ML_CHUNK - 1)
    d = 1
    while d < ML_CHUNK:
        x = op(x, jnp.where(pos >= d, pltpu.roll(x, d, 1), identity))
        d *= 2
    return x


def _inproj_kernel(x_ref, mod_ref, g_ref, wq_ref, wk_ref, wv_ref, wmqk_ref, wmv_ref, wmo_ref, wif_ref,
                   gq_ref, gk_ref, cos_ref, sina_ref, sinb_ref, bd_ref, cw_ref, cb_ref, bif_ref,
                   q_out, k_out, v_out, mq_out, mk_out, mv_out, mo_out, gate_out, ext_ref):
    tm = x_ref.shape[1]
    si = pl.program_id(1)
    mod = mod_ref[0]
    hb = _mod_norm(x_ref[0], g_ref[...], mod[0:1], mod[1:2]).astype(BF16)

    nh = ML_HEADS
    pre = lax.dot_general(wif_ref[...], hb, (((1,), (1,)), ((), ())),
                          preferred_element_type=F32)[0:2 * nh] + bif_ref[...]
    zf = pre[nh:2 * nh]
    log_f = jnp.minimum(zf, 0.0) - jnp.log1p(jnp.exp(-jnp.abs(zf)))
    b = _chunk_scan(log_f, jnp.add, 0.0)
    c = pre[0:nh] - b
    gate_out[0] = jnp.concatenate([c, b, _chunk_scan(c, jnp.maximum, -jnp.inf), jnp.zeros_like(c)], axis=0)

    cos, sina, sinb = cos_ref[...], sina_ref[...], sinb_ref[...]
    bd = bd_ref[...]
    uq = jnp.dot(hb, wq_ref[...], preferred_element_type=F32)
    q_out[0] = _group_norm_rope(uq, gq_ref[...], bd, cos, sina, sinb).astype(BF16)
    uk = jnp.dot(hb, wk_ref[...], preferred_element_type=F32)
    k_out[0] = _group_norm_rope(uk, gk_ref[...], bd, cos, sina, sinb).astype(BF16)
    v_out[0] = jnp.dot(hb, wv_ref[...], preferred_element_type=F32).astype(BF16)

    ext_ref[0:CONV_HALO, :] = jnp.where(si == 0, 0.0, ext_ref[0:CONV_HALO, :])
    uqk = jnp.dot(hb, wmqk_ref[...], preferred_element_type=F32)
    ext_ref[CONV_HALO:CONV_HALO + tm, :] = uqk
    cw = cw_ref[...]
    acc = cb_ref[...]
    for j in range(CONV_K - 1):
        off = CONV_HALO - (CONV_K - 1) + j
        acc = acc + ext_ref[off:off + tm, :] * cw[j:j + 1]
    acc = acc + uqk * cw[CONV_K - 1:CONV_K]
    ext_ref[0:CONV_HALO, :] = uqk[tm - CONV_HALO:tm, :]
    qk = acc * _sigmoid(acc)
    mq_out[0] = (qk[:, :ML_WIDTH] * (ML_DIM ** -0.5)).astype(BF16)
    mk_out[0] = qk[:, ML_WIDTH:].astype(BF16)

    mv_out[0] = lax.dot_general(wmv_ref[...], hb, (((1,), (1,)), ((), ())),
                                preferred_element_type=F32).astype(BF16)
    mo_out[0] = _sigmoid(jnp.dot(hb, wmo_ref[...], preferred_element_type=F32)).astype(BF16)


def _inproj(x, mod3, g, wq, wk, wv, wmqk, wmv, wmo, wif, gq, gk, cos, sina, sinb, bd, cw, cb, bif):
    bsz, s, d = x.shape
    tm = PROJ_TM
    tok = lambda width, dt: jax.ShapeDtypeStruct((bsz, s, width), dt)
    tok_spec = lambda width: pl.BlockSpec((1, tm, width), lambda b, i: (b, i, 0))
    tab_spec = pl.BlockSpec((tm, LANES), lambda b, i: (i, 0))
    return pl.pallas_call(
        _inproj_kernel,
        grid=(bsz, s // tm),
        in_specs=[tok_spec(d),
                  pl.BlockSpec((1, 3, d), lambda b, i: (b, 0, 0)),
                  _const_spec((1, d)),
                  _const_spec(wq.shape), _const_spec(wk.shape), _const_spec(wv.shape),
                  _const_spec(wmqk.shape), _const_spec(wmv.shape), _const_spec(wmo.shape),
                  _const_spec(wif.shape),
                  _const_spec(gq.shape), _const_spec(gk.shape),
                  tab_spec, tab_spec, tab_spec,
                  _const_spec(bd.shape), _const_spec(cw.shape), _const_spec(cb.shape),
                  _const_spec(bif.shape)],
        out_specs=[tok_spec(DA_WIDTH), tok_spec(DA_WIDTH), tok_spec(DA_WIDTH),
                   tok_spec(ML_WIDTH), tok_spec(ML_WIDTH),
                   pl.BlockSpec((1, ML_WIDTH, tm), lambda b, i: (b, 0, i)),
                   tok_spec(ML_WIDTH),
                   pl.BlockSpec((1, GATE_ROWS, tm), lambda b, i: (b, 0, i))],
        out_shape=[tok(DA_WIDTH, BF16), tok(DA_WIDTH, BF16), tok(DA_WIDTH, BF16),
                   tok(ML_WIDTH, BF16), tok(ML_WIDTH, BF16),
                   jax.ShapeDtypeStruct((bsz, ML_WIDTH, s), BF16),
                   tok(ML_WIDTH, BF16),
                   jax.ShapeDtypeStruct((bsz, GATE_ROWS, s), F32)],
        scratch_shapes=[pltpu.VMEM((tm + CONV_HALO, 2 * ML_WIDTH), F32)],
        compiler_params=pltpu.CompilerParams(
            dimension_semantics=("arbitrary", "arbitrary"), vmem_limit_bytes=VMEM_LIMIT),
        name="in_proj",
    )(x, mod3, g.reshape(1, d), wq, wk, wv, wmqk, wmv, wmo, wif, gq, gk, cos, sina, sinb, bd, cw, cb, bif)


def _attn_kernel(bounded_ref, q_ref, k_ref, v_ref, lam_ref, gout_ref, o_ref, qs_ref, m_ref, l_ref, acc_ref):
    t = ATT_T
    qi = pl.program_id(2)

    lane = lax.broadcasted_iota(jnp.int32, (t, DA_V_DIM), 1)
    for half in range(2):
        q = q_ref[0, half * t:(half + 1) * t, :]
        zero = jnp.zeros_like(q)
        qs_ref[half, 0:t, :] = jnp.where(lane < DA_QK_DIM, q, zero)
        qs_ref[half, t:2 * t, :] = jnp.where(lane >= DA_QK_DIM, q, zero)

    def keys(kb):
        start = pl.multiple_of(kb * t, t)
        return k_ref[0, pl.ds(start, t), :], v_ref[0, pl.ds(start, t), :]

    def block(half, kv, diagonal, first, stabilised):
        ks, vs = kv
        s = lax.dot_general(qs_ref[half], ks, (((1,), (1,)), ((), ())), preferred_element_type=F32)
        if diagonal:
            row = lax.broadcasted_iota(jnp.int32, s.shape, 0) & (t - 1)
            col = lax.broadcasted_iota(jnp.int32, s.shape, 1)
            s = jnp.where(col <= row, s, -jnp.inf)
        if stabilised:
            m_new = jnp.max(s, axis=1, keepdims=True)
            if not first:
                m_prev = m_ref[half]
                m_new = jnp.maximum(m_prev, m_new)
                alpha = jnp.exp2(m_prev - m_new)
            m_ref[half] = m_new
            s = s - m_new
        p = jnp.exp2(s)
        lp = p[:, 0:LANES]
        for c in range(1, t // LANES):
            lp = lp + p[:, c * LANES:(c + 1) * LANES]
        pv = jnp.dot(p.astype(BF16), vs, preferred_element_type=F32)
        if first:
            l_ref[half] = lp
            acc_ref[half] = pv
        elif stabilised:
            l_ref[half] = alpha * l_ref[half] + lp
            acc_ref[half] = alpha * acc_ref[half] + pv
        else:
            l_ref[half] += lp
            acc_ref[half] += pv

    def run(stabilised):
        kv = keys(2 * qi)
        block(0, kv, True, True, stabilised)
        block(1, kv, False, True, stabilised)
        block(1, keys(2 * qi + 1), True, False, stabilised)

        def body(kb, carry):
            kv = keys(kb)
            block(0, kv, False, False, stabilised)
            block(1, kv, False, False, stabilised)
            return carry

        lax.fori_loop(0, 2 * qi, body, 0)

    @pl.when(bounded_ref[0] != 0)
    def _():
        run(False)

    @pl.when(bounded_ref[0] == 0)
    def _():
        run(True)

    lv = lam_ref[...]
    lam = (jnp.exp(jnp.sum(lv[0:1] * lv[1:2], axis=1, keepdims=True))
           - jnp.exp(jnp.sum(lv[2:3] * lv[3:4], axis=1, keepdims=True)) + LAMBDA_INIT)
    for half in range(2):
        o = acc_ref[half] / jnp.sum(l_ref[half], axis=1, keepdims=True)
        od = o[0:t] - lam * o[t:2 * t]
        ms = jnp.mean(od * od, axis=-1, keepdims=True)
        o_ref[0, half * t:(half + 1) * t, :] = (
            (od * lax.rsqrt(ms + EPS)) * gout_ref[...] * (1.0 - LAMBDA_INIT)).astype(BF16)


def _attention(bounded, q, k, v, lam_vecs, g_out):
    bsz, s, _ = q.shape
    t = ATT_T
    blk = pl.BlockSpec((1, 2 * t, DA_V_DIM), lambda b, h, i: (b, i, h))
    full = pl.BlockSpec((1, s, DA_V_DIM), lambda b, h, i: (b, 0, h))
    return pl.pallas_call(
        _attn_kernel,
        grid=(bsz, DA_HEADS, s // (2 * t)),
        in_specs=[pl.BlockSpec(memory_space=pltpu.SMEM),
                  blk, full, full, _const_spec(lam_vecs.shape), _const_spec((1, DA_V_DIM))],
        out_specs=blk,
        out_shape=jax.ShapeDtypeStruct((bsz, s, DA_WIDTH), BF16),
        scratch_shapes=[pltpu.VMEM((2, 2 * t, DA_V_DIM), BF16),
                        pltpu.VMEM((2, 2 * t, 1), F32),
                        pltpu.VMEM((2, 2 * t, LANES), F32),
                        pltpu.VMEM((2, 2 * t, DA_V_DIM), F32)],
        compiler_params=pltpu.CompilerParams(
            dimension_semantics=("arbitrary", "arbitrary", "arbitrary"), vmem_limit_bytes=VMEM_LIMIT),
        name="diff_attention",
    )(bounded, q, k, v, lam_vecs, g_out.reshape(1, DA_V_DIM))


def _mlstm_kernel(q_ref, k_ref, vt_ref, o_ref, gr_ref, gout_ref, y_ref, ct_ref, m_ref):
    ln = ML_CHUNK
    nh = ML_HEADS
    nt = (((1,), (1,)), ((), ()))

    @pl.when(pl.program_id(1) == 0)
    def _():
        ct_ref[...] = jnp.zeros(ct_ref.shape, F32)
        m_ref[...] = jnp.zeros(m_ref.shape, F32)

    row = lax.broadcasted_iota(jnp.int32, (ln, ln), 0)
    col = lax.broadcasted_iota(jnp.int32, (ln, ln), 1)
    lower = col <= row
    ones_row = (lax.broadcasted_iota(jnp.int32, (ML_DIM, ln), 0) == 0).astype(BF16)

    for r in range(ML_ROWS):
        gr = gr_ref[r]
        c, b = gr[0:nh], gr[nh:2 * nh]
        m_prev = jnp.concatenate([m_ref[r, 0:nh, :]] * (ln // LANES), axis=1)
        m_run = jnp.maximum(gr[2 * nh:3 * nh], m_prev)
        m_last = m_run[:, ln - 1:ln]
        e_inter = jnp.exp(m_prev - m_run)
        e_floor = jnp.exp(-(b + m_run))
        w_in = jnp.exp(c - m_last)
        decay = e_inter[:, ln - 1:ln]
        m_ref[r, 0:nh, :] = jnp.broadcast_to(b[:, ln - 1:ln] + m_last, (nh, LANES))
        m_run_cols = jnp.concatenate([m_run, m_run], axis=0).T

        for h in range(nh):
            lanes = slice(h * ML_DIM, (h + 1) * ML_DIM)
            q = q_ref[r, :, lanes]
            k = k_ref[r, :, lanes]
            vt_aug = jnp.concatenate([vt_ref[r, lanes, :], ones_row], axis=0)
            s = lax.dot_general(q, k, nt, preferred_element_type=F32)
            gate = jnp.exp(jnp.where(lower, c[h:h + 1, :] - m_run_cols[:, h:h + 1], -jnp.inf))
            w = (s * gate).astype(BF16)
            state_t = ct_ref[r, h]
            full_t = (e_inter[h:h + 1, :]
                      * lax.dot_general(state_t.astype(BF16), q, nt, preferred_element_type=F32)
                      + lax.dot_general(vt_aug, w, nt, preferred_element_type=F32))
            den = full_t[ML_DIM:ML_DIM + 1, :]
            hh_t = full_t[0:ML_DIM, :] / jnp.maximum(jnp.abs(den), e_floor[h:h + 1, :])
            hn_t = hh_t * lax.rsqrt(jnp.mean(hh_t * hh_t, axis=0, keepdims=True) + EPS)
            y_ref[r, :, lanes] = (o_ref[r, :, lanes].astype(F32) * (hn_t.T * gout_ref[:, lanes])).astype(BF16)

            vw = (vt_aug.astype(F32) * w_in[h:h + 1, :]).astype(BF16)
            ct_ref[r, h] = decay[h:h + 1, :] * state_t + jnp.dot(vw, k, preferred_element_type=F32)


def _mlstm(q, k, vt, o, gates_row, g_out):
    bsz, s, _ = q.shape
    ln = ML_CHUNK
    rows = ML_ROWS
    tok = pl.BlockSpec((rows, ln, ML_WIDTH), lambda b, c: (b, c, 0))
    return pl.pallas_call(
        _mlstm_kernel,
        grid=(bsz // rows, s // ln),
        in_specs=[tok, tok, pl.BlockSpec((rows, ML_WIDTH, ln), lambda b, c: (b, 0, c)), tok,
                  pl.BlockSpec((rows, GATE_ROWS, ln), lambda b, c: (b, 0, c)),
                  _const_spec((1, ML_WIDTH))],
        out_specs=tok,
        out_shape=jax.ShapeDtypeStruct((bsz, s, ML_WIDTH), BF16),
        scratch_shapes=[pltpu.VMEM((rows, ML_HEADS, 2 * ML_DIM, ML_DIM), F32),
                        pltpu.VMEM((rows, 8, LANES), F32)],
        compiler_params=pltpu.CompilerParams(
            dimension_semantics=("arbitrary", "arbitrary"), vmem_limit_bytes=VMEM_LIMIT),
        name="mlstm",
    )(q, k, vt, o, gates_row, g_out.reshape(1, ML_WIDTH))


def _rope_tables(s):
    half = ROPE_DIM // 2
    pos = jnp.arange(s, dtype=F32)
    inv_freq = ROPE_THETA ** (-jnp.arange(0, ROPE_DIM, 2, dtype=F32) / ROPE_DIM)
    ang = pos[:, None] * inv_freq[None, :]
    cos, sin = jnp.cos(ang), jnp.sin(ang)
    rest = DA_QK_DIM - ROPE_DIM
    cos_g = jnp.concatenate([cos, cos, jnp.ones((s, rest), F32)], axis=1)
    sina_g = jnp.concatenate([-sin, jnp.zeros((s, DA_QK_DIM - half), F32)], axis=1)
    sinb_g = jnp.concatenate([jnp.zeros((s, half), F32), sin, jnp.zeros((s, rest), F32)], axis=1)
    rep = LANES // DA_QK_DIM
    return jnp.tile(cos_g, (1, rep)), jnp.tile(sina_g, (1, rep)), jnp.tile(sinb_g, (1, rep))


def kernel(x, c, w_ada, b_ada, g_norm, ffn1_w12, ffn1_w3, w_in, conv_w, conv_b, b_igate, b_fgate,
           g_qnorm, g_knorm, lambda_qk, g_da_out, g_ml_out, w_out, ffn2_w12, ffn2_w3):
    bsz, s, d = x.shape
    l = 0
    mod = _ada(c, w_ada[l], b_ada[l]).reshape(bsz, 3, 3, d)

    def ffn(xin, sub, w12, w3, mix=None):
        w12b = w12.astype(BF16)
        return _ffn(xin, mod[:, sub], g_norm[l, sub], w12b[:, :D_FF], w12b[:, D_FF:], w3.astype(BF16), mix)

    x = ffn(x, 0, ffn1_w12[l], ffn1_w3[l])

    wb = w_in[l].astype(BF16)
    o0 = 0
    parts = []
    for width in (DA_WIDTH, DA_WIDTH, DA_WIDTH, 2 * ML_WIDTH, ML_WIDTH, ML_WIDTH, 2 * ML_HEADS):
        parts.append(wb[:, o0:o0 + width])
        o0 += width
    wq, wk, wv, wmqk, wmv, wmo, wif = parts
    wif = jnp.pad(wif.T, ((0, 2 * ML_HEADS), (0, 0)))
    wmv = wmv.T
    groups = DA_WIDTH // DA_QK_DIM
    q_gain = g_qnorm[l] * (DA_QK_DIM ** -0.5)
    gq = jnp.tile(q_gain * LOG2E, groups).reshape(1, DA_WIDTH)
    score_bound = 1.05 * DA_QK_DIM * jnp.max(jnp.abs(q_gain)) * jnp.max(jnp.abs(g_knorm[l]))
    bounded = (score_bound <= SAFE_SCORE_BOUND).astype(jnp.int32).reshape(1)
    gk = jnp.tile(g_knorm[l], groups).reshape(1, DA_WIDTH)
    cos, sina, sinb = _rope_tables(s)
    gid = jnp.arange(DA_WIDTH) // DA_QK_DIM
    bd = (gid[:, None] == gid[None, :]).astype(BF16)
    bif = jnp.concatenate([b_igate[l], b_fgate[l]]).reshape(2 * ML_HEADS, 1)
    da_q, da_k, da_v, ml_q, ml_k, ml_v, ml_o, gates = _inproj(
        x, mod[:, 1], g_norm[l, 1], wq, wk, wv, wmqk, wmv, wmo, wif, gq, gk, cos, sina, sinb, bd,
        conv_w[l], conv_b[l].reshape(1, 2 * ML_WIDTH), bif)

    y_da = _attention(bounded, da_q, da_k, da_v, lambda_qk[l], g_da_out[l])
    y_ml = _mlstm(ml_q, ml_k, ml_v, ml_o, gates, g_ml_out[l])

    wo = w_out[l].astype(BF16)
    return ffn(x, 2, ffn2_w12[l], ffn2_w3[l], mix=(mod[:, 1], y_da, y_ml, wo[:DA_WIDTH], wo[DA_WIDTH:]))
```

```python
import functools

import jax
import jax.numpy as jnp
from jax import lax
from jax.experimental import pallas as pl
from jax.experimental.pallas import tpu as pltpu

F32 = jnp.float32
BF16 = jnp.bfloat16

D_MODEL = 1024
DA_HEADS = 4
DA_QK_DIM = 64
DA_V_DIM = 2 * DA_QK_DIM
DA_WIDTH = DA_HEADS * DA_V_DIM
ML_HEADS = 4
ML_DIM = 128
ML_WIDTH = ML_HEADS * ML_DIM
ROPE_THETA = 500000.0
ROPE_DIM = DA_QK_DIM // 4
D_FF = 2816
CONV_K = 4
EPS = 1e-6
LAMBDA_INIT = 0.8 - 0.6 * 1.0
LOG2E = 1.4426950408889634

LANES = 128
MXU_TILE = 256
VMEM_LIMIT = 56 * 1024 * 1024

FFN_TM = 1024
FFN_SUB = 256
PROJ_TM = 1024
ATT_T = 512
SAFE_SCORE_BOUND = 40.0
ML_CHUNK = 256
ML_ROWS = 4
ML_STATE_ROWS = ML_DIM + 16
GATE_ROWS = 16
CONV_HALO = 8


def _const_spec(shape):
    nd = len(shape)
    return pl.BlockSpec(shape, lambda *_: (0,) * nd, pipeline_mode=pl.Buffered(1))


def _sigmoid(x):
    return 1.0 / (1.0 + jnp.exp(-x))


def _mod_norm(x, g, shift, scale):
    ms = jnp.mean(x * x, axis=-1, keepdims=True)
    return (x * lax.rsqrt(ms + EPS)) * (g * (1.0 + scale)) + shift


def _ada_kernel(c_ref, w_ref, b_ref, o_ref):
    c = c_ref[...]
    cs = (c * _sigmoid(c)).astype(BF16)
    o_ref[...] = jnp.dot(cs, w_ref[...].astype(BF16), preferred_element_type=F32) + b_ref[...]


def _ada(c, w_ada, b_ada):
    bsz, d = c.shape
    n = w_ada.shape[1]
    tn = 1024
    return pl.pallas_call(
        _ada_kernel,
        grid=(n // tn,),
        in_specs=[pl.BlockSpec((bsz, d), lambda j: (0, 0)),
                  pl.BlockSpec((d, tn), lambda j: (0, j)),
                  pl.BlockSpec((1, tn), lambda j: (0, j))],
        out_specs=pl.BlockSpec((bsz, tn), lambda j: (0, j)),
        out_shape=jax.ShapeDtypeStruct((bsz, n), F32),
        compiler_params=pltpu.CompilerParams(dimension_semantics=("arbitrary",)),
        name="adaln_mod",
    )(c, w_ada, b_ada.reshape(1, n))


def _ffn_body(x, mod_ref, g_ref, w1_ref, w2_ref, w3_ref, o_ref, act_ref):
    mod = mod_ref[0]
    hb = _mod_norm(x, g_ref[...], mod[0:1], mod[1:2]).astype(BF16)
    for c in range(D_FF // FFN_SUB):
        cols = slice(c * FFN_SUB, (c + 1) * FFN_SUB)
        a = jnp.dot(hb, w1_ref[:, cols], preferred_element_type=F32)
        b = jnp.dot(hb, w2_ref[:, cols], preferred_element_type=F32)
        act_ref[:, cols] = (a * _sigmoid(a) * b).astype(BF16)
    y = jnp.dot(act_ref[...], w3_ref[...], preferred_element_type=F32)
    o_ref[0] = x + (0.5 * (1.0 + mod[2:3])) * y


def _ffn_kernel(x_ref, mod_ref, g_ref, w1_ref, w2_ref, w3_ref, o_ref, act_ref):
    _ffn_body(x_ref[0], mod_ref, g_ref, w1_ref, w2_ref, w3_ref, o_ref, act_ref)


def _mix_ffn_kernel(x_ref, mmod_ref, ya_ref, ym_ref, wa_ref, wm_ref,
                    mod_ref, g_ref, w1_ref, w2_ref, w3_ref, o_ref, act_ref):
    y = (jnp.dot(ya_ref[0], wa_ref[...], preferred_element_type=F32)
         + jnp.dot(ym_ref[0], wm_ref[...], preferred_element_type=F32))
    x = x_ref[0] + (1.0 + mmod_ref[0][2:3]) * y
    _ffn_body(x, mod_ref, g_ref, w1_ref, w2_ref, w3_ref, o_ref, act_ref)


def _ffn(x, mod3, g, w1, w2, w3, mix=None):
    bsz, s, d = x.shape
    tm = FFN_TM
    tok_spec = lambda width: pl.BlockSpec((1, tm, width), lambda b, i: (b, i, 0))
    mod_spec = pl.BlockSpec((1, 3, d), lambda b, i: (b, 0, 0))
    ffn_specs = [mod_spec, _const_spec((1, d)),
                 _const_spec(w1.shape), _const_spec(w2.shape), _const_spec(w3.shape)]
    ffn_args = (mod3, g.reshape(1, d), w1, w2, w3)
    if mix is None:
        body, in_specs, args = _ffn_kernel, [tok_spec(d)] + ffn_specs, (x,) + ffn_args
    else:
        mmod3, y_da, y_ml, w_a, w_m = mix
        body = _mix_ffn_kernel
        in_specs = [tok_spec(d), mod_spec, tok_spec(DA_WIDTH), tok_spec(ML_WIDTH),
                    _const_spec(w_a.shape), _const_spec(w_m.shape)] + ffn_specs
        args = (x, mmod3, y_da, y_ml, w_a, w_m) + ffn_args
    return pl.pallas_call(
        body,
        grid=(bsz, s // tm),
        in_specs=in_specs,
        out_specs=tok_spec(d),
        out_shape=jax.ShapeDtypeStruct(x.shape, F32),
        scratch_shapes=[pltpu.VMEM((tm, D_FF), BF16)],
        compiler_params=pltpu.CompilerParams(
            dimension_semantics=("arbitrary", "arbitrary"), vmem_limit_bytes=VMEM_LIMIT),
        name="ffn" if mix is None else "mix_ffn",
    )(*args)


def _group_norm_rope(u, gvec, bd, cos, sina, sinb):
    x2 = (u * u).astype(BF16)
    ssq = jnp.concatenate(
        [jnp.dot(x2[:, j:j + MXU_TILE], bd, preferred_element_type=F32) for j in range(0, u.shape[1], MXU_TILE)],
        axis=1)
    xn = (u * lax.rsqrt(ssq * (1.0 / DA_QK_DIM) + EPS)) * gvec
    outs = []
    for h in range(DA_HEADS):
        xh = xn[:, h * LANES:(h + 1) * LANES]
        up = pltpu.roll(xh, LANES - ROPE_DIM // 2, 1)
        dn = pltpu.roll(xh, ROPE_DIM // 2, 1)
        outs.append(xh * cos + up * sina + dn * sinb)
    return jnp.concatenate(outs, axis=1)


def _chunk_scan(x, op, identity):
    pos = lax.broadcasted_iota(jnp.int32, x.shape, 1) & (ML_CHUNK - 1)
    d = 1
    while d < ML_CHUNK:
        x = op(x, jnp.where(pos >= d, pltpu.roll(x, d, 1), identity))
        d *= 2
    return x


def _inproj_kernel(x_ref, mod_ref, g_ref, wq_ref, wk_ref, wv_ref, wmqk_ref, wmv_ref, wmo_ref, wif_ref,
                   gq_ref, gk_ref, cos_ref, sina_ref, sinb_ref, bd_ref, cw_ref, cb_ref, bif_ref,
                   q_out, k_out, v_out, mq_out, mk_out, mv_out, mo_out, gate_out, ext_ref):
    tm = x_ref.shape[1]
    si = pl.program_id(1)
    mod = mod_ref[0]
    hb = _mod_norm(x_ref[0], g_ref[...], mod[0:1], mod[1:2]).astype(BF16)

    nh = ML_HEADS
    pre = lax.dot_general(wif_ref[...], hb, (((1,), (1,)), ((), ())),
                          preferred_element_type=F32)[0:2 * nh] + bif_ref[...]
    zf = pre[nh:2 * nh]
    log_f = jnp.minimum(zf, 0.0) - jnp.log1p(jnp.exp(-jnp.abs(zf)))
    b = _chunk_scan(log_f, jnp.add, 0.0)
    c = pre[0:nh] - b
    gate_out[0] = jnp.concatenate([c, b, _chunk_scan(c, jnp.maximum, -jnp.inf), jnp.zeros_like(c)], axis=0)

    cos, sina, sinb = cos_ref[...], sina_ref[...], sinb_ref[...]
    bd = bd_ref[...]
    uq = jnp.dot(hb, wq_ref[...], preferred_element_type=F32)
    q_out[0] = _group_norm_rope(uq, gq_ref[...], bd, cos, sina, sinb).astype(BF16)
    uk = jnp.dot(hb, wk_ref[...], preferred_element_type=F32)
    k_out[0] = _group_norm_rope(uk, gk_ref[...], bd, cos, sina, sinb).astype(BF16)
    v_out[0] = jnp.dot(hb, wv_ref[...], preferred_element_type=F32).astype(BF16)

    ext_ref[0:CONV_HALO, :] = jnp.where(si == 0, 0.0, ext_ref[0:CONV_HALO, :])
    uqk = jnp.dot(hb, wmqk_ref[...], preferred_element_type=F32)
    ext_ref[CONV_HALO:CONV_HALO + tm, :] = uqk
    cw = cw_ref[...]
    acc = cb_ref[...]
    for j in range(CONV_K - 1):
        off = CONV_HALO - (CONV_K - 1) + j
        acc = acc + ext_ref[off:off + tm, :] * cw[j:j + 1]
    acc = acc + uqk * cw[CONV_K - 1:CONV_K]
    ext_ref[0:CONV_HALO, :] = uqk[tm - CONV_HALO:tm, :]
    qk = acc * _sigmoid(acc)
    mq_out[0] = (qk[:, :ML_WIDTH] * (ML_DIM ** -0.5)).astype(BF16)
    mk_out[0] = qk[:, ML_WIDTH:].astype(BF16)

    mv_out[0] = lax.dot_general(wmv_ref[...], hb, (((1,), (1,)), ((), ())),
                                preferred_element_type=F32).astype(BF16)
    mo_out[0] = _sigmoid(jnp.dot(hb, wmo_ref[...], preferred_element_type=F32)).astype(BF16)


def _inproj(x, mod3, g, wq, wk, wv, wmqk, wmv, wmo, wif, gq, gk, cos, sina, sinb, bd, cw, cb, bif):
    bsz, s, d = x.shape
    tm = PROJ_TM
    tok = lambda width, dt: jax.ShapeDtypeStruct((bsz, s, width), dt)
    tok_spec = lambda width: pl.BlockSpec((1, tm, width), lambda b, i: (b, i, 0))
    tab_spec = pl.BlockSpec((tm, LANES), lambda b, i: (i, 0))
    return pl.pallas_call(
        _inproj_kernel,
        grid=(bsz, s // tm),
        in_specs=[tok_spec(d),
                  pl.BlockSpec((1, 3, d), lambda b, i: (b, 0, 0)),
                  _const_spec((1, d)),
                  _const_spec(wq.shape), _const_spec(wk.shape), _const_spec(wv.shape),
                  _const_spec(wmqk.shape), _const_spec(wmv.shape), _const_spec(wmo.shape),
                  _const_spec(wif.shape),
                  _const_spec(gq.shape), _const_spec(gk.shape),
                  tab_spec, tab_spec, tab_spec,
                  _const_spec(bd.shape), _const_spec(cw.shape), _const_spec(cb.shape),
                  _const_spec(bif.shape)],
        out_specs=[tok_spec(DA_WIDTH), tok_spec(DA_WIDTH), tok_spec(DA_WIDTH),
                   tok_spec(ML_WIDTH), tok_spec(ML_WIDTH),
                   pl.BlockSpec((1, ML_WIDTH, tm), lambda b, i: (b, 0, i)),
                   tok_spec(ML_WIDTH),
                   pl.BlockSpec((1, GATE_ROWS, tm), lambda b, i: (b, 0, i))],
        out_shape=[tok(DA_WIDTH, BF16), tok(DA_WIDTH, BF16), tok(DA_WIDTH, BF16),
                   tok(ML_WIDTH, BF16), tok(ML_WIDTH, BF16),
                   jax.ShapeDtypeStruct((bsz, ML_WIDTH, s), BF16),
                   tok(ML_WIDTH, BF16),
                   jax.ShapeDtypeStruct((bsz, GATE_ROWS, s), F32)],
        scratch_shapes=[pltpu.VMEM((tm + CONV_HALO, 2 * ML_WIDTH), F32)],
        compiler_params=pltpu.CompilerParams(
            dimension_semantics=("arbitrary", "arbitrary"), vmem_limit_bytes=VMEM_LIMIT),
        name="in_proj",
    )(x, mod3, g.reshape(1, d), wq, wk, wv, wmqk, wmv, wmo, wif, gq, gk, cos, sina, sinb, bd, cw, cb, bif)


def _attn_kernel(bounded_ref, q_ref, k_ref, v_ref, lam_ref, gout_ref, o_ref, qs_ref, m_ref, acc_ref):
    t = ATT_T
    qi = pl.program_id(2)

    u = t // 2
    lane = lax.broadcasted_iota(jnp.int32, (u, DA_V_DIM), 1)
    for half in range(2):
        for part in range(2):
            q = q_ref[0, half * t + part * u:half * t + (part + 1) * u, :]
            zero = jnp.zeros_like(q)
            qs_ref[half, (2 * part) * u:(2 * part + 1) * u, :] = jnp.where(lane < DA_QK_DIM, q, zero)
            qs_ref[half, (2 * part + 1) * u:(2 * part + 2) * u, :] = jnp.where(lane >= DA_QK_DIM, q, zero)

    ones_cols = jnp.ones((t, DA_V_DIM), BF16)

    def keys(kb):
        start = pl.multiple_of(kb * t, t)
        return k_ref[0, pl.ds(start, t), :], jnp.concatenate([v_ref[0, pl.ds(start, t), :], ones_cols], axis=1)

    def piece(half, rows, ks, vs, causal_shift, first, stabilised):
        s = lax.dot_general(qs_ref[half, rows, :], ks, (((1,), (1,)), ((), ())), preferred_element_type=F32)
        if causal_shift is not None:
            row = lax.broadcasted_iota(jnp.int32, s.shape, 0) & (u - 1)
            col = lax.broadcasted_iota(jnp.int32, s.shape, 1)
            s = jnp.where(col <= row + causal_shift, s, -jnp.inf)
        if stabilised:
            m_new = jnp.max(s, axis=1, keepdims=True)
            if not first:
                m_prev = m_ref[half, rows, :]
                m_new = jnp.maximum(m_prev, m_new)
                alpha = jnp.exp2(m_prev - m_new)
            m_ref[half, rows, :] = m_new
            s = s - m_new
        pv = jnp.dot(jnp.exp2(s).astype(BF16), vs, preferred_element_type=F32)
        if first:
            acc_ref[half, rows, :] = pv
        elif stabilised:
            acc_ref[half, rows, :] = alpha * acc_ref[half, rows, :] + pv
        else:
            acc_ref[half, rows, :] += pv

    def block(half, kv, diagonal, first, stabilised):
        ks, vs = kv
        if diagonal:
            piece(half, slice(0, 2 * u), ks[0:u], vs[0:u], 0, first, stabilised)
            piece(half, slice(2 * u, 4 * u), ks, vs, u, first, stabilised)
        else:
            piece(half, slice(0, 4 * u), ks, vs, None, first, stabilised)

    def run(stabilised):
        kv = keys(2 * qi)
        block(0, kv, True, True, stabilised)
        block(1, kv, False, True, stabilised)
        block(1, keys(2 * qi + 1), True, False, stabilised)

        def body(pair, carry):
            for j in range(2):
                kv = keys(2 * pair + j)
                block(0, kv, False, False, stabilised)
                block(1, kv, False, False, stabilised)
            return carry

        lax.fori_loop(0, qi, body, 0)

    @pl.when(bounded_ref[0] != 0)
    def _():
        run(False)

    @pl.when(bounded_ref[0] == 0)
    def _():
        run(True)

    lv = lam_ref[...]
    lam = (jnp.exp(jnp.sum(lv[0:1] * lv[1:2], axis=1, keepdims=True))
           - jnp.exp(jnp.sum(lv[2:3] * lv[3:4], axis=1, keepdims=True)) + LAMBDA_INIT)
    for half in range(2):
        for part in range(2):
            rows = slice(2 * part * u, (2 * part + 2) * u)
            o = acc_ref[half, rows, 0:DA_V_DIM] / acc_ref[half, rows, DA_V_DIM:2 * DA_V_DIM]
            od = o[0:u] - lam * o[u:2 * u]
            ms = jnp.mean(od * od, axis=-1, keepdims=True)
            o_ref[0, half * t + part * u:half * t + (part + 1) * u, :] = (
                (od * lax.rsqrt(ms + EPS)) * gout_ref[...] * (1.0 - LAMBDA_INIT)).astype(BF16)


def _attention(bounded, q, k, v, lam_vecs, g_out):
    bsz, s, _ = q.shape
    t = ATT_T
    blk = pl.BlockSpec((1, 2 * t, DA_V_DIM), lambda b, h, i: (b, i, h))
    full = pl.BlockSpec((1, s, DA_V_DIM), lambda b, h, i: (b, 0, h))
    return pl.pallas_call(
        _attn_kernel,
        grid=(bsz, DA_HEADS, s // (2 * t)),
        in_specs=[pl.BlockSpec(memory_space=pltpu.SMEM),
                  blk, full, full, _const_spec(lam_vecs.shape), _const_spec((1, DA_V_DIM))],
        out_specs=blk,
        out_shape=jax.ShapeDtypeStruct((bsz, s, DA_WIDTH), BF16),
        scratch_shapes=[pltpu.VMEM((2, 2 * t, DA_V_DIM), BF16),
                        pltpu.VMEM((2, 2 * t, 1), F32),
                        pltpu.VMEM((2, 2 * t, 2 * DA_V_DIM), F32)],
        compiler_params=pltpu.CompilerParams(
            dimension_semantics=("arbitrary", "arbitrary", "arbitrary"), vmem_limit_bytes=VMEM_LIMIT),
        name="diff_attention",
    )(bounded, q, k, v, lam_vecs, g_out.reshape(1, DA_V_DIM))


def _mlstm_kernel(q_ref, k_ref, vt_ref, o_ref, gr_ref, gout_ref, y_ref, ct_ref, m_ref):
    ln = ML_CHUNK
    nh = ML_HEADS
    nt = (((1,), (1,)), ((), ()))

    @pl.when(pl.program_id(1) == 0)
    def _():
        ct_ref[...] = jnp.zeros(ct_ref.shape, F32)
        m_ref[...] = jnp.zeros(m_ref.shape, F32)

    row = lax.broadcasted_iota(jnp.int32, (ln, ln), 0)
    col = lax.broadcasted_iota(jnp.int32, (ln, ln), 1)
    lower = col <= row
    ones_row = (lax.broadcasted_iota(jnp.int32, (ML_STATE_ROWS - ML_DIM, ln), 0) == 0).astype(BF16)

    for r in range(ML_ROWS):
        gr = gr_ref[r]
        c, b = gr[0:nh], gr[nh:2 * nh]
        m_prev = jnp.concatenate([m_ref[r, 0:nh, :]] * (ln // LANES), axis=1)
        m_run = jnp.maximum(gr[2 * nh:3 * nh], m_prev)
        m_last = m_run[:, ln - 1:ln]
        e_inter = jnp.exp(m_prev - m_run)
        e_floor = jnp.exp(-(b + m_run))
        w_in = jnp.exp(c - m_last)
        decay = e_inter[:, ln - 1:ln]
        m_ref[r, 0:nh, :] = jnp.broadcast_to(b[:, ln - 1:ln] + m_last, (nh, LANES))
        m_run_cols = jnp.concatenate([m_run, m_run], axis=0).T

        for h in range(nh):
            lanes = slice(h * ML_DIM, (h + 1) * ML_DIM)
            q = q_ref[r, :, lanes]
            k = k_ref[r, :, lanes]
            vt_aug = jnp.concatenate([vt_ref[r, lanes, :], ones_row], axis=0)
            s = lax.dot_general(q, k, nt, preferred_element_type=F32)
            gate = jnp.exp(jnp.where(lower, c[h:h + 1, :] - m_run_cols[:, h:h + 1], -jnp.inf))
            w = (s * gate).astype(BF16)
            state_t = ct_ref[r, h]
            full_t = (e_inter[h:h + 1, :]
                      * lax.dot_general(state_t.astype(BF16), q, nt, preferred_element_type=F32)
                      + lax.dot_general(vt_aug, w, nt, preferred_element_type=F32))
            den = full_t[ML_DIM:ML_DIM + 1, :]
            hh_t = full_t[0:ML_DIM, :] / jnp.maximum(jnp.abs(den), e_floor[h:h + 1, :])
            hn_t = hh_t * lax.rsqrt(jnp.mean(hh_t * hh_t, axis=0, keepdims=True) + EPS)
            y_ref[r, :, lanes] = (o_ref[r, :, lanes].astype(F32) * (hn_t.T * gout_ref[:, lanes])).astype(BF16)

            vw = (vt_aug.astype(F32) * w_in[h:h + 1, :]).astype(BF16)
            ct_ref[r, h] = decay[h:h + 1, :] * state_t + jnp.dot(vw, k, preferred_element_type=F32)


def _mlstm(q, k, vt, o, gates_row, g_out):
    bsz, s, _ = q.shape
    ln = ML_CHUNK
    rows = ML_ROWS
    tok = pl.BlockSpec((rows, ln, ML_WIDTH), lambda b, c: (b, c, 0))
    return pl.pallas_call(
        _mlstm_kernel,
        grid=(bsz // rows, s // ln),
        in_specs=[tok, tok, pl.BlockSpec((rows, ML_WIDTH, ln), lambda b, c: (b, 0, c)), tok,
                  pl.BlockSpec((rows, GATE_ROWS, ln), lambda b, c: (b, 0, c)),
                  _const_spec((1, ML_WIDTH))],
        out_specs=tok,
        out_shape=jax.ShapeDtypeStruct((bsz, s, ML_WIDTH), BF16),
        scratch_shapes=[pltpu.VMEM((rows, ML_HEADS, ML_STATE_ROWS, ML_DIM), F32),
                        pltpu.VMEM((rows, 8, LANES), F32)],
        compiler_params=pltpu.CompilerParams(
            dimension_semantics=("arbitrary", "arbitrary"), vmem_limit_bytes=VMEM_LIMIT),
        name="mlstm",
    )(q, k, vt, o, gates_row, g_out.reshape(1, ML_WIDTH))


def _rope_tables(s):
    half = ROPE_DIM // 2
    pos = jnp.arange(s, dtype=F32)
    inv_freq = ROPE_THETA ** (-jnp.arange(0, ROPE_DIM, 2, dtype=F32) / ROPE_DIM)
    ang = pos[:, None] * inv_freq[None, :]
    cos, sin = jnp.cos(ang), jnp.sin(ang)
    rest = DA_QK_DIM - ROPE_DIM
    cos_g = jnp.concatenate([cos, cos, jnp.ones((s, rest), F32)], axis=1)
    sina_g = jnp.concatenate([-sin, jnp.zeros((s, DA_QK_DIM - half), F32)], axis=1)
    sinb_g = jnp.concatenate([jnp.zeros((s, half), F32), sin, jnp.zeros((s, rest), F32)], axis=1)
    rep = LANES // DA_QK_DIM
    return jnp.tile(cos_g, (1, rep)), jnp.tile(sina_g, (1, rep)), jnp.tile(sinb_g, (1, rep))


def kernel(x, c, w_ada, b_ada, g_norm, ffn1_w12, ffn1_w3, w_in, conv_w, conv_b, b_igate, b_fgate,
           g_qnorm, g_knorm, lambda_qk, g_da_out, g_ml_out, w_out, ffn2_w12, ffn2_w3):
    bsz, s, d = x.shape
    l = 0
    mod = _ada(c, w_ada[l], b_ada[l]).reshape(bsz, 3, 3, d)

    def ffn(xin, sub, w12, w3, mix=None):
        w12b = w12.astype(BF16)
        return _ffn(xin, mod[:, sub], g_norm[l, sub], w12b[:, :D_FF], w12b[:, D_FF:], w3.astype(BF16), mix)

    x = ffn(x, 0, ffn1_w12[l], ffn1_w3[l])

    wb = w_in[l].astype(BF16)
    o0 = 0
    parts = []
    for width in (DA_WIDTH, DA_WIDTH, DA_WIDTH, 2 * ML_WIDTH, ML_WIDTH, ML_WIDTH, 2 * ML_HEADS):
        parts.append(wb[:, o0:o0 + width])
        o0 += width
    wq, wk, wv, wmqk, wmv, wmo, wif = parts
    wif = jnp.pad(wif.T, ((0, 2 * ML_HEADS), (0, 0)))
    wmv = wmv.T
    groups = DA_WIDTH // DA_QK_DIM
    q_gain = g_qnorm[l] * (DA_QK_DIM ** -0.5)
    gq = jnp.tile(q_gain * LOG2E, groups).reshape(1, DA_WIDTH)
    score_bound = 1.05 * DA_QK_DIM * jnp.max(jnp.abs(q_gain)) * jnp.max(jnp.abs(g_knorm[l]))
    bounded = (score_bound <= SAFE_SCORE_BOUND).astype(jnp.int32).reshape(1)
    gk = jnp.tile(g_knorm[l], groups).reshape(1, DA_WIDTH)
    cos, sina, sinb = _rope_tables(s)
    gid = jnp.arange(MXU_TILE) // DA_QK_DIM
    bd = (gid[:, None] == gid[None, :]).astype(BF16)
    bif = jnp.concatenate([b_igate[l], b_fgate[l]]).reshape(2 * ML_HEADS, 1)
    da_q, da_k, da_v, ml_q, ml_k, ml_v, ml_o, gates = _inproj(
        x, mod[:, 1], g_norm[l, 1], wq, wk, wv, wmqk, wmv, wmo, wif, gq, gk, cos, sina, sinb, bd,
        conv_w[l], conv_b[l].reshape(1, 2 * ML_WIDTH), bif)

    y_da = _attention(bounded, da_q, da_k, da_v, lambda_qk[l], g_da_out[l])
    y_ml = _mlstm(ml_q, ml_k, ml_v, ml_o, gates, g_ml_out[l])

    wo = w_out[l].astype(BF16)
    return ffn(x, 2, ffn2_w12[l], ffn2_w3[l], mix=(mod[:, 1], y_da, y_ml, wo[:DA_WIDTH], wo[DA_WIDTH:]))
```

```python
import functools

import jax
import jax.numpy as jnp
from jax import lax
from jax.experimental import pallas as pl
from jax.experimental.pallas import tpu as pltpu

F32 = jnp.float32
BF16 = jnp.bfloat16

D_MODEL = 1024
DA_HEADS = 4
DA_QK_DIM = 64
DA_V_DIM = 2 * DA_QK_DIM
DA_WIDTH = DA_HEADS * DA_V_DIM
ML_HEADS = 4
ML_DIM = 128
ML_WIDTH = ML_HEADS * ML_DIM
ROPE_THETA = 500000.0
ROPE_DIM = DA_QK_DIM // 4
D_FF = 2816
CONV_K = 4
EPS = 1e-6
LAMBDA_INIT = 0.8 - 0.6 * 1.0
LOG2E = 1.4426950408889634

LANES = 128
MXU_TILE = 256
VMEM_LIMIT = 56 * 1024 * 1024

FFN_TM = 1024
FFN_SUB = 256
PROJ_TM = 1024
PROJ_SUB = 256
CONV_COLS = 256
ATT_T = 512
SAFE_SCORE_BOUND = 40.0
ML_CHUNK = 256
ML_ROWS = 4
ML_STATE_ROWS = ML_DIM + 16
GATE_ROWS = 16
CONV_HALO = 8


def _const_spec(shape):
    nd = len(shape)
    return pl.BlockSpec(shape, lambda *_: (0,) * nd, pipeline_mode=pl.Buffered(1))


def _sigmoid(x):
    return 1.0 / (1.0 + jnp.exp(-x))


def _mod_norm(x, g, shift, scale):
    ms = jnp.mean(x * x, axis=-1, keepdims=True)
    return (x * lax.rsqrt(ms + EPS)) * (g * (1.0 + scale)) + shift


def _ada_kernel(c_ref, w_ref, b_ref, o_ref):
    c = c_ref[...]
    cs = (c * _sigmoid(c)).astype(BF16)
    o_ref[...] = jnp.dot(cs, w_ref[...].astype(BF16), preferred_element_type=F32) + b_ref[...]


def _ada(c, w_ada, b_ada):
    bsz, d = c.shape
    n = w_ada.shape[1]
    tn = 1024
    return pl.pallas_call(
        _ada_kernel,
        grid=(n // tn,),
        in_specs=[pl.BlockSpec((bsz, d), lambda j: (0, 0)),
                  pl.BlockSpec((d, tn), lambda j: (0, j)),
                  pl.BlockSpec((1, tn), lambda j: (0, j))],
        out_specs=pl.BlockSpec((bsz, tn), lambda j: (0, j)),
        out_shape=jax.ShapeDtypeStruct((bsz, n), F32),
        compiler_params=pltpu.CompilerParams(dimension_semantics=("arbitrary",)),
        name="adaln_mod",
    )(c, w_ada, b_ada.reshape(1, n))


def _ffn_body(x, mod_ref, g_ref, w1_ref, w2_ref, w3_ref, o_ref, act_ref):
    mod = mod_ref[0]
    hb = _mod_norm(x, g_ref[...], mod[0:1], mod[1:2]).astype(BF16)
    for c in range(D_FF // FFN_SUB):
        cols = slice(c * FFN_SUB, (c + 1) * FFN_SUB)
        a = jnp.dot(hb, w1_ref[:, cols], preferred_element_type=F32)
        b = jnp.dot(hb, w2_ref[:, cols], preferred_element_type=F32)
        act_ref[:, cols] = (a * _sigmoid(a) * b).astype(BF16)
    y = jnp.dot(act_ref[...], w3_ref[...], preferred_element_type=F32)
    o_ref[0] = x + (0.5 * (1.0 + mod[2:3])) * y


def _ffn_kernel(x_ref, mod_ref, g_ref, w1_ref, w2_ref, w3_ref, o_ref, act_ref):
    _ffn_body(x_ref[0], mod_ref, g_ref, w1_ref, w2_ref, w3_ref, o_ref, act_ref)


def _mix_ffn_kernel(x_ref, mmod_ref, ya_ref, ym_ref, wa_ref, wm_ref,
                    mod_ref, g_ref, w1_ref, w2_ref, w3_ref, o_ref, act_ref):
    y = (jnp.dot(ya_ref[0], wa_ref[...], preferred_element_type=F32)
         + jnp.dot(ym_ref[0], wm_ref[...], preferred_element_type=F32))
    x = x_ref[0] + (1.0 + mmod_ref[0][2:3]) * y
    _ffn_body(x, mod_ref, g_ref, w1_ref, w2_ref, w3_ref, o_ref, act_ref)


def _ffn(x, mod3, g, w1, w2, w3, mix=None):
    bsz, s, d = x.shape
    tm = FFN_TM
    tok_spec = lambda width: pl.BlockSpec((1, tm, width), lambda b, i: (b, i, 0))
    mod_spec = pl.BlockSpec((1, 3, d), lambda b, i: (b, 0, 0))
    ffn_specs = [mod_spec, _const_spec((1, d)),
                 _const_spec(w1.shape), _const_spec(w2.shape), _const_spec(w3.shape)]
    ffn_args = (mod3, g.reshape(1, d), w1, w2, w3)
    if mix is None:
        body, in_specs, args = _ffn_kernel, [tok_spec(d)] + ffn_specs, (x,) + ffn_args
    else:
        mmod3, y_da, y_ml, w_a, w_m = mix
        body = _mix_ffn_kernel
        in_specs = [tok_spec(d), mod_spec, tok_spec(DA_WIDTH), tok_spec(ML_WIDTH),
                    _const_spec(w_a.shape), _const_spec(w_m.shape)] + ffn_specs
        args = (x, mmod3, y_da, y_ml, w_a, w_m) + ffn_args
    return pl.pallas_call(
        body,
        grid=(bsz, s // tm),
        in_specs=in_specs,
        out_specs=tok_spec(d),
        out_shape=jax.ShapeDtypeStruct(x.shape, F32),
        scratch_shapes=[pltpu.VMEM((tm, D_FF), BF16)],
        compiler_params=pltpu.CompilerParams(
            dimension_semantics=("arbitrary", "arbitrary"), vmem_limit_bytes=VMEM_LIMIT),
        name="ffn" if mix is None else "mix_ffn",
    )(*args)


def _group_norm_rope(u, gvec, bd, cos, sina, sinb):
    x2 = u * u
    ssq = jnp.concatenate(
        [jnp.dot(x2[:, j:j + MXU_TILE].astype(BF16), bd, preferred_element_type=F32)
         for j in range(0, u.shape[1], MXU_TILE)], axis=1)
    xn = (u * lax.rsqrt(ssq * (1.0 / DA_QK_DIM) + EPS)) * gvec
    outs = []
    for h in range(u.shape[1] // LANES):
        xh = xn[:, h * LANES:(h + 1) * LANES]
        up = pltpu.roll(xh, LANES - ROPE_DIM // 2, 1)
        dn = pltpu.roll(xh, ROPE_DIM // 2, 1)
        outs.append(xh * cos + up * sina + dn * sinb)
    return jnp.concatenate(outs, axis=1)


def _chunk_scan(x, op, identity):
    pos = lax.broadcasted_iota(jnp.int32, x.shape, 1) & (ML_CHUNK - 1)
    d = 1
    while d < ML_CHUNK:
        x = op(x, jnp.where(pos >= d, pltpu.roll(x, d, 1), identity))
        d *= 2
    return x


def _inproj_kernel(x_ref, mod_ref, g_ref, wq_ref, wk_ref, wv_ref, wmqk_ref, wmv_ref, wmo_ref, wif_ref,
                   gq_ref, gk_ref, cos_ref, sina_ref, sinb_ref, bd_ref, cw_ref, cb_ref, bif_ref,
                   q_out, k_out, v_out, mq_out, mk_out, mv_out, mo_out, gate_out, ext_ref):
    tm = x_ref.shape[1]
    si = pl.program_id(1)
    mod = mod_ref[0]
    nh = ML_HEADS
    nt = (((1,), (1,)), ((), ()))
    bd = bd_ref[...]
    cw = cw_ref[...]

    ext_ref[0:CONV_HALO, :] = jnp.where(si == 0, 0.0, ext_ref[0:CONV_HALO, :])

    for r0 in range(0, tm, PROJ_SUB):
        rows = slice(r0, r0 + PROJ_SUB)
        hb = _mod_norm(x_ref[0, rows, :], g_ref[...], mod[0:1], mod[1:2]).astype(BF16)

        pre = lax.dot_general(wif_ref[...], hb, nt, preferred_element_type=F32)[0:2 * nh] + bif_ref[...]
        zf = pre[nh:2 * nh]
        log_f = jnp.minimum(zf, 0.0) - jnp.log1p(jnp.exp(-jnp.abs(zf)))
        b = _chunk_scan(log_f, jnp.add, 0.0)
        c = pre[0:nh] - b
        gate_out[0, :, rows] = jnp.concatenate(
            [c, b, _chunk_scan(c, jnp.maximum, -jnp.inf), jnp.zeros_like(c)], axis=0)

        cos, sina, sinb = cos_ref[rows, :], sina_ref[rows, :], sinb_ref[rows, :]

        def conv_chunk(ci):
            cols = slice(ci * CONV_COLS, (ci + 1) * CONV_COLS)
            u = jnp.dot(hb, wmqk_ref[:, cols], preferred_element_type=F32)
            ext_ref[CONV_HALO + r0:CONV_HALO + r0 + PROJ_SUB, cols] = u
            acc = cb_ref[:, cols]
            for j in range(CONV_K - 1):
                off = CONV_HALO - (CONV_K - 1) + j + r0
                acc = acc + ext_ref[off:off + PROJ_SUB, cols] * cw[j:j + 1, cols]
            acc = acc + u * cw[CONV_K - 1:CONV_K, cols]
            act = acc * _sigmoid(acc)
            if ci < ML_WIDTH // CONV_COLS:
                mq_out[0, rows, cols] = (act * (ML_DIM ** -0.5)).astype(BF16)
            else:
                mk_out[0, rows, ci * CONV_COLS - ML_WIDTH:(ci + 1) * CONV_COLS - ML_WIDTH] = act.astype(BF16)

        chunks = iter(range(2 * ML_WIDTH // CONV_COLS))
        per_slot = 2 * ML_WIDTH // CONV_COLS // 4
        uq = jnp.dot(hb, wq_ref[...], preferred_element_type=F32)
        for _ in range(per_slot):
            conv_chunk(next(chunks))
        q_out[0, rows, :] = _group_norm_rope(uq, gq_ref[...], bd, cos, sina, sinb).astype(BF16)
        uk = jnp.dot(hb, wk_ref[...], preferred_element_type=F32)
        for _ in range(per_slot):
            conv_chunk(next(chunks))
        k_out[0, rows, :] = _group_norm_rope(uk, gk_ref[...], bd, cos, sina, sinb).astype(BF16)
        uv = jnp.dot(hb, wv_ref[...], preferred_element_type=F32)
        umv_t = lax.dot_general(wmv_ref[...], hb, nt, preferred_element_type=F32)
        for _ in range(per_slot):
            conv_chunk(next(chunks))
        v_out[0, rows, :] = uv.astype(BF16)
        mv_out[0, :, rows] = umv_t.astype(BF16)
        umo = jnp.dot(hb, wmo_ref[...], preferred_element_type=F32)
        for _ in range(per_slot):
            conv_chunk(next(chunks))
        mo_out[0, rows, :] = _sigmoid(umo).astype(BF16)

    ext_ref[0:CONV_HALO, :] = ext_ref[tm:tm + CONV_HALO, :]


def _inproj(x, mod3, g, wq, wk, wv, wmqk, wmv, wmo, wif, gq, gk, cos, sina, sinb, bd, cw, cb, bif):
    bsz, s, d = x.shape
    tm = PROJ_TM
    tok = lambda width, dt: jax.ShapeDtypeStruct((bsz, s, width), dt)
    tok_spec = lambda width: pl.BlockSpec((1, tm, width), lambda b, i: (b, i, 0))
    tab_spec = pl.BlockSpec((tm, LANES), lambda b, i: (i, 0))
    return pl.pallas_call(
        _inproj_kernel,
        grid=(bsz, s // tm),
        in_specs=[tok_spec(d),
                  pl.BlockSpec((1, 3, d), lambda b, i: (b, 0, 0)),
                  _const_spec((1, d)),
                  _const_spec(wq.shape), _const_spec(wk.shape), _const_spec(wv.shape),
                  _const_spec(wmqk.shape), _const_spec(wmv.shape), _const_spec(wmo.shape),
                  _const_spec(wif.shape),
                  _const_spec(gq.shape), _const_spec(gk.shape),
                  tab_spec, tab_spec, tab_spec,
                  _const_spec(bd.shape), _const_spec(cw.shape), _const_spec(cb.shape),
                  _const_spec(bif.shape)],
        out_specs=[tok_spec(DA_WIDTH), tok_spec(DA_WIDTH), tok_spec(DA_WIDTH),
                   tok_spec(ML_WIDTH), tok_spec(ML_WIDTH),
                   pl.BlockSpec((1, ML_WIDTH, tm), lambda b, i: (b, 0, i)),
                   tok_spec(ML_WIDTH),
                   pl.BlockSpec((1, GATE_ROWS, tm), lambda b, i: (b, 0, i))],
        out_shape=[tok(DA_WIDTH, BF16), tok(DA_WIDTH, BF16), tok(DA_WIDTH, BF16),
                   tok(ML_WIDTH, BF16), tok(ML_WIDTH, BF16),
                   jax.ShapeDtypeStruct((bsz, ML_WIDTH, s), BF16),
                   tok(ML_WIDTH, BF16),
                   jax.ShapeDtypeStruct((bsz, GATE_ROWS, s), F32)],
        scratch_shapes=[pltpu.VMEM((tm + CONV_HALO, 2 * ML_WIDTH), F32)],
        compiler_params=pltpu.CompilerParams(
            dimension_semantics=("arbitrary", "arbitrary"), vmem_limit_bytes=VMEM_LIMIT),
        name="in_proj",
    )(x, mod3, g.reshape(1, d), wq, wk, wv, wmqk, wmv, wmo, wif, gq, gk, cos, sina, sinb, bd, cw, cb, bif)


def _attn_kernel(bounded_ref, q_ref, k_ref, v_ref, lam_ref, gout_ref, o_ref, qs_ref, m_ref, acc_ref):
    t = ATT_T
    qi = pl.program_id(2)

    u = t // 2
    lane = lax.broadcasted_iota(jnp.int32, (u, DA_V_DIM), 1)
    for half in range(2):
        for part in range(2):
            q = q_ref[0, half * t + part * u:half * t + (part + 1) * u, :]
            zero = jnp.zeros_like(q)
            qs_ref[half, (2 * part) * u:(2 * part + 1) * u, :] = jnp.where(lane < DA_QK_DIM, q, zero)
            qs_ref[half, (2 * part + 1) * u:(2 * part + 2) * u, :] = jnp.where(lane >= DA_QK_DIM, q, zero)

    ones_cols = jnp.ones((t, DA_V_DIM), BF16)

    def keys(kb):
        start = pl.multiple_of(kb * t, t)
        return k_ref[0, pl.ds(start, t), :], jnp.concatenate([v_ref[0, pl.ds(start, t), :], ones_cols], axis=1)

    def piece(half, rows, ks, vs, causal_shift, first, stabilised):
        s = lax.dot_general(qs_ref[half, rows, :], ks, (((1,), (1,)), ((), ())), preferred_element_type=F32)
        if causal_shift is not None:
            row = lax.broadcasted_iota(jnp.int32, s.shape, 0) & (u - 1)
            col = lax.broadcasted_iota(jnp.int32, s.shape, 1)
            s = jnp.where(col <= row + causal_shift, s, -jnp.inf)
        if stabilised:
            m_new = jnp.max(s, axis=1, keepdims=True)
            if not first:
                m_prev = m_ref[half, rows, :]
                m_new = jnp.maximum(m_prev, m_new)
                alpha = jnp.exp2(m_prev - m_new)
            m_ref[half, rows, :] = m_new
            s = s - m_new
        pv = jnp.dot(jnp.exp2(s).astype(BF16), vs, preferred_element_type=F32)
        if first:
            acc_ref[half, rows, :] = pv
        elif stabilised:
            acc_ref[half, rows, :] = alpha * acc_ref[half, rows, :] + pv
        else:
            acc_ref[half, rows, :] += pv

    def block(half, kv, diagonal, first, stabilised):
        ks, vs = kv
        if diagonal:
            piece(half, slice(0, 2 * u), ks[0:u], vs[0:u], 0, first, stabilised)
            piece(half, slice(2 * u, 4 * u), ks, vs, u, first, stabilised)
        else:
            piece(half, slice(0, 4 * u), ks, vs, None, first, stabilised)

    def run(stabilised):
        kv = keys(2 * qi)
        block(0, kv, True, True, stabilised)
        block(1, kv, False, True, stabilised)
        block(1, keys(2 * qi + 1), True, False, stabilised)

        def body(pair, carry):
            for j in range(2):
                kv = keys(2 * pair + j)
                block(0, kv, False, False, stabilised)
                block(1, kv, False, False, stabilised)
            return carry

        lax.fori_loop(0, qi, body, 0)

    @pl.when(bounded_ref[0] != 0)
    def _():
        run(False)

    @pl.when(bounded_ref[0] == 0)
    def _():
        run(True)

    lv = lam_ref[...]
    lam = (jnp.exp(jnp.sum(lv[0:1] * lv[1:2], axis=1, keepdims=True))
           - jnp.exp(jnp.sum(lv[2:3] * lv[3:4], axis=1, keepdims=True)) + LAMBDA_INIT)
    for half in range(2):
        for part in range(2):
            rows = slice(2 * part * u, (2 * part + 2) * u)
            o = acc_ref[half, rows, 0:DA_V_DIM] / acc_ref[half, rows, DA_V_DIM:2 * DA_V_DIM]
            od = o[0:u] - lam * o[u:2 * u]
            ms = jnp.mean(od * od, axis=-1, keepdims=True)
            o_ref[0, half * t + part * u:half * t + (part + 1) * u, :] = (
                (od * lax.rsqrt(ms + EPS)) * gout_ref[...] * (1.0 - LAMBDA_INIT)).astype(BF16)


def _attention(bounded, q, k, v, lam_vecs, g_out):
    bsz, s, _ = q.shape
    t = ATT_T
    blk = pl.BlockSpec((1, 2 * t, DA_V_DIM), lambda b, h, i: (b, i, h))
    full = pl.BlockSpec((1, s, DA_V_DIM), lambda b, h, i: (b, 0, h))
    return pl.pallas_call(
        _attn_kernel,
        grid=(bsz, DA_HEADS, s // (2 * t)),
        in_specs=[pl.BlockSpec(memory_space=pltpu.SMEM),
                  blk, full, full, _const_spec(lam_vecs.shape), _const_spec((1, DA_V_DIM))],
        out_specs=blk,
        out_shape=jax.ShapeDtypeStruct((bsz, s, DA_WIDTH), BF16),
        scratch_shapes=[pltpu.VMEM((2, 2 * t, DA_V_DIM), BF16),
                        pltpu.VMEM((2, 2 * t, 1), F32),
                        pltpu.VMEM((2, 2 * t, 2 * DA_V_DIM), F32)],
        compiler_params=pltpu.CompilerParams(
            dimension_semantics=("arbitrary", "arbitrary", "arbitrary"), vmem_limit_bytes=VMEM_LIMIT),
        name="diff_attention",
    )(bounded, q, k, v, lam_vecs, g_out.reshape(1, DA_V_DIM))


def _mlstm_kernel(q_ref, k_ref, vt_ref, o_ref, gr_ref, gout_ref, y_ref, ct_ref, m_ref):
    ln = ML_CHUNK
    nh = ML_HEADS
    nt = (((1,), (1,)), ((), ()))

    @pl.when(pl.program_id(1) == 0)
    def _():
        ct_ref[...] = jnp.zeros(ct_ref.shape, F32)
        m_ref[...] = jnp.zeros(m_ref.shape, F32)

    row = lax.broadcasted_iota(jnp.int32, (ln, ln), 0)
    col = lax.broadcasted_iota(jnp.int32, (ln, ln), 1)
    lower = col <= row
    ones_row = (lax.broadcasted_iota(jnp.int32, (ML_STATE_ROWS - ML_DIM, ln), 0) == 0).astype(BF16)

    for r in range(ML_ROWS):
        gr = gr_ref[r]
        c, b = gr[0:nh], gr[nh:2 * nh]
        m_prev = jnp.concatenate([m_ref[r, 0:nh, :]] * (ln // LANES), axis=1)
        m_run = jnp.maximum(gr[2 * nh:3 * nh], m_prev)
        m_last = m_run[:, ln - 1:ln]
        e_inter = jnp.exp(m_prev - m_run)
        e_floor = jnp.exp(-(b + m_run))
        w_in = jnp.exp(c - m_last)
        decay = e_inter[:, ln - 1:ln]
        m_ref[r, 0:nh, :] = jnp.broadcast_to(b[:, ln - 1:ln] + m_last, (nh, LANES))
        m_run_cols = jnp.concatenate([m_run, m_run], axis=0).T

        for h in range(nh):
            lanes = slice(h * ML_DIM, (h + 1) * ML_DIM)
            q = q_ref[r, :, lanes]
            k = k_ref[r, :, lanes]
            vt_aug = jnp.concatenate([vt_ref[r, lanes, :], ones_row], axis=0)
            s = lax.dot_general(q, k, nt, preferred_element_type=F32)
            gate = jnp.exp(jnp.where(lower, c[h:h + 1, :] - m_run_cols[:, h:h + 1], -jnp.inf))
            w = (s * gate).astype(BF16)
            state_t = ct_ref[r, h]
            full_t = (e_inter[h:h + 1, :]
                      * lax.dot_general(state_t.astype(BF16), q, nt, preferred_element_type=F32)
                      + lax.dot_general(vt_aug, w, nt, preferred_element_type=F32))
            den = full_t[ML_DIM:ML_DIM + 1, :]
            hh_t = full_t[0:ML_DIM, :] / jnp.maximum(jnp.abs(den), e_floor[h:h + 1, :])
            hn_t = hh_t * lax.rsqrt(jnp.mean(hh_t * hh_t, axis=0, keepdims=True) + EPS)
            y_ref[r, :, lanes] = (o_ref[r, :, lanes].astype(F32) * (hn_t.T * gout_ref[:, lanes])).astype(BF16)

            vw = (vt_aug.astype(F32) * w_in[h:h + 1, :]).astype(BF16)
            ct_ref[r, h] = decay[h:h + 1, :] * state_t + jnp.dot(vw, k, preferred_element_type=F32)


def _mlstm(q, k, vt, o, gates_row, g_out):
    bsz, s, _ = q.shape
    ln = ML_CHUNK
    rows = ML_ROWS
    tok = pl.BlockSpec((rows, ln, ML_WIDTH), lambda b, c: (b, c, 0))
    return pl.pallas_call(
        _mlstm_kernel,
        grid=(bsz // rows, s // ln),
        in_specs=[tok, tok, pl.BlockSpec((rows, ML_WIDTH, ln), lambda b, c: (b, 0, c)), tok,
                  pl.BlockSpec((rows, GATE_ROWS, ln), lambda b, c: (b, 0, c)),
                  _const_spec((1, ML_WIDTH))],
        out_specs=tok,
        out_shape=jax.ShapeDtypeStruct((bsz, s, ML_WIDTH), BF16),
        scratch_shapes=[pltpu.VMEM((rows, ML_HEADS, ML_STATE_ROWS, ML_DIM), F32),
                        pltpu.VMEM((rows, 8, LANES), F32)],
        compiler_params=pltpu.CompilerParams(
            dimension_semantics=("arbitrary", "arbitrary"), vmem_limit_bytes=VMEM_LIMIT),
        name="mlstm",
    )(q, k, vt, o, gates_row, g_out.reshape(1, ML_WIDTH))


def _rope_tables(s):
    half = ROPE_DIM // 2
    pos = jnp.arange(s, dtype=F32)
    inv_freq = ROPE_THETA ** (-jnp.arange(0, ROPE_DIM, 2, dtype=F32) / ROPE_DIM)
    ang = pos[:, None] * inv_freq[None, :]
    cos, sin = jnp.cos(ang), jnp.sin(ang)
    rest = DA_QK_DIM - ROPE_DIM
    cos_g = jnp.concatenate([cos, cos, jnp.ones((s, rest), F32)], axis=1)
    sina_g = jnp.concatenate([-sin, jnp.zeros((s, DA_QK_DIM - half), F32)], axis=1)
    sinb_g = jnp.concatenate([jnp.zeros((s, half), F32), sin, jnp.zeros((s, rest), F32)], axis=1)
    rep = LANES // DA_QK_DIM
    return jnp.tile(cos_g, (1, rep)), jnp.tile(sina_g, (1, rep)), jnp.tile(sinb_g, (1, rep))


def kernel(x, c, w_ada, b_ada, g_norm, ffn1_w12, ffn1_w3, w_in, conv_w, conv_b, b_igate, b_fgate,
           g_qnorm, g_knorm, lambda_qk, g_da_out, g_ml_out, w_out, ffn2_w12, ffn2_w3):
    bsz, s, d = x.shape
    l = 0
    mod = _ada(c, w_ada[l], b_ada[l]).reshape(bsz, 3, 3, d)

    def ffn(xin, sub, w12, w3, mix=None):
        w12b = w12.astype(BF16)
        return _ffn(xin, mod[:, sub], g_norm[l, sub], w12b[:, :D_FF], w12b[:, D_FF:], w3.astype(BF16), mix)

    x = ffn(x, 0, ffn1_w12[l], ffn1_w3[l])

    wb = w_in[l].astype(BF16)
    o0 = 0
    parts = []
    for width in (DA_WIDTH, DA_WIDTH, DA_WIDTH, 2 * ML_WIDTH, ML_WIDTH, ML_WIDTH, 2 * ML_HEADS):
        parts.append(wb[:, o0:o0 + width])
        o0 += width
    wq, wk, wv, wmqk, wmv, wmo, wif = parts
    wif = jnp.pad(wif.T, ((0, 2 * ML_HEADS), (0, 0)))
    wmv = wmv.T
    groups = DA_WIDTH // DA_QK_DIM
    q_gain = g_qnorm[l] * (DA_QK_DIM ** -0.5)
    gq = jnp.tile(q_gain * LOG2E, groups).reshape(1, DA_WIDTH)
    score_bound = 1.05 * DA_QK_DIM * jnp.max(jnp.abs(q_gain)) * jnp.max(jnp.abs(g_knorm[l]))
    bounded = (score_bound <= SAFE_SCORE_BOUND).astype(jnp.int32).reshape(1)
    gk = jnp.tile(g_knorm[l], groups).reshape(1, DA_WIDTH)
    cos, sina, sinb = _rope_tables(s)
    gid = jnp.arange(MXU_TILE) // DA_QK_DIM
    bd = (gid[:, None] == gid[None, :]).astype(BF16)
    bif = jnp.concatenate([b_igate[l], b_fgate[l]]).reshape(2 * ML_HEADS, 1)
    da_q, da_k, da_v, ml_q, ml_k, ml_v, ml_o, gates = _inproj(
        x, mod[:, 1], g_norm[l, 1], wq, wk, wv, wmqk, wmv, wmo, wif, gq, gk, cos, sina, sinb, bd,
        conv_w[l], conv_b[l].reshape(1, 2 * ML_WIDTH), bif)

    y_da = _attention(bounded, da_q, da_k, da_v, lambda_qk[l], g_da_out[l])
    y_ml = _mlstm(ml_q, ml_k, ml_v, ml_o, gates, g_ml_out[l])

    wo = w_out[l].astype(BF16)
    return ffn(x, 2, ffn2_w12[l], ffn2_w3[l], mix=(mod[:, 1], y_da, y_ml, wo[:DA_WIDTH], wo[DA_WIDTH:]))
```

```python
import functools

import jax
import jax.numpy as jnp
from jax import lax
from jax.experimental import pallas as pl
from jax.experimental.pallas import tpu as pltpu

F32 = jnp.float32
BF16 = jnp.bfloat16

D_MODEL = 1024
DA_HEADS = 4
DA_QK_DIM = 64
DA_V_DIM = 2 * DA_QK_DIM
DA_WIDTH = DA_HEADS * DA_V_DIM
ML_HEADS = 4
ML_DIM = 128
ML_WIDTH = ML_HEADS * ML_DIM
ROPE_THETA = 500000.0
ROPE_DIM = DA_QK_DIM // 4
D_FF = 2816
CONV_K = 4
EPS = 1e-6
LAMBDA_INIT = 0.8 - 0.6 * 1.0
LOG2E = 1.4426950408889634

LANES = 128
MXU_TILE = 256
VMEM_LIMIT = 56 * 1024 * 1024

FFN_TM = 1024
FFN_SUB = 256
PROJ_TM = 1024
PROJ_SUB = 256
CONV_COLS = 256
ATT_T = 512
SAFE_SCORE_BOUND = 40.0
ML_CHUNK = 256
ML_ROWS = 4
ML_STATE_ROWS = ML_DIM + 16
GATE_ROWS = 16
CONV_HALO = 8


def _const_spec(shape):
    nd = len(shape)
    return pl.BlockSpec(shape, lambda *_: (0,) * nd, pipeline_mode=pl.Buffered(1))


def _sigmoid(x):
    return 1.0 / (1.0 + jnp.exp(-x))


def _mod_norm(x, g, shift, scale):
    ms = jnp.mean(x * x, axis=-1, keepdims=True)
    return (x * lax.rsqrt(ms + EPS)) * (g * (1.0 + scale)) + shift


def _ada_kernel(c_ref, w_ref, b_ref, o_ref):
    c = c_ref[...]
    cs = (c * _sigmoid(c)).astype(BF16)
    o_ref[...] = jnp.dot(cs, w_ref[...].astype(BF16), preferred_element_type=F32) + b_ref[...]


def _ada(c, w_ada, b_ada):
    bsz, d = c.shape
    n = w_ada.shape[1]
    tn = 1024
    return pl.pallas_call(
        _ada_kernel,
        grid=(n // tn,),
        in_specs=[pl.BlockSpec((bsz, d), lambda j: (0, 0)),
                  pl.BlockSpec((d, tn), lambda j: (0, j)),
                  pl.BlockSpec((1, tn), lambda j: (0, j))],
        out_specs=pl.BlockSpec((bsz, tn), lambda j: (0, j)),
        out_shape=jax.ShapeDtypeStruct((bsz, n), F32),
        compiler_params=pltpu.CompilerParams(dimension_semantics=("arbitrary",)),
        name="adaln_mod",
    )(c, w_ada, b_ada.reshape(1, n))


def _ffn_body(x, mod_ref, g_ref, w12_ref, w3_ref, o_ref, act_ref):
    mod = mod_ref[0]
    hb = _mod_norm(x, g_ref[...], mod[0:1], mod[1:2]).astype(BF16)
    for c in range(D_FF // FFN_SUB):
        cols = slice(c * FFN_SUB, (c + 1) * FFN_SUB)
        a = jnp.dot(hb, w12_ref[:, cols], preferred_element_type=F32)
        b = jnp.dot(hb, w12_ref[:, D_FF + c * FFN_SUB:D_FF + (c + 1) * FFN_SUB], preferred_element_type=F32)
        act_ref[:, cols] = (a * _sigmoid(a) * b).astype(BF16)
    y = jnp.dot(act_ref[...], w3_ref[...], preferred_element_type=F32)
    o_ref[0] = x + (0.5 * (1.0 + mod[2:3])) * y


def _ffn_kernel(x_ref, mod_ref, g_ref, w12_ref, w3_ref, o_ref, act_ref):
    _ffn_body(x_ref[0], mod_ref, g_ref, w12_ref, w3_ref, o_ref, act_ref)


def _mix_ffn_kernel(x_ref, mmod_ref, ya_ref, ym_ref, wa_ref, wm_ref,
                    mod_ref, g_ref, w12_ref, w3_ref, o_ref, act_ref):
    y = (jnp.dot(ya_ref[0], wa_ref[...], preferred_element_type=F32)
         + jnp.dot(ym_ref[0], wm_ref[...], preferred_element_type=F32))
    x = x_ref[0] + (1.0 + mmod_ref[0][2:3]) * y
    _ffn_body(x, mod_ref, g_ref, w12_ref, w3_ref, o_ref, act_ref)


def _ffn(x, mod3, g, w12, w3, mix=None):
    bsz, s, d = x.shape
    tm = FFN_TM
    tok_spec = lambda width: pl.BlockSpec((1, tm, width), lambda b, i: (b, i, 0))
    mod_spec = pl.BlockSpec((1, 3, d), lambda b, i: (b, 0, 0))
    ffn_specs = [mod_spec, _const_spec((1, d)),
                 _const_spec(w12.shape), _const_spec(w3.shape)]
    ffn_args = (mod3, g.reshape(1, d), w12, w3)
    if mix is None:
        body, in_specs, args = _ffn_kernel, [tok_spec(d)] + ffn_specs, (x,) + ffn_args
    else:
        mmod3, y_da, y_ml, w_a, w_m = mix
        body = _mix_ffn_kernel
        in_specs = [tok_spec(d), mod_spec, tok_spec(DA_WIDTH), tok_spec(ML_WIDTH),
                    _const_spec(w_a.shape), _const_spec(w_m.shape)] + ffn_specs
        args = (x, mmod3, y_da, y_ml, w_a, w_m) + ffn_args
    return pl.pallas_call(
        body,
        grid=(bsz, s // tm),
        in_specs=in_specs,
        out_specs=tok_spec(d),
        out_shape=jax.ShapeDtypeStruct(x.shape, F32),
        scratch_shapes=[pltpu.VMEM((tm, D_FF), BF16)],
        compiler_params=pltpu.CompilerParams(
            dimension_semantics=("arbitrary", "arbitrary"), vmem_limit_bytes=VMEM_LIMIT),
        name="ffn" if mix is None else "mix_ffn",
    )(*args)


def _group_norm_rope(u, gvec, bd, cos, sina, sinb):
    x2 = u * u
    ssq = jnp.concatenate(
        [jnp.dot(x2[:, j:j + MXU_TILE].astype(BF16), bd, preferred_element_type=F32)
         for j in range(0, u.shape[1], MXU_TILE)], axis=1)
    xn = (u * lax.rsqrt(ssq * (1.0 / DA_QK_DIM) + EPS)) * gvec
    outs = []
    for h in range(u.shape[1] // LANES):
        xh = xn[:, h * LANES:(h + 1) * LANES]
        up = pltpu.roll(xh, LANES - ROPE_DIM // 2, 1)
        dn = pltpu.roll(xh, ROPE_DIM // 2, 1)
        outs.append(xh * cos + up * sina + dn * sinb)
    return jnp.concatenate(outs, axis=1)


def _chunk_scan(x, op, identity):
    pos = lax.broadcasted_iota(jnp.int32, x.shape, 1) & (ML_CHUNK - 1)
    d = 1
    while d < ML_CHUNK:
        x = op(x, jnp.where(pos >= d, pltpu.roll(x, d, 1), identity))
        d *= 2
    return x


def _inproj_kernel(x_ref, mod_ref, g_ref, wq_ref, wk_ref, wv_ref, wmqk_ref, wmv_ref, wmo_ref, wif_ref,
                   gq_ref, gk_ref, rope_ref, bd_ref, cw_ref, cb_ref, bif_ref,
                   q_out, k_out, v_out, mq_out, mk_out, mv_out, mo_out, gate_out, ext_ref):
    tm = x_ref.shape[1]
    si = pl.program_id(1)
    mod = mod_ref[0]
    nh = ML_HEADS
    nt = (((1,), (1,)), ((), ()))
    bd = bd_ref[...]
    cw = cw_ref[...]

    ext_ref[0:CONV_HALO, :] = jnp.where(si == 0, 0.0, ext_ref[0:CONV_HALO, :])

    for r0 in range(0, tm, PROJ_SUB):
        rows = slice(r0, r0 + PROJ_SUB)
        hb = _mod_norm(x_ref[0, rows, :], g_ref[...], mod[0:1], mod[1:2]).astype(BF16)

        pre = lax.dot_general(wif_ref[...], hb, nt, preferred_element_type=F32)[0:2 * nh] + bif_ref[...]
        zf = pre[nh:2 * nh]
        log_f = jnp.minimum(zf, 0.0) - jnp.log1p(jnp.exp(-jnp.abs(zf)))
        b = _chunk_scan(log_f, jnp.add, 0.0)
        c = pre[0:nh] - b
        gate_out[0, :, rows] = jnp.concatenate(
            [c, b, _chunk_scan(c, jnp.maximum, -jnp.inf), jnp.zeros_like(c)], axis=0)

        cos, sina, sinb = (rope_ref[rows, j * LANES:(j + 1) * LANES] for j in range(3))

        def conv_chunk(ci):
            cols = slice(ci * CONV_COLS, (ci + 1) * CONV_COLS)
            u = jnp.dot(hb, wmqk_ref[:, cols], preferred_element_type=F32)
            ext_ref[CONV_HALO + r0:CONV_HALO + r0 + PROJ_SUB, cols] = u
            acc = cb_ref[:, cols]
            for j in range(CONV_K - 1):
                off = CONV_HALO - (CONV_K - 1) + j + r0
                acc = acc + ext_ref[off:off + PROJ_SUB, cols] * cw[j:j + 1, cols]
            acc = acc + u * cw[CONV_K - 1:CONV_K, cols]
            act = acc * _sigmoid(acc)
            if ci < ML_WIDTH // CONV_COLS:
                mq_out[0, rows, cols] = (act * (ML_DIM ** -0.5)).astype(BF16)
            else:
                mk_out[0, rows, ci * CONV_COLS - ML_WIDTH:(ci + 1) * CONV_COLS - ML_WIDTH] = act.astype(BF16)

        chunks = iter(range(2 * ML_WIDTH // CONV_COLS))
        per_slot = 2 * ML_WIDTH // CONV_COLS // 4
        uq = jnp.dot(hb, wq_ref[...], preferred_element_type=F32)
        for _ in range(per_slot):
            conv_chunk(next(chunks))
        q_out[0, rows, :] = _group_norm_rope(uq, gq_ref[...], bd, cos, sina, sinb).astype(BF16)
        uk = jnp.dot(hb, wk_ref[...], preferred_element_type=F32)
        for _ in range(per_slot):
            conv_chunk(next(chunks))
        k_out[0, rows, :] = _group_norm_rope(uk, gk_ref[...], bd, cos, sina, sinb).astype(BF16)
        uv = jnp.dot(hb, wv_ref[...], preferred_element_type=F32)
        umv_t = lax.dot_general(wmv_ref[...], hb, nt, preferred_element_type=F32)
        for _ in range(per_slot):
            conv_chunk(next(chunks))
        v_out[0, rows, :] = uv.astype(BF16)
        mv_out[0, :, rows] = umv_t.astype(BF16)
        umo = jnp.dot(hb, wmo_ref[...], preferred_element_type=F32)
        for _ in range(per_slot):
            conv_chunk(next(chunks))
        mo_out[0, rows, :] = _sigmoid(umo).astype(BF16)

    ext_ref[0:CONV_HALO, :] = ext_ref[tm:tm + CONV_HALO, :]


def _inproj(x, mod3, g, wq, wk, wv, wmqk, wmv, wmo, wif, gq, gk, rope, bd, cw, cb, bif):
    bsz, s, d = x.shape
    tm = PROJ_TM
    tok = lambda width, dt: jax.ShapeDtypeStruct((bsz, s, width), dt)
    tok_spec = lambda width: pl.BlockSpec((1, tm, width), lambda b, i: (b, i, 0))
    return pl.pallas_call(
        _inproj_kernel,
        grid=(bsz, s // tm),
        in_specs=[tok_spec(d),
                  pl.BlockSpec((1, 3, d), lambda b, i: (b, 0, 0)),
                  _const_spec((1, d)),
                  _const_spec(wq.shape), _const_spec(wk.shape), _const_spec(wv.shape),
                  _const_spec(wmqk.shape), _const_spec(wmv.shape), _const_spec(wmo.shape),
                  _const_spec(wif.shape),
                  _const_spec(gq.shape), _const_spec(gk.shape),
                  pl.BlockSpec((tm, 3 * LANES), lambda b, i: (i, 0)),
                  _const_spec(bd.shape), _const_spec(cw.shape), _const_spec(cb.shape),
                  _const_spec(bif.shape)],
        out_specs=[tok_spec(DA_WIDTH), tok_spec(DA_WIDTH), tok_spec(DA_WIDTH),
                   tok_spec(ML_WIDTH), tok_spec(ML_WIDTH),
                   pl.BlockSpec((1, ML_WIDTH, tm), lambda b, i: (b, 0, i)),
                   tok_spec(ML_WIDTH),
                   pl.BlockSpec((1, GATE_ROWS, tm), lambda b, i: (b, 0, i))],
        out_shape=[tok(DA_WIDTH, BF16), tok(DA_WIDTH, BF16), tok(DA_WIDTH, BF16),
                   tok(ML_WIDTH, BF16), tok(ML_WIDTH, BF16),
                   jax.ShapeDtypeStruct((bsz, ML_WIDTH, s), BF16),
                   tok(ML_WIDTH, BF16),
                   jax.ShapeDtypeStruct((bsz, GATE_ROWS, s), F32)],
        scratch_shapes=[pltpu.VMEM((tm + CONV_HALO, 2 * ML_WIDTH), F32)],
        compiler_params=pltpu.CompilerParams(
            dimension_semantics=("arbitrary", "arbitrary"), vmem_limit_bytes=VMEM_LIMIT),
        name="in_proj",
    )(x, mod3, g.reshape(1, d), wq, wk, wv, wmqk, wmv, wmo, wif, gq, gk, rope, bd, cw, cb, bif)


def _attn_kernel(bounded_ref, q_ref, k_ref, v_ref, lam_ref, gout_ref, o_ref, qs_ref, m_ref, acc_ref):
    t = ATT_T
    u = t // 2
    lane = lax.broadcasted_iota(jnp.int32, (u, DA_V_DIM), 1)
    ones_cols = jnp.ones((t, DA_V_DIM), BF16)
    lv = lam_ref[...]
    lam = (jnp.exp(jnp.sum(lv[0:1] * lv[1:2], axis=1, keepdims=True))
           - jnp.exp(jnp.sum(lv[2:3] * lv[3:4], axis=1, keepdims=True)) + LAMBDA_INIT)

    def query_rows(qi, half, part):
        return pl.ds(pl.multiple_of(qi * (2 * t) + half * t + part * u, u), u)

    def stack_queries(qi):
        for half in range(2):
            for part in range(2):
                q = q_ref[0, query_rows(qi, half, part), :]
                zero = jnp.zeros_like(q)
                qs_ref[half, (2 * part) * u:(2 * part + 1) * u, :] = jnp.where(lane < DA_QK_DIM, q, zero)
                qs_ref[half, (2 * part + 1) * u:(2 * part + 2) * u, :] = jnp.where(lane >= DA_QK_DIM, q, zero)

    def keys(kb):
        start = pl.multiple_of(kb * t, t)
        return k_ref[0, pl.ds(start, t), :], jnp.concatenate([v_ref[0, pl.ds(start, t), :], ones_cols], axis=1)

    def piece(half, rows, ks, vs, causal_shift, first, stabilised):
        s = lax.dot_general(qs_ref[half, rows, :], ks, (((1,), (1,)), ((), ())), preferred_element_type=F32)
        if causal_shift is not None:
            row = lax.broadcasted_iota(jnp.int32, s.shape, 0) & (u - 1)
            col = lax.broadcasted_iota(jnp.int32, s.shape, 1)
            s = jnp.where(col <= row + causal_shift, s, -jnp.inf)
        if stabilised:
            m_new = jnp.max(s, axis=1, keepdims=True)
            if not first:
                m_prev = m_ref[half, rows, :]
                m_new = jnp.maximum(m_prev, m_new)
                alpha = jnp.exp2(m_prev - m_new)
            m_ref[half, rows, :] = m_new
            s = s - m_new
        pv = jnp.dot(jnp.exp2(s).astype(BF16), vs, preferred_element_type=F32)
        if first:
            acc_ref[half, rows, :] = pv
        elif stabilised:
            acc_ref[half, rows, :] = alpha * acc_ref[half, rows, :] + pv
        else:
            acc_ref[half, rows, :] += pv

    def block(half, kv, diagonal, first, stabilised):
        ks, vs = kv
        if diagonal:
            piece(half, slice(0, 2 * u), ks[0:u], vs[0:u], 0, first, stabilised)
            piece(half, slice(2 * u, 4 * u), ks, vs, u, first, stabilised)
        else:
            piece(half, slice(0, 4 * u), ks, vs, None, first, stabilised)

    def tile_pair(qi, stabilised):
        stack_queries(qi)
        kv = keys(2 * qi)
        block(0, kv, True, True, stabilised)
        block(1, kv, False, True, stabilised)
        block(1, keys(2 * qi + 1), True, False, stabilised)

        def body(pair, carry):
            for j in range(2):
                kv = keys(2 * pair + j)
                block(0, kv, False, False, stabilised)
                block(1, kv, False, False, stabilised)
            return carry

        lax.fori_loop(0, qi, body, 0)

        for half in range(2):
            for part in range(2):
                rows = slice(2 * part * u, (2 * part + 2) * u)
                o = acc_ref[half, rows, 0:DA_V_DIM] / acc_ref[half, rows, DA_V_DIM:2 * DA_V_DIM]
                od = o[0:u] - lam * o[u:2 * u]
                ms = jnp.mean(od * od, axis=-1, keepdims=True)
                o_ref[0, query_rows(qi, half, part), :] = (
                    (od * lax.rsqrt(ms + EPS)) * gout_ref[...] * (1.0 - LAMBDA_INIT)).astype(BF16)

    def run(stabilised):
        def body(qi, carry):
            tile_pair(qi, stabilised)
            return carry

        lax.fori_loop(0, q_ref.shape[1] // (2 * t), body, 0)

    @pl.when(bounded_ref[0] != 0)
    def _():
        run(False)

    @pl.when(bounded_ref[0] == 0)
    def _():
        run(True)


def _attention(bounded, q, k, v, lam_vecs, g_out):
    bsz, s, _ = q.shape
    t = ATT_T
    full = pl.BlockSpec((1, s, DA_V_DIM), lambda b, h: (b, 0, h))
    return pl.pallas_call(
        _attn_kernel,
        grid=(bsz, DA_HEADS),
        in_specs=[pl.BlockSpec(memory_space=pltpu.SMEM),
                  full, full, full, _const_spec(lam_vecs.shape), _const_spec((1, DA_V_DIM))],
        out_specs=full,
        out_shape=jax.ShapeDtypeStruct((bsz, s, DA_WIDTH), BF16),
        scratch_shapes=[pltpu.VMEM((2, 2 * t, DA_V_DIM), BF16),
                        pltpu.VMEM((2, 2 * t, 1), F32),
                        pltpu.VMEM((2, 2 * t, 2 * DA_V_DIM), F32)],
        compiler_params=pltpu.CompilerParams(
            dimension_semantics=("arbitrary", "arbitrary"), vmem_limit_bytes=VMEM_LIMIT),
        name="diff_attention",
    )(bounded, q, k, v, lam_vecs, g_out.reshape(1, DA_V_DIM))


def _mlstm_kernel(q_ref, k_ref, vt_ref, o_ref, gr_ref, gout_ref, y_ref, ct_ref, m_ref):
    ln = ML_CHUNK
    nh = ML_HEADS
    nt = (((1,), (1,)), ((), ()))

    @pl.when(pl.program_id(1) == 0)
    def _():
        ct_ref[...] = jnp.zeros(ct_ref.shape, F32)
        m_ref[...] = jnp.zeros(m_ref.shape, F32)

    row = lax.broadcasted_iota(jnp.int32, (ln, ln), 0)
    col = lax.broadcasted_iota(jnp.int32, (ln, ln), 1)
    lower = col <= row
    ones_row = (lax.broadcasted_iota(jnp.int32, (ML_STATE_ROWS - ML_DIM, ln), 0) == 0).astype(BF16)

    for r in range(ML_ROWS):
        gr = gr_ref[r]
        c, b = gr[0:nh], gr[nh:2 * nh]
        m_prev = jnp.concatenate([m_ref[r, 0:nh, :]] * (ln // LANES), axis=1)
        m_run = jnp.maximum(gr[2 * nh:3 * nh], m_prev)
        m_last = m_run[:, ln - 1:ln]
        e_inter = jnp.exp(m_prev - m_run)
        e_floor = jnp.exp(-(b + m_run))
        w_in = jnp.exp(c - m_last)
        decay = e_inter[:, ln - 1:ln]
        m_ref[r, 0:nh, :] = jnp.broadcast_to(b[:, ln - 1:ln] + m_last, (nh, LANES))
        m_run_cols = jnp.concatenate([m_run, m_run], axis=0).T

        for h in range(nh):
            lanes = slice(h * ML_DIM, (h + 1) * ML_DIM)
            q = q_ref[r, :, lanes]
            k = k_ref[r, :, lanes]
            vt_aug = jnp.concatenate([vt_ref[r, lanes, :], ones_row], axis=0)
            s = lax.dot_general(q, k, nt, preferred_element_type=F32)
            gate = jnp.exp(jnp.where(lower, c[h:h + 1, :] - m_run_cols[:, h:h + 1], -jnp.inf))
            w = (s * gate).astype(BF16)
            state_t = ct_ref[r, h]
            full_t = (e_inter[h:h + 1, :]
                      * lax.dot_general(state_t.astype(BF16), q, nt, preferred_element_type=F32)
                      + lax.dot_general(vt_aug, w, nt, preferred_element_type=F32))
            den = full_t[ML_DIM:ML_DIM + 1, :]
            hh_t = full_t[0:ML_DIM, :] / jnp.maximum(jnp.abs(den), e_floor[h:h + 1, :])
            hn_t = hh_t * lax.rsqrt(jnp.mean(hh_t * hh_t, axis=0, keepdims=True) + EPS)
            y_ref[r, :, lanes] = (o_ref[r, :, lanes].astype(F32) * (hn_t.T * gout_ref[:, lanes])).astype(BF16)

            vw = (vt_aug.astype(F32) * w_in[h:h + 1, :]).astype(BF16)
            ct_ref[r, h] = decay[h:h + 1, :] * state_t + jnp.dot(vw, k, preferred_element_type=F32)


def _mlstm(q, k, vt, o, gates_row, g_out):
    bsz, s, _ = q.shape
    ln = ML_CHUNK
    rows = ML_ROWS
    tok = pl.BlockSpec((rows, ln, ML_WIDTH), lambda b, c: (b, c, 0))
    return pl.pallas_call(
        _mlstm_kernel,
        grid=(bsz // rows, s // ln),
        in_specs=[tok, tok, pl.BlockSpec((rows, ML_WIDTH, ln), lambda b, c: (b, 0, c)), tok,
                  pl.BlockSpec((rows, GATE_ROWS, ln), lambda b, c: (b, 0, c)),
                  _const_spec((1, ML_WIDTH))],
        out_specs=tok,
        out_shape=jax.ShapeDtypeStruct((bsz, s, ML_WIDTH), BF16),
        scratch_shapes=[pltpu.VMEM((rows, ML_HEADS, ML_STATE_ROWS, ML_DIM), F32),
                        pltpu.VMEM((rows, 8, LANES), F32)],
        compiler_params=pltpu.CompilerParams(
            dimension_semantics=("arbitrary", "arbitrary"), vmem_limit_bytes=VMEM_LIMIT),
        name="mlstm",
    )(q, k, vt, o, gates_row, g_out.reshape(1, ML_WIDTH))


def _rope_tables(s):
    half = ROPE_DIM // 2
    pos = jnp.arange(s, dtype=F32)
    inv_freq = ROPE_THETA ** (-jnp.arange(0, ROPE_DIM, 2, dtype=F32) / ROPE_DIM)
    ang = pos[:, None] * inv_freq[None, :]
    basis = jnp.concatenate([jnp.cos(ang), jnp.sin(ang), jnp.ones((s, 1), F32)], axis=1)
    g = jnp.arange(LANES) % DA_QK_DIM
    j = jnp.arange(2 * half + 1)[:, None]
    cos_sel = jnp.where(g < ROPE_DIM, j == g % half, j == 2 * half).astype(F32)
    sin_up_sel = -((g < half) & (j == half + g)).astype(F32)
    sin_dn_sel = ((g >= half) & (g < ROPE_DIM) & (j == g)).astype(F32)
    sel = jnp.concatenate([cos_sel, sin_up_sel, sin_dn_sel], axis=1)
    return jnp.dot(basis, sel, precision=lax.Precision.HIGHEST)


def kernel(x, c, w_ada, b_ada, g_norm, ffn1_w12, ffn1_w3, w_in, conv_w, conv_b, b_igate, b_fgate,
           g_qnorm, g_knorm, lambda_qk, g_da_out, g_ml_out, w_out, ffn2_w12, ffn2_w3):
    bsz, s, d = x.shape
    l = 0
    mod = _ada(c, w_ada[l], b_ada[l]).reshape(bsz, 3, 3, d)

    def ffn(xin, sub, w12, w3, mix=None):
        return _ffn(xin, mod[:, sub], g_norm[l, sub], w12.astype(BF16), w3.astype(BF16), mix)

    x = ffn(x, 0, ffn1_w12[l], ffn1_w3[l])

    wb = w_in[l].astype(BF16)
    o0 = 0
    parts = []
    for width in (DA_WIDTH, DA_WIDTH, DA_WIDTH, 2 * ML_WIDTH, ML_WIDTH, ML_WIDTH, 2 * ML_HEADS):
        parts.append(wb[:, o0:o0 + width])
        o0 += width
    wq, wk, wv, wmqk, wmv, wmo, wif = parts
    wif = jnp.pad(wif.T, ((0, 2 * ML_HEADS), (0, 0)))
    wmv = wmv.T
    groups = DA_WIDTH // DA_QK_DIM
    q_gain = g_qnorm[l] * (DA_QK_DIM ** -0.5)
    gq = jnp.tile(q_gain * LOG2E, groups).reshape(1, DA_WIDTH)
    score_bound = 1.05 * DA_QK_DIM * jnp.max(jnp.abs(q_gain)) * jnp.max(jnp.abs(g_knorm[l]))
    bounded = (score_bound <= SAFE_SCORE_BOUND).astype(jnp.int32).reshape(1)
    gk = jnp.tile(g_knorm[l], groups).reshape(1, DA_WIDTH)
    rope = _rope_tables(s)
    gid = jnp.arange(MXU_TILE) // DA_QK_DIM
    bd = (gid[:, None] == gid[None, :]).astype(BF16)
    bif = jnp.concatenate([b_igate[l], b_fgate[l]]).reshape(2 * ML_HEADS, 1)
    da_q, da_k, da_v, ml_q, ml_k, ml_v, ml_o, gates = _inproj(
        x, mod[:, 1], g_norm[l, 1], wq, wk, wv, wmqk, wmv, wmo, wif, gq, gk, rope, bd,
        conv_w[l], conv_b[l].reshape(1, 2 * ML_WIDTH), bif)

    y_da = _attention(bounded, da_q, da_k, da_v, lambda_qk[l], g_da_out[l])
    y_ml = _mlstm(ml_q, ml_k, ml_v, ml_o, gates, g_ml_out[l])

    wo = w_out[l].astype(BF16)
    return ffn(x, 2, ffn2_w12[l], ffn2_w3[l], mix=(mod[:, 1], y_da, y_ml, wo[:DA_WIDTH], wo[DA_WIDTH:]))
```

```python
import functools

import jax
import jax.numpy as jnp
from jax import lax
from jax.experimental import pallas as pl
from jax.experimental.pallas import tpu as pltpu

F32 = jnp.float32
BF16 = jnp.bfloat16

D_MODEL = 1024
DA_HEADS = 4
DA_QK_DIM = 64
DA_V_DIM = 2 * DA_QK_DIM
DA_WIDTH = DA_HEADS * DA_V_DIM
ML_HEADS = 4
ML_DIM = 128
ML_WIDTH = ML_HEADS * ML_DIM
ROPE_THETA = 500000.0
ROPE_DIM = DA_QK_DIM // 4
D_FF = 2816
CONV_K = 4
EPS = 1e-6
LAMBDA_INIT = 0.8 - 0.6 * 1.0
LOG2E = 1.4426950408889634

LANES = 128
MXU_TILE = 256
VMEM_LIMIT = 56 * 1024 * 1024

FFN_TM = 1024
FFN_SUB = 256
PROJ_TM = 1024
PROJ_SUB = ML_CHUNK = 256
CONV_COLS = 256
ATT_T = 512
SAFE_SCORE_BOUND = 40.0
ML_ROWS = 4
ML_STATE_ROWS = ML_DIM + 16
GATE_ROWS = 16
CONV_HALO = 8


def _const_spec(shape):
    nd = len(shape)
    return pl.BlockSpec(shape, lambda *_: (0,) * nd, pipeline_mode=pl.Buffered(1))


def _sigmoid(x):
    return 1.0 / (1.0 + jnp.exp(-x))


def _mod_norm(x, g, shift, scale):
    ms = jnp.mean(x * x, axis=-1, keepdims=True)
    return (x * lax.rsqrt(ms + EPS)) * (g * (1.0 + scale)) + shift


def _ada_kernel(c_ref, w_ref, b_ref, o_ref):
    c = c_ref[...]
    cs = (c * _sigmoid(c)).astype(BF16)
    o_ref[...] = jnp.dot(cs, w_ref[...].astype(BF16), preferred_element_type=F32) + b_ref[...]


def _ada(c, w_ada, b_ada):
    bsz, d = c.shape
    n = w_ada.shape[1]
    tn = 1024
    return pl.pallas_call(
        _ada_kernel,
        grid=(n // tn,),
        in_specs=[pl.BlockSpec((bsz, d), lambda j: (0, 0)),
                  pl.BlockSpec((d, tn), lambda j: (0, j)),
                  pl.BlockSpec((1, tn), lambda j: (0, j))],
        out_specs=pl.BlockSpec((bsz, tn), lambda j: (0, j)),
        out_shape=jax.ShapeDtypeStruct((bsz, n), F32),
        compiler_params=pltpu.CompilerParams(dimension_semantics=("arbitrary",)),
        name="adaln_mod",
    )(c, w_ada, b_ada.reshape(1, n))


def _ffn_body(x, mod_ref, g_ref, w12_ref, w3_ref, o_ref, act_ref):
    mod = mod_ref[0]
    hb = _mod_norm(x, g_ref[...], mod[0:1], mod[1:2]).astype(BF16)
    for c in range(D_FF // FFN_SUB):
        cols = slice(c * FFN_SUB, (c + 1) * FFN_SUB)
        a = jnp.dot(hb, w12_ref[:, cols], preferred_element_type=F32)
        b = jnp.dot(hb, w12_ref[:, D_FF + c * FFN_SUB:D_FF + (c + 1) * FFN_SUB], preferred_element_type=F32)
        act_ref[:, cols] = (a * _sigmoid(a) * b).astype(BF16)
    y = jnp.dot(act_ref[...], w3_ref[...], preferred_element_type=F32)
    o_ref[0] = x + (0.5 * (1.0 + mod[2:3])) * y


def _ffn_kernel(x_ref, mod_ref, g_ref, w12_ref, w3_ref, o_ref, act_ref):
    _ffn_body(x_ref[0], mod_ref, g_ref, w12_ref, w3_ref, o_ref, act_ref)


def _mix_ffn_kernel(x_ref, mmod_ref, ya_ref, ym_ref, wa_ref, wm_ref,
                    mod_ref, g_ref, w12_ref, w3_ref, o_ref, act_ref):
    y = (jnp.dot(ya_ref[0], wa_ref[...], preferred_element_type=F32)
         + jnp.dot(ym_ref[0], wm_ref[...], preferred_element_type=F32))
    x = x_ref[0] + (1.0 + mmod_ref[0][2:3]) * y
    _ffn_body(x, mod_ref, g_ref, w12_ref, w3_ref, o_ref, act_ref)


def _ffn(x, mod3, g, w12, w3, mix=None):
    bsz, s, d = x.shape
    tm = FFN_TM
    tok_spec = lambda width: pl.BlockSpec((1, tm, width), lambda b, i: (b, i, 0))
    mod_spec = pl.BlockSpec((1, 3, d), lambda b, i: (b, 0, 0))
    ffn_specs = [mod_spec, _const_spec((1, d)),
                 _const_spec(w12.shape), _const_spec(w3.shape)]
    ffn_args = (mod3, g.reshape(1, d), w12, w3)
    if mix is None:
        body, in_specs, args = _ffn_kernel, [tok_spec(d)] + ffn_specs, (x,) + ffn_args
    else:
        mmod3, y_da, y_ml, w_a, w_m = mix
        body = _mix_ffn_kernel
        in_specs = [tok_spec(d), mod_spec, tok_spec(DA_WIDTH), tok_spec(ML_WIDTH),
                    _const_spec(w_a.shape), _const_spec(w_m.shape)] + ffn_specs
        args = (x, mmod3, y_da, y_ml, w_a, w_m) + ffn_args
    return pl.pallas_call(
        body,
        grid=(bsz, s // tm),
        in_specs=in_specs,
        out_specs=tok_spec(d),
        out_shape=jax.ShapeDtypeStruct(x.shape, F32),
        scratch_shapes=[pltpu.VMEM((tm, D_FF), BF16)],
        compiler_params=pltpu.CompilerParams(
            dimension_semantics=("arbitrary", "arbitrary"), vmem_limit_bytes=VMEM_LIMIT),
        name="ffn" if mix is None else "mix_ffn",
    )(*args)


def _group_norm_rope(u, gvec, bd, cos, sina, sinb):
    x2 = u * u
    ssq = jnp.concatenate(
        [jnp.dot(x2[:, j:j + MXU_TILE].astype(BF16), bd, preferred_element_type=F32)
         for j in range(0, u.shape[1], MXU_TILE)], axis=1)
    xn = (u * lax.rsqrt(ssq * (1.0 / DA_QK_DIM) + EPS)) * gvec
    outs = []
    for h in range(u.shape[1] // LANES):
        xh = xn[:, h * LANES:(h + 1) * LANES]
        up = pltpu.roll(xh, LANES - ROPE_DIM // 2, 1)
        dn = pltpu.roll(xh, ROPE_DIM // 2, 1)
        outs.append(xh * cos + up * sina + dn * sinb)
    return jnp.concatenate(outs, axis=1)


def _chunk_scan(x, op, identity):
    pos = lax.broadcasted_iota(jnp.int32, x.shape, 1) & (ML_CHUNK - 1)
    d = 1
    while d < ML_CHUNK:
        x = op(x, jnp.where(pos >= d, pltpu.roll(x, d, 1), identity))
        d *= 2
    return x


def _inproj_kernel(x_ref, mod_ref, g_ref, wq_ref, wk_ref, wv_ref, wmqk_ref, wmv_ref, wmo_ref, wif_ref,
                   gq_ref, gk_ref, rope_ref, bd_ref, cw_ref, cb_ref, bif_ref,
                   q_out, k_out, v_out, mq_out, mk_out, mv_out, mo_out, gate_out, ext_ref):
    tm = x_ref.shape[1]
    si = pl.program_id(1)
    mod = mod_ref[0]
    nh = ML_HEADS
    nt = (((1,), (1,)), ((), ()))
    bd = bd_ref[...]
    cw = cw_ref[...]

    ext_ref[0:CONV_HALO, :] = jnp.where(si == 0, 0.0, ext_ref[0:CONV_HALO, :])

    for r0 in range(0, tm, PROJ_SUB):
        rows = slice(r0, r0 + PROJ_SUB)
        hb = _mod_norm(x_ref[0, rows, :], g_ref[...], mod[0:1], mod[1:2]).astype(BF16)

        pre = lax.dot_general(wif_ref[...], hb, nt, preferred_element_type=F32)[0:2 * nh] + bif_ref[...]
        zf = pre[nh:2 * nh]
        log_f = jnp.minimum(zf, 0.0) - jnp.log1p(jnp.exp(-jnp.abs(zf)))
        b = _chunk_scan(log_f, jnp.add, 0.0)
        c = pre[0:nh] - b
        gate_out[0, r0 // ML_CHUNK] = jnp.concatenate(
            [c, b, _chunk_scan(c, jnp.maximum, -jnp.inf), jnp.zeros_like(c)], axis=0)

        cos, sina, sinb = (rope_ref[rows, j * LANES:(j + 1) * LANES] for j in range(3))

        def conv_chunk(ci):
            cols = slice(ci * CONV_COLS, (ci + 1) * CONV_COLS)
            u = jnp.dot(hb, wmqk_ref[:, cols], preferred_element_type=F32)
            ext_ref[CONV_HALO + r0:CONV_HALO + r0 + PROJ_SUB, cols] = u
            acc = cb_ref[:, cols]
            for j in range(CONV_K - 1):
                off = CONV_HALO - (CONV_K - 1) + j + r0
                acc = acc + ext_ref[off:off + PROJ_SUB, cols] * cw[j:j + 1, cols]
            acc = acc + u * cw[CONV_K - 1:CONV_K, cols]
            act = acc * _sigmoid(acc)
            if ci < ML_WIDTH // CONV_COLS:
                mq_out[0, rows, cols] = (act * (ML_DIM ** -0.5)).astype(BF16)
            else:
                mk_out[0, rows, ci * CONV_COLS - ML_WIDTH:(ci + 1) * CONV_COLS - ML_WIDTH] = act.astype(BF16)

        chunks = iter(range(2 * ML_WIDTH // CONV_COLS))
        per_slot = 2 * ML_WIDTH // CONV_COLS // 4
        uq = jnp.dot(hb, wq_ref[...], preferred_element_type=F32)
        for _ in range(per_slot):
            conv_chunk(next(chunks))
        q_out[0, rows, :] = _group_norm_rope(uq, gq_ref[...], bd, cos, sina, sinb).astype(BF16)
        uk = jnp.dot(hb, wk_ref[...], preferred_element_type=F32)
        for _ in range(per_slot):
            conv_chunk(next(chunks))
        k_out[0, rows, :] = _group_norm_rope(uk, gk_ref[...], bd, cos, sina, sinb).astype(BF16)
        uv = jnp.dot(hb, wv_ref[...], preferred_element_type=F32)
        umv_t = lax.dot_general(wmv_ref[...], hb, nt, preferred_element_type=F32)
        for _ in range(per_slot):
            conv_chunk(next(chunks))
        v_out[0, rows, :] = uv.astype(BF16)
        mv_out[0, r0 // ML_CHUNK] = umv_t.astype(BF16)
        umo = jnp.dot(hb, wmo_ref[...], preferred_element_type=F32)
        for _ in range(per_slot):
            conv_chunk(next(chunks))
        mo_out[0, rows, :] = _sigmoid(umo).astype(BF16)

    ext_ref[0:CONV_HALO, :] = ext_ref[tm:tm + CONV_HALO, :]


def _inproj(x, mod3, g, wq, wk, wv, wmqk, wmv, wmo, wif, gq, gk, rope, bd, cw, cb, bif):
    bsz, s, d = x.shape
    tm = PROJ_TM
    tok = lambda width, dt: jax.ShapeDtypeStruct((bsz, s, width), dt)
    tok_spec = lambda width: pl.BlockSpec((1, tm, width), lambda b, i: (b, i, 0))
    chunk_spec = lambda nrows: pl.BlockSpec((1, tm // ML_CHUNK, nrows, ML_CHUNK), lambda b, i: (b, i, 0, 0))
    return pl.pallas_call(
        _inproj_kernel,
        grid=(bsz, s // tm),
        in_specs=[tok_spec(d),
                  pl.BlockSpec((1, 3, d), lambda b, i: (b, 0, 0)),
                  _const_spec((1, d)),
                  _const_spec(wq.shape), _const_spec(wk.shape), _const_spec(wv.shape),
                  _const_spec(wmqk.shape), _const_spec(wmv.shape), _const_spec(wmo.shape),
                  _const_spec(wif.shape),
                  _const_spec(gq.shape), _const_spec(gk.shape),
                  pl.BlockSpec((tm, 3 * LANES), lambda b, i: (i, 0)),
                  _const_spec(bd.shape), _const_spec(cw.shape), _const_spec(cb.shape),
                  _const_spec(bif.shape)],
        out_specs=[tok_spec(DA_WIDTH), tok_spec(DA_WIDTH), tok_spec(DA_WIDTH),
                   tok_spec(ML_WIDTH), tok_spec(ML_WIDTH),
                   chunk_spec(ML_WIDTH),
                   tok_spec(ML_WIDTH),
                   chunk_spec(GATE_ROWS)],
        out_shape=[tok(DA_WIDTH, BF16), tok(DA_WIDTH, BF16), tok(DA_WIDTH, BF16),
                   tok(ML_WIDTH, BF16), tok(ML_WIDTH, BF16),
                   jax.ShapeDtypeStruct((bsz, s // ML_CHUNK, ML_WIDTH, ML_CHUNK), BF16),
                   tok(ML_WIDTH, BF16),
                   jax.ShapeDtypeStruct((bsz, s // ML_CHUNK, GATE_ROWS, ML_CHUNK), F32)],
        scratch_shapes=[pltpu.VMEM((tm + CONV_HALO, 2 * ML_WIDTH), F32)],
        compiler_params=pltpu.CompilerParams(
            dimension_semantics=("arbitrary", "arbitrary"), vmem_limit_bytes=VMEM_LIMIT),
        name="in_proj",
    )(x, mod3, g.reshape(1, d), wq, wk, wv, wmqk, wmv, wmo, wif, gq, gk, rope, bd, cw, cb, bif)


def _attn_kernel(bounded_ref, q_ref, k_ref, v_ref, lam_ref, gout_ref, o_ref, qs_ref, m_ref, acc_ref):
    t = ATT_T
    u = t // 2
    lane = lax.broadcasted_iota(jnp.int32, (u, DA_V_DIM), 1)
    ones_cols = jnp.ones((t, DA_V_DIM), BF16)
    lv = lam_ref[...]
    lam = (jnp.exp(jnp.sum(lv[0:1] * lv[1:2], axis=1, keepdims=True))
           - jnp.exp(jnp.sum(lv[2:3] * lv[3:4], axis=1, keepdims=True)) + LAMBDA_INIT)

    def query_rows(qi, half, part):
        return pl.ds(pl.multiple_of(qi * (2 * t) + half * t + part * u, u), u)

    def stack_queries(qi):
        for half in range(2):
            for part in range(2):
                q = q_ref[0, query_rows(qi, half, part), :]
                zero = jnp.zeros_like(q)
                qs_ref[half, (2 * part) * u:(2 * part + 1) * u, :] = jnp.where(lane < DA_QK_DIM, q, zero)
                qs_ref[half, (2 * part + 1) * u:(2 * part + 2) * u, :] = jnp.where(lane >= DA_QK_DIM, q, zero)

    def keys(kb):
        start = pl.multiple_of(kb * t, t)
        return k_ref[0, pl.ds(start, t), :], jnp.concatenate([v_ref[0, pl.ds(start, t), :], ones_cols], axis=1)

    def piece(half, rows, ks, vs, causal_shift, first, stabilised):
        s = lax.dot_general(qs_ref[half, rows, :], ks, (((1,), (1,)), ((), ())), preferred_element_type=F32)
        if causal_shift is not None:
            row = lax.broadcasted_iota(jnp.int32, s.shape, 0) & (u - 1)
            col = lax.broadcasted_iota(jnp.int32, s.shape, 1)
            s = jnp.where(col <= row + causal_shift, s, -jnp.inf)
        if stabilised:
            m_new = jnp.max(s, axis=1, keepdims=True)
            if not first:
                m_prev = m_ref[half, rows, :]
                m_new = jnp.maximum(m_prev, m_new)
                alpha = jnp.exp2(m_prev - m_new)
            m_ref[half, rows, :] = m_new
            s = s - m_new
        pv = jnp.dot(jnp.exp2(s).astype(BF16), vs, preferred_element_type=F32)
        if first:
            acc_ref[half, rows, :] = pv
        elif stabilised:
            acc_ref[half, rows, :] = alpha * acc_ref[half, rows, :] + pv
        else:
            acc_ref[half, rows, :] += pv

    def block(half, kv, diagonal, first, stabilised):
        ks, vs = kv
        if diagonal:
            piece(half, slice(0, 2 * u), ks[0:u], vs[0:u], 0, first, stabilised)
            piece(half, slice(2 * u, 4 * u), ks, vs, u, first, stabilised)
        else:
            piece(half, slice(0, 4 * u), ks, vs, None, first, stabilised)

    def tile_pair(qi, stabilised):
        stack_queries(qi)
        kv = keys(2 * qi)
        block(0, kv, True, True, stabilised)
        block(1, kv, False, True, stabilised)
        block(1, keys(2 * qi + 1), True, False, stabilised)

        def body(pair, carry):
            for j in range(2):
                kv = keys(2 * pair + j)
                block(0, kv, False, False, stabilised)
                block(1, kv, False, False, stabilised)
            return carry

        lax.fori_loop(0, qi, body, 0)

        for half in range(2):
            for part in range(2):
                rows = slice(2 * part * u, (2 * part + 2) * u)
                o = acc_ref[half, rows, 0:DA_V_DIM] / acc_ref[half, rows, DA_V_DIM:2 * DA_V_DIM]
                od = o[0:u] - lam * o[u:2 * u]
                ms = jnp.mean(od * od, axis=-1, keepdims=True)
                o_ref[0, query_rows(qi, half, part), :] = (
                    (od * lax.rsqrt(ms + EPS)) * gout_ref[...] * (1.0 - LAMBDA_INIT)).astype(BF16)

    def run(stabilised):
        def body(qi, carry):
            tile_pair(qi, stabilised)
            return carry

        lax.fori_loop(0, q_ref.shape[1] // (2 * t), body, 0)

    @pl.when(bounded_ref[0] != 0)
    def _():
        run(False)

    @pl.when(bounded_ref[0] == 0)
    def _():
        run(True)


def _attention(bounded, q, k, v, lam_vecs, g_out):
    bsz, s, _ = q.shape
    t = ATT_T
    full = pl.BlockSpec((1, s, DA_V_DIM), lambda b, h: (b, 0, h))
    return pl.pallas_call(
        _attn_kernel,
        grid=(bsz, DA_HEADS),
        in_specs=[pl.BlockSpec(memory_space=pltpu.SMEM),
                  full, full, full, _const_spec(lam_vecs.shape), _const_spec((1, DA_V_DIM))],
        out_specs=full,
        out_shape=jax.ShapeDtypeStruct((bsz, s, DA_WIDTH), BF16),
        scratch_shapes=[pltpu.VMEM((2, 2 * t, DA_V_DIM), BF16),
                        pltpu.VMEM((2, 2 * t, 1), F32),
                        pltpu.VMEM((2, 2 * t, 2 * DA_V_DIM), F32)],
        compiler_params=pltpu.CompilerParams(
            dimension_semantics=("arbitrary", "arbitrary"), vmem_limit_bytes=VMEM_LIMIT),
        name="diff_attention",
    )(bounded, q, k, v, lam_vecs, g_out.reshape(1, DA_V_DIM))


def _mlstm_kernel(q_ref, k_ref, vt_ref, o_ref, gr_ref, gout_ref, y_ref, ct_ref, m_ref):
    ln = ML_CHUNK
    nh = ML_HEADS
    nt = (((1,), (1,)), ((), ()))

    @pl.when(pl.program_id(1) == 0)
    def _():
        ct_ref[...] = jnp.zeros(ct_ref.shape, F32)
        m_ref[...] = jnp.zeros(m_ref.shape, F32)

    row = lax.broadcasted_iota(jnp.int32, (ln, ln), 0)
    col = lax.broadcasted_iota(jnp.int32, (ln, ln), 1)
    lower = col <= row
    ones_row = (lax.broadcasted_iota(jnp.int32, (ML_STATE_ROWS - ML_DIM, ln), 0) == 0).astype(BF16)

    for r in range(ML_ROWS):
        gr = gr_ref[r, 0]
        c, b = gr[0:nh], gr[nh:2 * nh]
        m_prev = jnp.concatenate([m_ref[r, 0:nh, :]] * (ln // LANES), axis=1)
        m_run = jnp.maximum(gr[2 * nh:3 * nh], m_prev)
        m_last = m_run[:, ln - 1:ln]
        e_inter = jnp.exp(m_prev - m_run)
        e_floor = jnp.exp(-(b + m_run))
        w_in = jnp.exp(c - m_last)
        decay = e_inter[:, ln - 1:ln]
        m_ref[r, 0:nh, :] = jnp.broadcast_to(b[:, ln - 1:ln] + m_last, (nh, LANES))
        m_run_cols = jnp.concatenate([m_run, m_run], axis=0).T

        for h in range(nh):
            lanes = slice(h * ML_DIM, (h + 1) * ML_DIM)
            q = q_ref[r, :, lanes]
            k = k_ref[r, :, lanes]
            vt_aug = jnp.concatenate([vt_ref[r, 0, lanes, :], ones_row], axis=0)
            s = lax.dot_general(q, k, nt, preferred_element_type=F32)
            gate = jnp.exp(jnp.where(lower, c[h:h + 1, :] - m_run_cols[:, h:h + 1], -jnp.inf))
            w = (s * gate).astype(BF16)
            state_t = ct_ref[r, h]
            full_t = (e_inter[h:h + 1, :]
                      * lax.dot_general(state_t.astype(BF16), q, nt, preferred_element_type=F32)
                      + lax.dot_general(vt_aug, w, nt, preferred_element_type=F32))
            den = full_t[ML_DIM:ML_DIM + 1, :]
            hh_t = full_t[0:ML_DIM, :] / jnp.maximum(jnp.abs(den), e_floor[h:h + 1, :])
            hn_t = hh_t * lax.rsqrt(jnp.mean(hh_t * hh_t, axis=0, keepdims=True) + EPS)
            y_ref[r, :, lanes] = (o_ref[r, :, lanes].astype(F32) * (hn_t.T * gout_ref[:, lanes])).astype(BF16)

            vw = (vt_aug.astype(F32) * w_in[h:h + 1, :]).astype(BF16)
            ct_ref[r, h] = decay[h:h + 1, :] * state_t + jnp.dot(vw, k, preferred_element_type=F32)


def _mlstm(q, k, vt, o, gates_row, g_out):
    bsz, s, _ = q.shape
    ln = ML_CHUNK
    rows = ML_ROWS
    tok = pl.BlockSpec((rows, ln, ML_WIDTH), lambda b, c: (b, c, 0))
    return pl.pallas_call(
        _mlstm_kernel,
        grid=(bsz // rows, s // ln),
        in_specs=[tok, tok, pl.BlockSpec((rows, 1, ML_WIDTH, ln), lambda b, c: (b, c, 0, 0)), tok,
                  pl.BlockSpec((rows, 1, GATE_ROWS, ln), lambda b, c: (b, c, 0, 0)),
                  _const_spec((1, ML_WIDTH))],
        out_specs=tok,
        out_shape=jax.ShapeDtypeStruct((bsz, s, ML_WIDTH), BF16),
        scratch_shapes=[pltpu.VMEM((rows, ML_HEADS, ML_STATE_ROWS, ML_DIM), F32),
                        pltpu.VMEM((rows, 8, LANES), F32)],
        compiler_params=pltpu.CompilerParams(
            dimension_semantics=("arbitrary", "arbitrary"), vmem_limit_bytes=VMEM_LIMIT),
        name="mlstm",
    )(q, k, vt, o, gates_row, g_out.reshape(1, ML_WIDTH))


def _rope_tables(s):
    half = ROPE_DIM // 2
    pos = jnp.arange(s, dtype=F32)
    inv_freq = ROPE_THETA ** (-jnp.arange(0, ROPE_DIM, 2, dtype=F32) / ROPE_DIM)
    ang = pos[:, None] * inv_freq[None, :]
    basis = jnp.concatenate([jnp.cos(ang), jnp.sin(ang), jnp.ones((s, 1), F32)], axis=1)
    g = jnp.arange(LANES) % DA_QK_DIM
    j = jnp.arange(2 * half + 1)[:, None]
    cos_sel = jnp.where(g < ROPE_DIM, j == g % half, j == 2 * half).astype(F32)
    sin_up_sel = -((g < half) & (j == half + g)).astype(F32)
    sin_dn_sel = ((g >= half) & (g < ROPE_DIM) & (j == g)).astype(F32)
    sel = jnp.concatenate([cos_sel, sin_up_sel, sin_dn_sel], axis=1)
    return jnp.dot(basis, sel, precision=lax.Precision.HIGHEST)


def kernel(x, c, w_ada, b_ada, g_norm, ffn1_w12, ffn1_w3, w_in, conv_w, conv_b, b_igate, b_fgate,
           g_qnorm, g_knorm, lambda_qk, g_da_out, g_ml_out, w_out, ffn2_w12, ffn2_w3):
    bsz, s, d = x.shape
    l = 0
    mod = _ada(c, w_ada[l], b_ada[l]).reshape(bsz, 3, 3, d)

    def ffn(xin, sub, w12, w3, mix=None):
        return _ffn(xin, mod[:, sub], g_norm[l, sub], w12.astype(BF16), w3.astype(BF16), mix)

    x = ffn(x, 0, ffn1_w12[l], ffn1_w3[l])

    wb = w_in[l].astype(BF16)
    o0 = 0
    parts = []
    for width in (DA_WIDTH, DA_WIDTH, DA_WIDTH, 2 * ML_WIDTH, ML_WIDTH, ML_WIDTH, 2 * ML_HEADS):
        parts.append(wb[:, o0:o0 + width])
        o0 += width
    wq, wk, wv, wmqk, wmv, wmo, wif = parts
    wif = jnp.pad(wif.T, ((0, 2 * ML_HEADS), (0, 0)))
    wmv = wmv.T
    groups = DA_WIDTH // DA_QK_DIM
    q_gain = g_qnorm[l] * (DA_QK_DIM ** -0.5)
    gq = jnp.tile(q_gain * LOG2E, groups).reshape(1, DA_WIDTH)
    score_bound = 1.05 * DA_QK_DIM * jnp.max(jnp.abs(q_gain)) * jnp.max(jnp.abs(g_knorm[l]))
    bounded = (score_bound <= SAFE_SCORE_BOUND).astype(jnp.int32).reshape(1)
    gk = jnp.tile(g_knorm[l], groups).reshape(1, DA_WIDTH)
    rope = _rope_tables(s)
    gid = jnp.arange(MXU_TILE) // DA_QK_DIM
    bd = (gid[:, None] == gid[None, :]).astype(BF16)
    bif = jnp.concatenate([b_igate[l], b_fgate[l]]).reshape(2 * ML_HEADS, 1)
    da_q, da_k, da_v, ml_q, ml_k, ml_v, ml_o, gates = _inproj(
        x, mod[:, 1], g_norm[l, 1], wq, wk, wv, wmqk, wmv, wmo, wif, gq, gk, rope, bd,
        conv_w[l], conv_b[l].reshape(1, 2 * ML_WIDTH), bif)

    y_da = _attention(bounded, da_q, da_k, da_v, lambda_qk[l], g_da_out[l])
    y_ml = _mlstm(ml_q, ml_k, ml_v, ml_o, gates, g_ml_out[l])

    wo = w_out[l].astype(BF16)
    return ffn(x, 2, ffn2_w12[l], ffn2_w3[l], mix=(mod[:, 1], y_da, y_ml, wo[:DA_WIDTH], wo[DA_WIDTH:]))
```

```python
import functools

import jax
import jax.numpy as jnp
from jax import lax
from jax.experimental import pallas as pl
from jax.experimental.pallas import tpu as pltpu

F32 = jnp.float32
BF16 = jnp.bfloat16

D_MODEL = 1024
DA_HEADS = 4
DA_QK_DIM = 64
DA_V_DIM = 2 * DA_QK_DIM
DA_WIDTH = DA_HEADS * DA_V_DIM
ML_HEADS = 4
ML_DIM = 128
ML_WIDTH = ML_HEADS * ML_DIM
ROPE_THETA = 500000.0
ROPE_DIM = DA_QK_DIM // 4
D_FF = 2816
CONV_K = 4
EPS = 1e-6
LAMBDA_INIT = 0.8 - 0.6 * 1.0
LOG2E = 1.4426950408889634

LANES = 128
MXU_TILE = 256
VMEM_LIMIT = 56 * 1024 * 1024

FFN_TM = 1024
FFN_SUB = 256
PROJ_TM = 1024
PROJ_SUB = 256
CONV_COLS = 256
ATT_T = 512
SAFE_SCORE_BOUND = 40.0
ML_CHUNK = 256
ML_ROWS = 4
ML_STATE_ROWS = ML_DIM + 16
GATE_ROWS = 16
CONV_HALO = 8


def _const_spec(shape):
    nd = len(shape)
    return pl.BlockSpec(shape, lambda *_: (0,) * nd, pipeline_mode=pl.Buffered(1))


def _sigmoid(x):
    return 1.0 / (1.0 + jnp.exp(-x))


def _mod_norm(x, g, shift, scale):
    ms = jnp.mean(x * x, axis=-1, keepdims=True)
    return (x * lax.rsqrt(ms + EPS)) * (g * (1.0 + scale)) + shift


def _ada_kernel(c_ref, w_ref, b_ref, o_ref):
    c = c_ref[...]
    cs = (c * _sigmoid(c)).astype(BF16)
    o_ref[...] = jnp.dot(cs, w_ref[...].astype(BF16), preferred_element_type=F32) + b_ref[...]


def _ada(c, w_ada, b_ada):
    bsz, d = c.shape
    n = w_ada.shape[1]
    tn = 1024
    return pl.pallas_call(
        _ada_kernel,
        grid=(n // tn,),
        in_specs=[pl.BlockSpec((bsz, d), lambda j: (0, 0)),
                  pl.BlockSpec((d, tn), lambda j: (0, j)),
                  pl.BlockSpec((1, tn), lambda j: (0, j))],
        out_specs=pl.BlockSpec((bsz, tn), lambda j: (0, j)),
        out_shape=jax.ShapeDtypeStruct((bsz, n), F32),
        compiler_params=pltpu.CompilerParams(dimension_semantics=("arbitrary",)),
        name="adaln_mod",
    )(c, w_ada, b_ada.reshape(1, n))


def _ffn_body(x, mod_ref, g_ref, w12_ref, w3_ref, o_ref, act_ref):
    mod = mod_ref[0]
    hb = _mod_norm(x, g_ref[...], mod[0:1], mod[1:2]).astype(BF16)
    for c in range(D_FF // FFN_SUB):
        cols = slice(c * FFN_SUB, (c + 1) * FFN_SUB)
        a = jnp.dot(hb, w12_ref[:, cols], preferred_element_type=F32)
        b = jnp.dot(hb, w12_ref[:, D_FF + c * FFN_SUB:D_FF + (c + 1) * FFN_SUB], preferred_element_type=F32)
        act_ref[:, cols] = (a * _sigmoid(a) * b).astype(BF16)
    y = jnp.dot(act_ref[...], w3_ref[...], preferred_element_type=F32)
    o_ref[0] = x + (0.5 * (1.0 + mod[2:3])) * y


def _ffn_kernel(x_ref, mod_ref, g_ref, w12_ref, w3_ref, o_ref, act_ref):
    _ffn_body(x_ref[0], mod_ref, g_ref, w12_ref, w3_ref, o_ref, act_ref)


def _mix_ffn_kernel(x_ref, mmod_ref, ya_ref, ym_ref, wa_ref, wm_ref,
                    mod_ref, g_ref, w12_ref, w3_ref, o_ref, act_ref):
    y = (jnp.dot(ya_ref[0], wa_ref[...], preferred_element_type=F32)
         + jnp.dot(ym_ref[0], wm_ref[...], preferred_element_type=F32))
    x = x_ref[0] + (1.0 + mmod_ref[0][2:3]) * y
    _ffn_body(x, mod_ref, g_ref, w12_ref, w3_ref, o_ref, act_ref)


def _ffn(x, mod3, g, w12, w3, mix=None):
    bsz, s, d = x.shape
    tm = FFN_TM
    tok_spec = lambda width: pl.BlockSpec((1, tm, width), lambda b, i: (b, i, 0))
    mod_spec = pl.BlockSpec((1, 3, d), lambda b, i: (b, 0, 0))
    ffn_specs = [mod_spec, _const_spec((1, d)),
                 _const_spec(w12.shape), _const_spec(w3.shape)]
    ffn_args = (mod3, g.reshape(1, d), w12, w3)
    if mix is None:
        body, in_specs, args = _ffn_kernel, [tok_spec(d)] + ffn_specs, (x,) + ffn_args
    else:
        mmod3, y_da, y_ml, w_a, w_m = mix
        body = _mix_ffn_kernel
        in_specs = [tok_spec(d), mod_spec, tok_spec(DA_WIDTH), tok_spec(ML_WIDTH),
                    _const_spec(w_a.shape), _const_spec(w_m.shape)] + ffn_specs
        args = (x, mmod3, y_da, y_ml, w_a, w_m) + ffn_args
    return pl.pallas_call(
        body,
        grid=(bsz, s // tm),
        in_specs=in_specs,
        out_specs=tok_spec(d),
        out_shape=jax.ShapeDtypeStruct(x.shape, F32),
        scratch_shapes=[pltpu.VMEM((tm, D_FF), BF16)],
        compiler_params=pltpu.CompilerParams(
            dimension_semantics=("arbitrary", "arbitrary"), vmem_limit_bytes=VMEM_LIMIT),
        name="ffn" if mix is None else "mix_ffn",
    )(*args)


def _group_norm_rope(u, gvec, bd, cos, sina, sinb):
    x2 = u * u
    ssq = jnp.concatenate(
        [jnp.dot(x2[:, j:j + MXU_TILE].astype(BF16), bd, preferred_element_type=F32)
         for j in range(0, u.shape[1], MXU_TILE)], axis=1)
    xn = (u * lax.rsqrt(ssq * (1.0 / DA_QK_DIM) + EPS)) * gvec
    outs = []
    for h in range(u.shape[1] // LANES):
        xh = xn[:, h * LANES:(h + 1) * LANES]
        up = pltpu.roll(xh, LANES - ROPE_DIM // 2, 1)
        dn = pltpu.roll(xh, ROPE_DIM // 2, 1)
        outs.append(xh * cos + up * sina + dn * sinb)
    return jnp.concatenate(outs, axis=1)


def _chunk_scan(x, op, identity):
    pos = lax.broadcasted_iota(jnp.int32, x.shape, 1) & (ML_CHUNK - 1)
    d = 1
    while d < ML_CHUNK:
        x = op(x, jnp.where(pos >= d, pltpu.roll(x, d, 1), identity))
        d *= 2
    return x


def _inproj_kernel(x_ref, mod_ref, g_ref, wq_ref, wk_ref, wv_ref, wmqk_ref, wmv_ref, wmo_ref, wif_ref,
                   gq_ref, gk_ref, rope_ref, bd_ref, cw_ref, cb_ref, bif_ref,
                   q_out, k_out, v_out, mq_out, mk_out, mv_out, mo_out, gate_out, halo_ref):
    tm = x_ref.shape[1]
    si = pl.program_id(1)
    mod = mod_ref[0]
    nh = ML_HEADS
    nt = (((1,), (1,)), ((), ()))
    bd = bd_ref[...]
    cw = cw_ref[...]

    tails = jnp.where(si == 0, 0.0, halo_ref[...])
    tails = [tails[:, ci * CONV_COLS:(ci + 1) * CONV_COLS] for ci in range(2 * ML_WIDTH // CONV_COLS)]

    for r0 in range(0, tm, PROJ_SUB):
        rows = slice(r0, r0 + PROJ_SUB)
        hb = _mod_norm(x_ref[0, rows, :], g_ref[...], mod[0:1], mod[1:2]).astype(BF16)

        pre = lax.dot_general(wif_ref[...], hb, nt, preferred_element_type=F32)[0:2 * nh] + bif_ref[...]
        zf = pre[nh:2 * nh]
        log_f = jnp.minimum(zf, 0.0) - jnp.log1p(jnp.exp(-jnp.abs(zf)))
        b = _chunk_scan(log_f, jnp.add, 0.0)
        c = pre[0:nh] - b
        gate_out[0, :, rows] = jnp.concatenate(
            [c, b, _chunk_scan(c, jnp.maximum, -jnp.inf), jnp.zeros_like(c)], axis=0)

        cos, sina, sinb = (rope_ref[rows, j * LANES:(j + 1) * LANES] for j in range(3))

        def conv_chunk(ci):
            cols = slice(ci * CONV_COLS, (ci + 1) * CONV_COLS)
            u = jnp.dot(hb, wmqk_ref[:, cols], preferred_element_type=F32)
            ext = jnp.concatenate([tails[ci], u], axis=0)
            tails[ci] = u[PROJ_SUB - CONV_HALO:, :]
            acc = cb_ref[:, cols] + u * cw[CONV_K - 1:CONV_K, cols]
            for d in range(1, CONV_K):
                acc = acc + pltpu.roll(ext, d, 0)[CONV_HALO:, :] * cw[CONV_K - 1 - d:CONV_K - d, cols]
            act = acc * _sigmoid(acc)
            if ci < ML_WIDTH // CONV_COLS:
                mq_out[0, rows, cols] = (act * (ML_DIM ** -0.5)).astype(BF16)
            else:
                mk_out[0, rows, ci * CONV_COLS - ML_WIDTH:(ci + 1) * CONV_COLS - ML_WIDTH] = act.astype(BF16)

        chunks = iter(range(2 * ML_WIDTH // CONV_COLS))
        per_slot = 2 * ML_WIDTH // CONV_COLS // 4
        uq = jnp.dot(hb, wq_ref[...], preferred_element_type=F32)
        for _ in range(per_slot):
            conv_chunk(next(chunks))
        q_out[0, rows, :] = _group_norm_rope(uq, gq_ref[...], bd, cos, sina, sinb).astype(BF16)
        uk = jnp.dot(hb, wk_ref[...], preferred_element_type=F32)
        for _ in range(per_slot):
            conv_chunk(next(chunks))
        k_out[0, rows, :] = _group_norm_rope(uk, gk_ref[...], bd, cos, sina, sinb).astype(BF16)
        uv = jnp.dot(hb, wv_ref[...], preferred_element_type=F32)
        umv_t = lax.dot_general(wmv_ref[...], hb, nt, preferred_element_type=F32)
        for _ in range(per_slot):
            conv_chunk(next(chunks))
        v_out[0, rows, :] = uv.astype(BF16)
        mv_out[0, :, rows] = umv_t.astype(BF16)
        umo = jnp.dot(hb, wmo_ref[...], preferred_element_type=F32)
        for _ in range(per_slot):
            conv_chunk(next(chunks))
        mo_out[0, rows, :] = _sigmoid(umo).astype(BF16)

    halo_ref[...] = jnp.concatenate(tails, axis=1)


def _inproj(x, mod3, g, wq, wk, wv, wmqk, wmv, wmo, wif, gq, gk, rope, bd, cw, cb, bif):
    bsz, s, d = x.shape
    tm = PROJ_TM
    tok = lambda width, dt: jax.ShapeDtypeStruct((bsz, s, width), dt)
    tok_spec = lambda width: pl.BlockSpec((1, tm, width), lambda b, i: (b, i, 0))
    return pl.pallas_call(
        _inproj_kernel,
        grid=(bsz, s // tm),
        in_specs=[tok_spec(d),
                  pl.BlockSpec((1, 3, d), lambda b, i: (b, 0, 0)),
                  _const_spec((1, d)),
                  _const_spec(wq.shape), _const_spec(wk.shape), _const_spec(wv.shape),
                  _const_spec(wmqk.shape), _const_spec(wmv.shape), _const_spec(wmo.shape),
                  _const_spec(wif.shape),
                  _const_spec(gq.shape), _const_spec(gk.shape),
                  pl.BlockSpec((tm, 3 * LANES), lambda b, i: (i, 0)),
                  _const_spec(bd.shape), _const_spec(cw.shape), _const_spec(cb.shape),
                  _const_spec(bif.shape)],
        out_specs=[tok_spec(DA_WIDTH), tok_spec(DA_WIDTH), tok_spec(DA_WIDTH),
                   tok_spec(ML_WIDTH), tok_spec(ML_WIDTH),
                   pl.BlockSpec((1, ML_WIDTH, tm), lambda b, i: (b, 0, i)),
                   tok_spec(ML_WIDTH),
                   pl.BlockSpec((1, GATE_ROWS, tm), lambda b, i: (b, 0, i))],
        out_shape=[tok(DA_WIDTH, BF16), tok(DA_WIDTH, BF16), tok(DA_WIDTH, BF16),
                   tok(ML_WIDTH, BF16), tok(ML_WIDTH, BF16),
                   jax.ShapeDtypeStruct((bsz, ML_WIDTH, s), BF16),
                   tok(ML_WIDTH, BF16),
                   jax.ShapeDtypeStruct((bsz, GATE_ROWS, s), F32)],
        scratch_shapes=[pltpu.VMEM((CONV_HALO, 2 * ML_WIDTH), F32)],
        compiler_params=pltpu.CompilerParams(
            dimension_semantics=("arbitrary", "arbitrary"), vmem_limit_bytes=VMEM_LIMIT),
        name="in_proj",
    )(x, mod3, g.reshape(1, d), wq, wk, wv, wmqk, wmv, wmo, wif, gq, gk, rope, bd, cw, cb, bif)


def _attn_kernel(bounded_ref, q_ref, k_ref, v_ref, lam_ref, gout_ref, o_ref, qs_ref, m_ref, acc_ref):
    t = ATT_T
    u = t // 2
    lane = lax.broadcasted_iota(jnp.int32, (u, DA_V_DIM), 1)
    ones_cols = jnp.ones((t, DA_V_DIM), BF16)
    lv = lam_ref[...]
    lam = (jnp.exp(jnp.sum(lv[0:1] * lv[1:2], axis=1, keepdims=True))
           - jnp.exp(jnp.sum(lv[2:3] * lv[3:4], axis=1, keepdims=True)) + LAMBDA_INIT)

    def query_rows(qi, half, part):
        return pl.ds(pl.multiple_of(qi * (2 * t) + half * t + part * u, u), u)

    def stack_queries(qi):
        for half in range(2):
            for part in range(2):
                q = q_ref[0, query_rows(qi, half, part), :]
                zero = jnp.zeros_like(q)
                qs_ref[half, (2 * part) * u:(2 * part + 1) * u, :] = jnp.where(lane < DA_QK_DIM, q, zero)
                qs_ref[half, (2 * part + 1) * u:(2 * part + 2) * u, :] = jnp.where(lane >= DA_QK_DIM, q, zero)

    def keys(kb):
        start = pl.multiple_of(kb * t, t)
        return k_ref[0, pl.ds(start, t), :], jnp.concatenate([v_ref[0, pl.ds(start, t), :], ones_cols], axis=1)

    def piece(half, rows, ks, vs, causal_shift, first, stabilised):
        s = lax.dot_general(qs_ref[half, rows, :], ks, (((1,), (1,)), ((), ())), preferred_element_type=F32)
        if causal_shift is not None:
            row = lax.broadcasted_iota(jnp.int32, s.shape, 0) & (u - 1)
            col = lax.broadcasted_iota(jnp.int32, s.shape, 1)
            s = jnp.where(col <= row + causal_shift, s, -jnp.inf)
        if stabilised:
            m_new = jnp.max(s, axis=1, keepdims=True)
            if not first:
                m_prev = m_ref[half, rows, :]
                m_new = jnp.maximum(m_prev, m_new)
                alpha = jnp.exp2(m_prev - m_new)
            m_ref[half, rows, :] = m_new
            s = s - m_new
        pv = jnp.dot(jnp.exp2(s).astype(BF16), vs, preferred_element_type=F32)
        if first:
            acc_ref[half, rows, :] = pv
        elif stabilised:
            acc_ref[half, rows, :] = alpha * acc_ref[half, rows, :] + pv
        else:
            acc_ref[half, rows, :] += pv

    def block(half, kv, diagonal, first, stabilised):
        ks, vs = kv
        if diagonal:
            piece(half, slice(0, 2 * u), ks[0:u], vs[0:u], 0, first, stabilised)
            piece(half, slice(2 * u, 4 * u), ks, vs, u, first, stabilised)
        else:
            piece(half, slice(0, 4 * u), ks, vs, None, first, stabilised)

    def tile_pair(qi, stabilised):
        stack_queries(qi)
        kv = keys(2 * qi)
        block(0, kv, True, True, stabilised)
        block(1, kv, False, True, stabilised)
        block(1, keys(2 * qi + 1), True, False, stabilised)

        def body(pair, carry):
            for j in range(2):
                kv = keys(2 * pair + j)
                block(0, kv, False, False, stabilised)
                block(1, kv, False, False, stabilised)
            return carry

        lax.fori_loop(0, qi, body, 0)

        for half in range(2):
            for part in range(2):
                rows = slice(2 * part * u, (2 * part + 2) * u)
                o = acc_ref[half, rows, 0:DA_V_DIM] / acc_ref[half, rows, DA_V_DIM:2 * DA_V_DIM]
                od = o[0:u] - lam * o[u:2 * u]
                ms = jnp.mean(od * od, axis=-1, keepdims=True)
                o_ref[0, query_rows(qi, half, part), :] = (
                    (od * lax.rsqrt(ms + EPS)) * gout_ref[...] * (1.0 - LAMBDA_INIT)).astype(BF16)

    def run(stabilised):
        def body(qi, carry):
            tile_pair(qi, stabilised)
            return carry

        lax.fori_loop(0, q_ref.shape[1] // (2 * t), body, 0)

    @pl.when(bounded_ref[0] != 0)
    def _():
        run(False)

    @pl.when(bounded_ref[0] == 0)
    def _():
        run(True)


def _attention(bounded, q, k, v, lam_vecs, g_out):
    bsz, s, _ = q.shape
    t = ATT_T
    full = pl.BlockSpec((1, s, DA_V_DIM), lambda b, h: (b, 0, h))
    return pl.pallas_call(
        _attn_kernel,
        grid=(bsz, DA_HEADS),
        in_specs=[pl.BlockSpec(memory_space=pltpu.SMEM),
                  full, full, full, _const_spec(lam_vecs.shape), _const_spec((1, DA_V_DIM))],
        out_specs=full,
        out_shape=jax.ShapeDtypeStruct((bsz, s, DA_WIDTH), BF16),
        scratch_shapes=[pltpu.VMEM((2, 2 * t, DA_V_DIM), BF16),
                        pltpu.VMEM((2, 2 * t, 1), F32),
                        pltpu.VMEM((2, 2 * t, 2 * DA_V_DIM), F32)],
        compiler_params=pltpu.CompilerParams(
            dimension_semantics=("arbitrary", "arbitrary"), vmem_limit_bytes=VMEM_LIMIT),
        name="diff_attention",
    )(bounded, q, k, v, lam_vecs, g_out.reshape(1, DA_V_DIM))


def _mlstm_kernel(q_ref, k_ref, vt_ref, o_ref, gr_ref, gout_ref, y_ref, ct_ref, m_ref):
    ln = ML_CHUNK
    nh = ML_HEADS
    nt = (((1,), (1,)), ((), ()))

    @pl.when(pl.program_id(1) == 0)
    def _():
        ct_ref[...] = jnp.zeros(ct_ref.shape, F32)
        m_ref[...] = jnp.zeros(m_ref.shape, F32)

    row = lax.broadcasted_iota(jnp.int32, (ln, ln), 0)
    col = lax.broadcasted_iota(jnp.int32, (ln, ln), 1)
    lower = col <= row
    ones_row = (lax.broadcasted_iota(jnp.int32, (ML_STATE_ROWS - ML_DIM, ln), 0) == 0).astype(BF16)

    for r in range(ML_ROWS):
        gr = gr_ref[r]
        c, b = gr[0:nh], gr[nh:2 * nh]
        m_prev = jnp.concatenate([m_ref[r, 0:nh, :]] * (ln // LANES), axis=1)
        m_run = jnp.maximum(gr[2 * nh:3 * nh], m_prev)
        m_last = m_run[:, ln - 1:ln]
        e_inter = jnp.exp(m_prev - m_run)
        e_floor = jnp.exp(-(b + m_run))
        w_in = jnp.exp(c - m_last)
        decay = e_inter[:, ln - 1:ln]
        m_ref[r, 0:nh, :] = jnp.broadcast_to(b[:, ln - 1:ln] + m_last, (nh, LANES))
        m_run_cols = jnp.concatenate([m_run, m_run], axis=0).T

        for h in range(nh):
            lanes = slice(h * ML_DIM, (h + 1) * ML_DIM)
            q = q_ref[r, :, lanes]
            k = k_ref[r, :, lanes]
            vt_aug = jnp.concatenate([vt_ref[r, lanes, :], ones_row], axis=0)
            s = lax.dot_general(q, k, nt, preferred_element_type=F32)
            gate = jnp.exp(jnp.where(lower, c[h:h + 1, :] - m_run_cols[:, h:h + 1], -jnp.inf))
            w = (s * gate).astype(BF16)
            state_t = ct_ref[r, h]
            full_t = (e_inter[h:h + 1, :]
                      * lax.dot_general(state_t.astype(BF16), q, nt, preferred_element_type=F32)
                      + lax.dot_general(vt_aug, w, nt, preferred_element_type=F32))
            den = full_t[ML_DIM:ML_DIM + 1, :]
            hh_t = full_t[0:ML_DIM, :] / jnp.maximum(jnp.abs(den), e_floor[h:h + 1, :])
            hn_t = hh_t * lax.rsqrt(jnp.mean(hh_t * hh_t, axis=0, keepdims=True) + EPS)
            y_ref[r, :, lanes] = (o_ref[r, :, lanes].astype(F32) * (hn_t.T * gout_ref[:, lanes])).astype(BF16)

            vw = (vt_aug.astype(F32) * w_in[h:h + 1, :]).astype(BF16)
            ct_ref[r, h] = decay[h:h + 1, :] * state_t + jnp.dot(vw, k, preferred_element_type=F32)


def _mlstm(q, k, vt, o, gates_row, g_out):
    bsz, s, _ = q.shape
    ln = ML_CHUNK
    rows = ML_ROWS
    tok = pl.BlockSpec((rows, ln, ML_WIDTH), lambda b, c: (b, c, 0))
    return pl.pallas_call(
        _mlstm_kernel,
        grid=(bsz // rows, s // ln),
        in_specs=[tok, tok, pl.BlockSpec((rows, ML_WIDTH, ln), lambda b, c: (b, 0, c)), tok,
                  pl.BlockSpec((rows, GATE_ROWS, ln), lambda b, c: (b, 0, c)),
                  _const_spec((1, ML_WIDTH))],
        out_specs=tok,
        out_shape=jax.ShapeDtypeStruct((bsz, s, ML_WIDTH), BF16),
        scratch_shapes=[pltpu.VMEM((rows, ML_HEADS, ML_STATE_ROWS, ML_DIM), F32),
                        pltpu.VMEM((rows, 8, LANES), F32)],
        compiler_params=pltpu.CompilerParams(
            dimension_semantics=("arbitrary", "arbitrary"), vmem_limit_bytes=VMEM_LIMIT),
        name="mlstm",
    )(q, k, vt, o, gates_row, g_out.reshape(1, ML_WIDTH))


def _rope_tables(s):
    half = ROPE_DIM // 2
    pos = jnp.arange(s, dtype=F32)
    inv_freq = ROPE_THETA ** (-jnp.arange(0, ROPE_DIM, 2, dtype=F32) / ROPE_DIM)
    ang = pos[:, None] * inv_freq[None, :]
    basis = jnp.concatenate([jnp.cos(ang), jnp.sin(ang), jnp.ones((s, 1), F32)], axis=1)
    g = jnp.arange(LANES) % DA_QK_DIM
    j = jnp.arange(2 * half + 1)[:, None]
    cos_sel = jnp.where(g < ROPE_DIM, j == g % half, j == 2 * half).astype(F32)
    sin_up_sel = -((g < half) & (j == half + g)).astype(F32)
    sin_dn_sel = ((g >= half) & (g < ROPE_DIM) & (j == g)).astype(F32)
    sel = jnp.concatenate([cos_sel, sin_up_sel, sin_dn_sel], axis=1)
    return jnp.dot(basis, sel, precision=lax.Precision.HIGHEST)


def kernel(x, c, w_ada, b_ada, g_norm, ffn1_w12, ffn1_w3, w_in, conv_w, conv_b, b_igate, b_fgate,
           g_qnorm, g_knorm, lambda_qk, g_da_out, g_ml_out, w_out, ffn2_w12, ffn2_w3):
    bsz, s, d = x.shape
    l = 0
    mod = _ada(c, w_ada[l], b_ada[l]).reshape(bsz, 3, 3, d)

    def ffn(xin, sub, w12, w3, mix=None):
        return _ffn(xin, mod[:, sub], g_norm[l, sub], w12.astype(BF16), w3.astype(BF16), mix)

    x = ffn(x, 0, ffn1_w12[l], ffn1_w3[l])

    wb = w_in[l].astype(BF16)
    o0 = 0
    parts = []
    for width in (DA_WIDTH, DA_WIDTH, DA_WIDTH, 2 * ML_WIDTH, ML_WIDTH, ML_WIDTH, 2 * ML_HEADS):
        parts.append(wb[:, o0:o0 + width])
        o0 += width
    wq, wk, wv, wmqk, wmv, wmo, wif = parts
    wif = jnp.pad(wif.T, ((0, 2 * ML_HEADS), (0, 0)))
    wmv = wmv.T
    groups = DA_WIDTH // DA_QK_DIM
    q_gain = g_qnorm[l] * (DA_QK_DIM ** -0.5)
    gq = jnp.tile(q_gain * LOG2E, groups).reshape(1, DA_WIDTH)
    score_bound = 1.05 * DA_QK_DIM * jnp.max(jnp.abs(q_gain)) * jnp.max(jnp.abs(g_knorm[l]))
    bounded = (score_bound <= SAFE_SCORE_BOUND).astype(jnp.int32).reshape(1)
    gk = jnp.tile(g_knorm[l], groups).reshape(1, DA_WIDTH)
    rope = _rope_tables(s)
    gid = jnp.arange(MXU_TILE) // DA_QK_DIM
    bd = (gid[:, None] == gid[None, :]).astype(BF16)
    bif = jnp.concatenate([b_igate[l], b_fgate[l]]).reshape(2 * ML_HEADS, 1)
    da_q, da_k, da_v, ml_q, ml_k, ml_v, ml_o, gates = _inproj(
        x, mod[:, 1], g_norm[l, 1], wq, wk, wv, wmqk, wmv, wmo, wif, gq, gk, rope, bd,
        conv_w[l], conv_b[l].reshape(1, 2 * ML_WIDTH), bif)

    y_da = _attention(bounded, da_q, da_k, da_v, lambda_qk[l], g_da_out[l])
    y_ml = _mlstm(ml_q, ml_k, ml_v, ml_o, gates, g_ml_out[l])

    wo = w_out[l].astype(BF16)
    return ffn(x, 2, ffn2_w12[l], ffn2_w3[l], mix=(mod[:, 1], y_da, y_ml, wo[:DA_WIDTH], wo[DA_WIDTH:]))
```

```python
import functools

import jax
import jax.numpy as jnp
from jax import lax
from jax.experimental import pallas as pl
from jax.experimental.pallas import tpu as pltpu

F32 = jnp.float32
BF16 = jnp.bfloat16

D_MODEL = 1024
DA_HEADS = 4
DA_QK_DIM = 64
DA_V_DIM = 2 * DA_QK_DIM
DA_WIDTH = DA_HEADS * DA_V_DIM
ML_HEADS = 4
ML_DIM = 128
ML_WIDTH = ML_HEADS * ML_DIM
ROPE_THETA = 500000.0
ROPE_DIM = DA_QK_DIM // 4
D_FF = 2816
CONV_K = 4
EPS = 1e-6
LAMBDA_INIT = 0.8 - 0.6 * 1.0
LOG2E = 1.4426950408889634

LANES = 128
MXU_TILE = 256
VMEM_LIMIT = 56 * 1024 * 1024

FFN_TM = 1024
FFN_SUB = 256
PROJ_TM = 1024
PROJ_SUB = 256
CONV_COLS = 256
ATT_T = 512
SAFE_SCORE_BOUND = 40.0
ML_CHUNK = 256
ML_ROWS = 4
ML_STATE_ROWS = ML_DIM + 16
GATE_ROWS = 16
CONV_HALO = 8


def _const_spec(shape):
    nd = len(shape)
    return pl.BlockSpec(shape, lambda *_: (0,) * nd, pipeline_mode=pl.Buffered(1))


def _sigmoid(x):
    return 1.0 / (1.0 + jnp.exp(-x))


def _mod_norm(x, g, shift, scale):
    ms = jnp.mean(x * x, axis=-1, keepdims=True)
    return (x * lax.rsqrt(ms + EPS)) * (g * (1.0 + scale)) + shift


def _ada_kernel(c_ref, w_ref, b_ref, o_ref):
    c = c_ref[...]
    cs = (c * _sigmoid(c)).astype(BF16)
    o_ref[...] = jnp.dot(cs, w_ref[...].astype(BF16), preferred_element_type=F32) + b_ref[...]


def _ada(c, w_ada, b_ada):
    bsz, d = c.shape
    n = w_ada.shape[1]
    tn = 1024
    return pl.pallas_call(
        _ada_kernel,
        grid=(n // tn,),
        in_specs=[pl.BlockSpec((bsz, d), lambda j: (0, 0)),
                  pl.BlockSpec((d, tn), lambda j: (0, j)),
                  pl.BlockSpec((1, tn), lambda j: (0, j))],
        out_specs=pl.BlockSpec((bsz, tn), lambda j: (0, j)),
        out_shape=jax.ShapeDtypeStruct((bsz, n), F32),
        compiler_params=pltpu.CompilerParams(dimension_semantics=("arbitrary",)),
        name="adaln_mod",
    )(c, w_ada, b_ada.reshape(1, n))


def _ffn_body(x, mod_ref, g_ref, w12_ref, w3_ref, o_ref, act_ref):
    mod = mod_ref[0]
    hb = _mod_norm(x, g_ref[...], mod[0:1], mod[1:2]).astype(BF16)
    for c in range(D_FF // FFN_SUB):
        cols = slice(c * FFN_SUB, (c + 1) * FFN_SUB)
        a = jnp.dot(hb, w12_ref[:, cols], preferred_element_type=F32)
        b = jnp.dot(hb, w12_ref[:, D_FF + c * FFN_SUB:D_FF + (c + 1) * FFN_SUB], preferred_element_type=F32)
        act_ref[:, cols] = (a * _sigmoid(a) * b).astype(BF16)
    y = jnp.dot(act_ref[...], w3_ref[...], preferred_element_type=F32)
    o_ref[0] = x + (0.5 * (1.0 + mod[2:3])) * y


def _ffn_kernel(x_ref, mod_ref, g_ref, w12_ref, w3_ref, *rest):
    n = (len(rest) - 2) // 2
    casts_in, o_ref, casts_out, act_ref = rest[:n], rest[n], rest[n + 1:2 * n + 1], rest[-1]
    _ffn_body(x_ref[0], mod_ref, g_ref, w12_ref, w3_ref, o_ref, act_ref)
    for src, dst in zip(casts_in, casts_out):
        dst[...] = src[...].astype(BF16)


def _mix_ffn_kernel(x_ref, mmod_ref, ya_ref, ym_ref, wa_ref, wm_ref,
                    mod_ref, g_ref, w12_ref, w3_ref, o_ref, act_ref):
    y = (jnp.dot(ya_ref[0], wa_ref[...], preferred_element_type=F32)
         + jnp.dot(ym_ref[0], wm_ref[...], preferred_element_type=F32))
    x = x_ref[0] + (1.0 + mmod_ref[0][2:3]) * y
    _ffn_body(x, mod_ref, g_ref, w12_ref, w3_ref, o_ref, act_ref)


def _ffn(x, mod3, g, w12, w3, mix=None, casts=()):
    bsz, s, d = x.shape
    tm = FFN_TM
    steps = bsz * (s // tm)
    tok_spec = lambda width: pl.BlockSpec((1, tm, width), lambda b, i: (b, i, 0))
    mod_spec = pl.BlockSpec((1, 3, d), lambda b, i: (b, 0, 0))
    ffn_specs = [mod_spec, _const_spec((1, d)),
                 _const_spec(w12.shape), _const_spec(w3.shape)]
    ffn_args = (mod3, g.reshape(1, d), w12, w3)
    slab = lambda w: pl.BlockSpec((w.shape[0] // steps, w.shape[1]), lambda b, i: (b * (s // tm) + i, 0))
    out_specs, out_shape = tok_spec(d), jax.ShapeDtypeStruct(x.shape, F32)
    if mix is None:
        body = _ffn_kernel
        in_specs = [tok_spec(d)] + ffn_specs + [slab(w) for w in casts]
        args = (x,) + ffn_args + tuple(casts)
        if casts:
            out_specs = [out_specs] + [slab(w) for w in casts]
            out_shape = [out_shape] + [jax.ShapeDtypeStruct(w.shape, BF16) for w in casts]
    else:
        mmod3, y_da, y_ml, w_a, w_m = mix
        body = _mix_ffn_kernel
        in_specs = [tok_spec(d), mod_spec, tok_spec(DA_WIDTH), tok_spec(ML_WIDTH),
                    _const_spec(w_a.shape), _const_spec(w_m.shape)] + ffn_specs
        args = (x, mmod3, y_da, y_ml, w_a, w_m) + ffn_args
    return pl.pallas_call(
        body,
        grid=(bsz, s // tm),
        in_specs=in_specs,
        out_specs=out_specs,
        out_shape=out_shape,
        scratch_shapes=[pltpu.VMEM((tm, D_FF), BF16)],
        compiler_params=pltpu.CompilerParams(
            dimension_semantics=("arbitrary", "arbitrary"), vmem_limit_bytes=VMEM_LIMIT),
        name="ffn" if mix is None else "mix_ffn",
    )(*args)


def _group_norm_rope(u, gvec, bd, cos, sina, sinb):
    x2 = u * u
    ssq = jnp.concatenate(
        [jnp.dot(x2[:, j:j + MXU_TILE].astype(BF16), bd, preferred_element_type=F32)
         for j in range(0, u.shape[1], MXU_TILE)], axis=1)
    xn = (u * lax.rsqrt(ssq * (1.0 / DA_QK_DIM) + EPS)) * gvec
    outs = []
    for h in range(u.shape[1] // LANES):
        xh = xn[:, h * LANES:(h + 1) * LANES]
        up = pltpu.roll(xh, LANES - ROPE_DIM // 2, 1)
        dn = pltpu.roll(xh, ROPE_DIM // 2, 1)
        outs.append(xh * cos + up * sina + dn * sinb)
    return jnp.concatenate(outs, axis=1)


def _chunk_scan(x, op, identity):
    pos = lax.broadcasted_iota(jnp.int32, x.shape, 1) & (ML_CHUNK - 1)
    d = 1
    while d < ML_CHUNK:
        x = op(x, jnp.where(pos >= d, pltpu.roll(x, d, 1), identity))
        d *= 2
    return x


def _inproj_kernel(x_ref, mod_ref, g_ref, wq_ref, wk_ref, wv_ref, wmqk_ref, wmv_ref, wmo_ref, wif_ref,
                   gq_ref, gk_ref, rope_ref, bd_ref, cw_ref, cb_ref, bif_ref,
                   q_out, k_out, v_out, mq_out, mk_out, mv_out, mo_out, gate_out, halo_ref):
    tm = x_ref.shape[1]
    si = pl.program_id(1)
    mod = mod_ref[0]
    nh = ML_HEADS
    nt = (((1,), (1,)), ((), ()))
    bd = bd_ref[...]
    cw = cw_ref[...]

    tails = jnp.where(si == 0, 0.0, halo_ref[...])
    tails = [tails[:, ci * CONV_COLS:(ci + 1) * CONV_COLS] for ci in range(2 * ML_WIDTH // CONV_COLS)]

    for r0 in range(0, tm, PROJ_SUB):
        rows = slice(r0, r0 + PROJ_SUB)
        hb = _mod_norm(x_ref[0, rows, :], g_ref[...], mod[0:1], mod[1:2]).astype(BF16)

        pre = lax.dot_general(wif_ref[...], hb, nt, preferred_element_type=F32)[0:2 * nh] + bif_ref[...]
        zf = pre[nh:2 * nh]
        log_f = jnp.minimum(zf, 0.0) - jnp.log1p(jnp.exp(-jnp.abs(zf)))
        b = _chunk_scan(log_f, jnp.add, 0.0)
        c = pre[0:nh] - b
        gate_out[0, :, rows] = jnp.concatenate(
            [c, b, _chunk_scan(c, jnp.maximum, -jnp.inf), jnp.zeros_like(c)], axis=0)

        cos, sina, sinb = (rope_ref[rows, j * LANES:(j + 1) * LANES] for j in range(3))

        def conv_chunk(ci):
            cols = slice(ci * CONV_COLS, (ci + 1) * CONV_COLS)
            u = jnp.dot(hb, wmqk_ref[:, cols], preferred_element_type=F32)
            ext = jnp.concatenate([tails[ci], u], axis=0)
            tails[ci] = u[PROJ_SUB - CONV_HALO:, :]
            acc = cb_ref[:, cols] + u * cw[CONV_K - 1:CONV_K, cols]
            for d in range(1, CONV_K):
                acc = acc + pltpu.roll(ext, d, 0)[CONV_HALO:, :] * cw[CONV_K - 1 - d:CONV_K - d, cols]
            act = acc * _sigmoid(acc)
            if ci < ML_WIDTH // CONV_COLS:
                mq_out[0, rows, cols] = (act * (ML_DIM ** -0.5)).astype(BF16)
            else:
                mk_out[0, rows, ci * CONV_COLS - ML_WIDTH:(ci + 1) * CONV_COLS - ML_WIDTH] = act.astype(BF16)

        chunks = iter(range(2 * ML_WIDTH // CONV_COLS))
        per_slot = 2 * ML_WIDTH // CONV_COLS // 4
        uq = jnp.dot(hb, wq_ref[...], preferred_element_type=F32)
        for _ in range(per_slot):
            conv_chunk(next(chunks))
        q_out[0, rows, :] = _group_norm_rope(uq, gq_ref[...], bd, cos, sina, sinb).astype(BF16)
        uk = jnp.dot(hb, wk_ref[...], preferred_element_type=F32)
        for _ in range(per_slot):
            conv_chunk(next(chunks))
        k_out[0, rows, :] = _group_norm_rope(uk, gk_ref[...], bd, cos, sina, sinb).astype(BF16)
        uv = jnp.dot(hb, wv_ref[...], preferred_element_type=F32)
        umv_t = lax.dot_general(wmv_ref[...], hb, nt, preferred_element_type=F32)
        for _ in range(per_slot):
            conv_chunk(next(chunks))
        v_out[0, rows, :] = uv.astype(BF16)
        mv_out[0, :, rows] = umv_t.astype(BF16)
        umo = jnp.dot(hb, wmo_ref[...], preferred_element_type=F32)
        for _ in range(per_slot):
            conv_chunk(next(chunks))
        mo_out[0, rows, :] = _sigmoid(umo).astype(BF16)

    halo_ref[...] = jnp.concatenate(tails, axis=1)


def _inproj(x, mod3, g, wq, wk, wv, wmqk, wmv, wmo, wif, gq, gk, rope, bd, cw, cb, bif):
    bsz, s, d = x.shape
    tm = PROJ_TM
    tok = lambda width, dt: jax.ShapeDtypeStruct((bsz, s, width), dt)
    tok_spec = lambda width: pl.BlockSpec((1, tm, width), lambda b, i: (b, i, 0))
    return pl.pallas_call(
        _inproj_kernel,
        grid=(bsz, s // tm),
        in_specs=[tok_spec(d),
                  pl.BlockSpec((1, 3, d), lambda b, i: (b, 0, 0)),
                  _const_spec((1, d)),
                  _const_spec(wq.shape), _const_spec(wk.shape), _const_spec(wv.shape),
                  _const_spec(wmqk.shape), _const_spec(wmv.shape), _const_spec(wmo.shape),
                  _const_spec(wif.shape),
                  _const_spec(gq.shape), _const_spec(gk.shape),
                  pl.BlockSpec((tm, 3 * LANES), lambda b, i: (i, 0)),
                  _const_spec(bd.shape), _const_spec(cw.shape), _const_spec(cb.shape),
                  _const_spec(bif.shape)],
        out_specs=[tok_spec(DA_WIDTH), tok_spec(DA_WIDTH), tok_spec(DA_WIDTH),
                   tok_spec(ML_WIDTH), tok_spec(ML_WIDTH),
                   pl.BlockSpec((1, ML_WIDTH, tm), lambda b, i: (b, 0, i)),
                   tok_spec(ML_WIDTH),
                   pl.BlockSpec((1, GATE_ROWS, tm), lambda b, i: (b, 0, i))],
        out_shape=[tok(DA_WIDTH, BF16), tok(DA_WIDTH, BF16), tok(DA_WIDTH, BF16),
                   tok(ML_WIDTH, BF16), tok(ML_WIDTH, BF16),
                   jax.ShapeDtypeStruct((bsz, ML_WIDTH, s), BF16),
                   tok(ML_WIDTH, BF16),
                   jax.ShapeDtypeStruct((bsz, GATE_ROWS, s), F32)],
        scratch_shapes=[pltpu.VMEM((CONV_HALO, 2 * ML_WIDTH), F32)],
        compiler_params=pltpu.CompilerParams(
            dimension_semantics=("arbitrary", "arbitrary"), vmem_limit_bytes=VMEM_LIMIT),
        name="in_proj",
    )(x, mod3, g.reshape(1, d), wq, wk, wv, wmqk, wmv, wmo, wif, gq, gk, rope, bd, cw, cb, bif)


def _attn_kernel(bounded_ref, q_ref, k_ref, v_ref, lam_ref, gout_ref, o_ref, qs_ref, m_ref, acc_ref):
    t = ATT_T
    u = t // 2
    lane = lax.broadcasted_iota(jnp.int32, (u, DA_V_DIM), 1)
    ones_cols = jnp.ones((t, DA_V_DIM), BF16)
    lv = lam_ref[...]
    lam = (jnp.exp(jnp.sum(lv[0:1] * lv[1:2], axis=1, keepdims=True))
           - jnp.exp(jnp.sum(lv[2:3] * lv[3:4], axis=1, keepdims=True)) + LAMBDA_INIT)

    def query_rows(qi, half, part):
        return pl.ds(pl.multiple_of(qi * (2 * t) + half * t + part * u, u), u)

    def stack_queries(qi):
        for half in range(2):
            for part in range(2):
                q = q_ref[0, query_rows(qi, half, part), :]
                zero = jnp.zeros_like(q)
                qs_ref[half, (2 * part) * u:(2 * part + 1) * u, :] = jnp.where(lane < DA_QK_DIM, q, zero)
                qs_ref[half, (2 * part + 1) * u:(2 * part + 2) * u, :] = jnp.where(lane >= DA_QK_DIM, q, zero)

    def keys(kb):
        start = pl.multiple_of(kb * t, t)
        return k_ref[0, pl.ds(start, t), :], jnp.concatenate([v_ref[0, pl.ds(start, t), :], ones_cols], axis=1)

    def piece(half, rows, ks, vs, causal_shift, first, stabilised):
        s = lax.dot_general(qs_ref[half, rows, :], ks, (((1,), (1,)), ((), ())), preferred_element_type=F32)
        if causal_shift is not None:
            row = lax.broadcasted_iota(jnp.int32, s.shape, 0) & (u - 1)
            col = lax.broadcasted_iota(jnp.int32, s.shape, 1)
            s = jnp.where(col <= row + causal_shift, s, -jnp.inf)
        if stabilised:
            m_new = jnp.max(s, axis=1, keepdims=True)
            if not first:
                m_prev = m_ref[half, rows, :]
                m_new = jnp.maximum(m_prev, m_new)
                alpha = jnp.exp2(m_prev - m_new)
            m_ref[half, rows, :] = m_new
            s = s - m_new
        pv = jnp.dot(jnp.exp2(s).astype(BF16), vs, preferred_element_type=F32)
        if first:
            acc_ref[half, rows, :] = pv
        elif stabilised:
            acc_ref[half, rows, :] = alpha * acc_ref[half, rows, :] + pv
        else:
            acc_ref[half, rows, :] += pv

    def block(half, kv, diagonal, first, stabilised):
        ks, vs = kv
        if diagonal:
            piece(half, slice(0, 2 * u), ks[0:u], vs[0:u], 0, first, stabilised)
            piece(half, slice(2 * u, 4 * u), ks, vs, u, first, stabilised)
        else:
            piece(half, slice(0, 4 * u), ks, vs, None, first, stabilised)

    def tile_pair(qi, stabilised):
        stack_queries(qi)
        kv = keys(2 * qi)
        block(0, kv, True, True, stabilised)
        block(1, kv, False, True, stabilised)
        block(1, keys(2 * qi + 1), True, False, stabilised)

        def body(pair, carry):
            for j in range(2):
                kv = keys(2 * pair + j)
                block(0, kv, False, False, stabilised)
                block(1, kv, False, False, stabilised)
            return carry

        lax.fori_loop(0, qi, body, 0)

        for half in range(2):
            for part in range(2):
                rows = slice(2 * part * u, (2 * part + 2) * u)
                o = acc_ref[half, rows, 0:DA_V_DIM] / acc_ref[half, rows, DA_V_DIM:2 * DA_V_DIM]
                od = o[0:u] - lam * o[u:2 * u]
                ms = jnp.mean(od * od, axis=-1, keepdims=True)
                o_ref[0, query_rows(qi, half, part), :] = (
                    (od * lax.rsqrt(ms + EPS)) * gout_ref[...] * (1.0 - LAMBDA_INIT)).astype(BF16)

    def run(stabilised):
        def body(qi, carry):
            tile_pair(qi, stabilised)
            return carry

        lax.fori_loop(0, q_ref.shape[1] // (2 * t), body, 0)

    @pl.when(bounded_ref[0] != 0)
    def _():
        run(False)

    @pl.when(bounded_ref[0] == 0)
    def _():
        run(True)


def _attention(bounded, q, k, v, lam_vecs, g_out):
    bsz, s, _ = q.shape
    t = ATT_T
    full = pl.BlockSpec((1, s, DA_V_DIM), lambda b, h: (b, 0, h))
    return pl.pallas_call(
        _attn_kernel,
        grid=(bsz, DA_HEADS),
        in_specs=[pl.BlockSpec(memory_space=pltpu.SMEM),
                  full, full, full, _const_spec(lam_vecs.shape), _const_spec((1, DA_V_DIM))],
        out_specs=full,
        out_shape=jax.ShapeDtypeStruct((bsz, s, DA_WIDTH), BF16),
        scratch_shapes=[pltpu.VMEM((2, 2 * t, DA_V_DIM), BF16),
                        pltpu.VMEM((2, 2 * t, 1), F32),
                        pltpu.VMEM((2, 2 * t, 2 * DA_V_DIM), F32)],
        compiler_params=pltpu.CompilerParams(
            dimension_semantics=("arbitrary", "arbitrary"), vmem_limit_bytes=VMEM_LIMIT),
        name="diff_attention",
    )(bounded, q, k, v, lam_vecs, g_out.reshape(1, DA_V_DIM))


def _mlstm_kernel(q_ref, k_ref, vt_ref, o_ref, gr_ref, gout_ref, y_ref, ct_ref, m_ref):
    ln = ML_CHUNK
    nh = ML_HEADS
    nt = (((1,), (1,)), ((), ()))

    @pl.when(pl.program_id(1) == 0)
    def _():
        ct_ref[...] = jnp.zeros(ct_ref.shape, F32)
        m_ref[...] = jnp.zeros(m_ref.shape, F32)

    row = lax.broadcasted_iota(jnp.int32, (ln, ln), 0)
    col = lax.broadcasted_iota(jnp.int32, (ln, ln), 1)
    lower = col <= row
    ones_row = (lax.broadcasted_iota(jnp.int32, (ML_STATE_ROWS - ML_DIM, ln), 0) == 0).astype(BF16)

    for r in range(ML_ROWS):
        gr = gr_ref[r]
        c, b = gr[0:nh], gr[nh:2 * nh]
        m_prev = jnp.concatenate([m_ref[r, 0:nh, :]] * (ln // LANES), axis=1)
        m_run = jnp.maximum(gr[2 * nh:3 * nh], m_prev)
        m_last = m_run[:, ln - 1:ln]
        e_inter = jnp.exp(m_prev - m_run)
        e_floor = jnp.exp(-(b + m_run))
        w_in = jnp.exp(c - m_last)
        decay = e_inter[:, ln - 1:ln]
        m_ref[r, 0:nh, :] = jnp.broadcast_to(b[:, ln - 1:ln] + m_last, (nh, LANES))
        m_run_cols = jnp.concatenate([m_run, m_run], axis=0).T

        for h in range(nh):
            lanes = slice(h * ML_DIM, (h + 1) * ML_DIM)
            q = q_ref[r, :, lanes]
            k = k_ref[r, :, lanes]
            vt_aug = jnp.concatenate([vt_ref[r, lanes, :], ones_row], axis=0)
            s = lax.dot_general(q, k, nt, preferred_element_type=F32)
            gate = jnp.exp(jnp.where(lower, c[h:h + 1, :] - m_run_cols[:, h:h + 1], -jnp.inf))
            w = (s * gate).astype(BF16)
            state_t = ct_ref[r, h]
            full_t = (e_inter[h:h + 1, :]
                      * lax.dot_general(state_t.astype(BF16), q, nt, preferred_element_type=F32)
                      + lax.dot_general(vt_aug, w, nt, preferred_element_type=F32))
            den = full_t[ML_DIM:ML_DIM + 1, :]
            hh_t = full_t[0:ML_DIM, :] / jnp.maximum(jnp.abs(den), e_floor[h:h + 1, :])
            hn_t = hh_t * lax.rsqrt(jnp.mean(hh_t * hh_t, axis=0, keepdims=True) + EPS)
            y_ref[r, :, lanes] = (o_ref[r, :, lanes].astype(F32) * (hn_t.T * gout_ref[:, lanes])).astype(BF16)

            vw = (vt_aug.astype(F32) * w_in[h:h + 1, :]).astype(BF16)
            ct_ref[r, h] = decay[h:h + 1, :] * state_t + jnp.dot(vw, k, preferred_element_type=F32)


def _mlstm(q, k, vt, o, gates_row, g_out):
    bsz, s, _ = q.shape
    ln = ML_CHUNK
    rows = ML_ROWS
    tok = pl.BlockSpec((rows, ln, ML_WIDTH), lambda b, c: (b, c, 0))
    return pl.pallas_call(
        _mlstm_kernel,
        grid=(bsz // rows, s // ln),
        in_specs=[tok, tok, pl.BlockSpec((rows, ML_WIDTH, ln), lambda b, c: (b, 0, c)), tok,
                  pl.BlockSpec((rows, GATE_ROWS, ln), lambda b, c: (b, 0, c)),
                  _const_spec((1, ML_WIDTH))],
        out_specs=tok,
        out_shape=jax.ShapeDtypeStruct((bsz, s, ML_WIDTH), BF16),
        scratch_shapes=[pltpu.VMEM((rows, ML_HEADS, ML_STATE_ROWS, ML_DIM), F32),
                        pltpu.VMEM((rows, 8, LANES), F32)],
        compiler_params=pltpu.CompilerParams(
            dimension_semantics=("arbitrary", "arbitrary"), vmem_limit_bytes=VMEM_LIMIT),
        name="mlstm",
    )(q, k, vt, o, gates_row, g_out.reshape(1, ML_WIDTH))


def _rope_tables(s):
    half = ROPE_DIM // 2
    pos = jnp.arange(s, dtype=F32)
    inv_freq = ROPE_THETA ** (-jnp.arange(0, ROPE_DIM, 2, dtype=F32) / ROPE_DIM)
    ang = pos[:, None] * inv_freq[None, :]
    basis = jnp.concatenate([jnp.cos(ang), jnp.sin(ang), jnp.ones((s, 1), F32)], axis=1)
    g = jnp.arange(LANES) % DA_QK_DIM
    j = jnp.arange(2 * half + 1)[:, None]
    cos_sel = jnp.where(g < ROPE_DIM, j == g % half, j == 2 * half).astype(F32)
    sin_up_sel = -((g < half) & (j == half + g)).astype(F32)
    sin_dn_sel = ((g >= half) & (g < ROPE_DIM) & (j == g)).astype(F32)
    sel = jnp.concatenate([cos_sel, sin_up_sel, sin_dn_sel], axis=1)
    return jnp.dot(basis, sel, precision=lax.Precision.HIGHEST)


def kernel(x, c, w_ada, b_ada, g_norm, ffn1_w12, ffn1_w3, w_in, conv_w, conv_b, b_igate, b_fgate,
           g_qnorm, g_knorm, lambda_qk, g_da_out, g_ml_out, w_out, ffn2_w12, ffn2_w3):
    bsz, s, d = x.shape
    l = 0
    mod = _ada(c, w_ada[l], b_ada[l]).reshape(bsz, 3, 3, d)

    x, wb, w12_2, w3_2, wo = _ffn(
        x, mod[:, 0], g_norm[l, 0], ffn1_w12[l].astype(BF16), ffn1_w3[l].astype(BF16),
        casts=(w_in[l], ffn2_w12[l], ffn2_w3[l].reshape(d, D_FF), w_out[l]))
    w3_2 = w3_2.reshape(D_FF, d)
    o0 = 0
    parts = []
    for width in (DA_WIDTH, DA_WIDTH, DA_WIDTH, 2 * ML_WIDTH, ML_WIDTH, ML_WIDTH, 2 * ML_HEADS):
        parts.append(wb[:, o0:o0 + width])
        o0 += width
    wq, wk, wv, wmqk, wmv, wmo, wif = parts
    wif = jnp.pad(wif.T, ((0, 2 * ML_HEADS), (0, 0)))
    wmv = wmv.T
    groups = DA_WIDTH // DA_QK_DIM
    q_gain = g_qnorm[l] * (DA_QK_DIM ** -0.5)
    gq = jnp.tile(q_gain * LOG2E, groups).reshape(1, DA_WIDTH)
    score_bound = 1.05 * DA_QK_DIM * jnp.max(jnp.abs(q_gain)) * jnp.max(jnp.abs(g_knorm[l]))
    bounded = (score_bound <= SAFE_SCORE_BOUND).astype(jnp.int32).reshape(1)
    gk = jnp.tile(g_knorm[l], groups).reshape(1, DA_WIDTH)
    rope = _rope_tables(s)
    gid = jnp.arange(MXU_TILE) // DA_QK_DIM
    bd = (gid[:, None] == gid[None, :]).astype(BF16)
    bif = jnp.concatenate([b_igate[l], b_fgate[l]]).reshape(2 * ML_HEADS, 1)
    da_q, da_k, da_v, ml_q, ml_k, ml_v, ml_o, gates = _inproj(
        x, mod[:, 1], g_norm[l, 1], wq, wk, wv, wmqk, wmv, wmo, wif, gq, gk, rope, bd,
        conv_w[l], conv_b[l].reshape(1, 2 * ML_WIDTH), bif)

    y_da = _attention(bounded, da_q, da_k, da_v, lambda_qk[l], g_da_out[l])
    y_ml = _mlstm(ml_q, ml_k, ml_v, ml_o, gates, g_ml_out[l])

    return _ffn(x, mod[:, 2], g_norm[l, 2], w12_2, w3_2,
                mix=(mod[:, 1], y_da, y_ml, wo[:DA_WIDTH], wo[DA_WIDTH:]))
```

```python
import functools

import jax
import jax.numpy as jnp
from jax import lax
from jax.experimental import pallas as pl
from jax.experimental.pallas import tpu as pltpu

F32 = jnp.float32
BF16 = jnp.bfloat16

D_MODEL = 1024
DA_HEADS = 4
DA_QK_DIM = 64
DA_V_DIM = 2 * DA_QK_DIM
DA_WIDTH = DA_HEADS * DA_V_DIM
ML_HEADS = 4
ML_DIM = 128
ML_WIDTH = ML_HEADS * ML_DIM
ROPE_THETA = 500000.0
ROPE_DIM = DA_QK_DIM // 4
D_FF = 2816
CONV_K = 4
EPS = 1e-6
LAMBDA_INIT = 0.8 - 0.6 * 1.0
LOG2E = 1.4426950408889634

LANES = 128
BF16_SUBLANES = 16
MXU_TILE = 256
VMEM_LIMIT = 56 * 1024 * 1024

FFN_TM = 1024
FFN_SUB = 256
PROJ_TM = 1024
PROJ_SUB = 256
CONV_COLS = 256
ATT_T = 512
SAFE_SCORE_BOUND = 40.0
ML_CHUNK = 256
ML_ROWS = 4
ML_STATE_ROWS = ML_DIM + 16
GATE_ROWS = 16
CONV_HALO = 8


def _const_spec(shape):
    nd = len(shape)
    return pl.BlockSpec(shape, lambda *_: (0,) * nd, pipeline_mode=pl.Buffered(1))


def _sigmoid(x):
    return 1.0 / (1.0 + jnp.exp(-x))


def _mod_norm(x, g, shift, scale):
    ms = jnp.mean(x * x, axis=-1, keepdims=True)
    return (x * lax.rsqrt(ms + EPS)) * (g * (1.0 + scale)) + shift


def _ada_kernel(c_ref, w_ref, b_ref, o_ref):
    c = c_ref[...]
    cs = (c * _sigmoid(c)).astype(BF16)
    o_ref[...] = jnp.dot(cs, w_ref[...].astype(BF16), preferred_element_type=F32) + b_ref[...]


def _ada(c, w_ada, b_ada):
    bsz, d = c.shape
    n = w_ada.shape[1]
    tn = 1024
    return pl.pallas_call(
        _ada_kernel,
        grid=(n // tn,),
        in_specs=[pl.BlockSpec((bsz, d), lambda j: (0, 0)),
                  pl.BlockSpec((d, tn), lambda j: (0, j)),
                  pl.BlockSpec((1, tn), lambda j: (0, j))],
        out_specs=pl.BlockSpec((bsz, tn), lambda j: (0, j)),
        out_shape=jax.ShapeDtypeStruct((bsz, n), F32),
        compiler_params=pltpu.CompilerParams(dimension_semantics=("arbitrary",)),
        name="adaln_mod",
    )(c, w_ada, b_ada.reshape(1, n))


def _ffn_body(x, mod_ref, g_ref, w12_ref, w3_ref, o_ref, act_ref):
    mod = mod_ref[0]
    hb = _mod_norm(x, g_ref[...], mod[0:1], mod[1:2]).astype(BF16)
    for c in range(D_FF // FFN_SUB):
        cols = slice(c * FFN_SUB, (c + 1) * FFN_SUB)
        a = jnp.dot(hb, w12_ref[:, cols], preferred_element_type=F32)
        b = jnp.dot(hb, w12_ref[:, D_FF + c * FFN_SUB:D_FF + (c + 1) * FFN_SUB], preferred_element_type=F32)
        act_ref[:, cols] = (a * _sigmoid(a) * b).astype(BF16)
    y = jnp.dot(act_ref[...], w3_ref[...], preferred_element_type=F32)
    o_ref[0] = x + (0.5 * (1.0 + mod[2:3])) * y


def _ffn_kernel(x_ref, mod_ref, g_ref, w12_ref, w3_ref, *rest):
    n = (len(rest) - 2) // 2
    casts_in, o_ref, casts_out, act_ref = rest[:n], rest[n], rest[n + 1:2 * n + 1], rest[-1]
    _ffn_body(x_ref[0], mod_ref, g_ref, w12_ref, w3_ref, o_ref, act_ref)
    for src, dst in zip(casts_in, casts_out):
        dst[...] = src[...].astype(BF16)


def _mix_ffn_kernel(x_ref, mmod_ref, ya_ref, ym_ref, wa_ref, wm_ref,
                    mod_ref, g_ref, w12_ref, w3_ref, o_ref, act_ref):
    y = (jnp.dot(ya_ref[0], wa_ref[...], preferred_element_type=F32)
         + jnp.dot(ym_ref[0], wm_ref[...], preferred_element_type=F32))
    x = x_ref[0] + (1.0 + mmod_ref[0][2:3]) * y
    _ffn_body(x, mod_ref, g_ref, w12_ref, w3_ref, o_ref, act_ref)


def _ffn(x, mod3, g, w12, w3, mix=None, casts=()):
    bsz, s, d = x.shape
    tm = FFN_TM
    steps = bsz * (s // tm)
    tok_spec = lambda width: pl.BlockSpec((1, tm, width), lambda b, i: (b, i, 0))
    mod_spec = pl.BlockSpec((1, 3, d), lambda b, i: (b, 0, 0))
    ffn_specs = [mod_spec, _const_spec((1, d)),
                 _const_spec(w12.shape), _const_spec(w3.shape)]
    ffn_args = (mod3, g.reshape(1, d), w12, w3)
    def slab(w, layer=None):
        rows, cols = w.shape[-2:]
        lead = () if layer is None else (layer,)
        if rows % (steps * BF16_SUBLANES) == 0:
            return pl.BlockSpec((None,) * len(lead) + (rows // steps, cols),
                                lambda b, i: lead + (b * (s // tm) + i, 0))
        return pl.BlockSpec((None,) * len(lead) + (rows // bsz, cols), lambda b, i: lead + (b, 0))

    out_specs, out_shape = tok_spec(d), jax.ShapeDtypeStruct(x.shape, F32)
    if mix is None:
        body = _ffn_kernel
        in_specs = [tok_spec(d)] + ffn_specs + [slab(w, layer) for w, layer in casts]
        args = (x,) + ffn_args + tuple(w for w, _ in casts)
        if casts:
            cast_shapes = [jax.ShapeDtypeStruct(w.shape[-2:], BF16) for w, _ in casts]
            out_specs = [out_specs] + [slab(c) for c in cast_shapes]
            out_shape = [out_shape] + cast_shapes
    else:
        mmod3, y_da, y_ml, w_a, w_m = mix
        body = _mix_ffn_kernel
        in_specs = [tok_spec(d), mod_spec, tok_spec(DA_WIDTH), tok_spec(ML_WIDTH),
                    _const_spec(w_a.shape), _const_spec(w_m.shape)] + ffn_specs
        args = (x, mmod3, y_da, y_ml, w_a, w_m) + ffn_args
    return pl.pallas_call(
        body,
        grid=(bsz, s // tm),
        in_specs=in_specs,
        out_specs=out_specs,
        out_shape=out_shape,
        scratch_shapes=[pltpu.VMEM((tm, D_FF), BF16)],
        compiler_params=pltpu.CompilerParams(
            dimension_semantics=("arbitrary", "arbitrary"), vmem_limit_bytes=VMEM_LIMIT),
        name="ffn" if mix is None else "mix_ffn",
    )(*args)


def _group_norm_rope(u, gvec, bd, cos, sina, sinb):
    x2 = u * u
    ssq = jnp.concatenate(
        [jnp.dot(x2[:, j:j + MXU_TILE].astype(BF16), bd, preferred_element_type=F32)
         for j in range(0, u.shape[1], MXU_TILE)], axis=1)
    xn = (u * lax.rsqrt(ssq * (1.0 / DA_QK_DIM) + EPS)) * gvec
    outs = []
    for h in range(u.shape[1] // LANES):
        xh = xn[:, h * LANES:(h + 1) * LANES]
        up = pltpu.roll(xh, LANES - ROPE_DIM // 2, 1)
        dn = pltpu.roll(xh, ROPE_DIM // 2, 1)
        outs.append(xh * cos + up * sina + dn * sinb)
    return jnp.concatenate(outs, axis=1)


def _chunk_scan(x, op, identity):
    pos = lax.broadcasted_iota(jnp.int32, x.shape, 1) & (ML_CHUNK - 1)
    d = 1
    while d < ML_CHUNK:
        x = op(x, jnp.where(pos >= d, pltpu.roll(x, d, 1), identity))
        d *= 2
    return x


def _inproj_kernel(x_ref, mod_ref, g_ref, wq_ref, wk_ref, wv_ref, wmqk_ref, wmv_ref, wmo_ref, wif_ref,
                   gq_ref, gk_ref, rope_ref, bd_ref, cw_ref, cb_ref, bif_ref,
                   q_out, k_out, v_out, mq_out, mk_out, mv_out, mo_out, gate_out, halo_ref):
    tm = x_ref.shape[1]
    si = pl.program_id(1)
    mod = mod_ref[0]
    nh = ML_HEADS
    nt = (((1,), (1,)), ((), ()))
    bd = bd_ref[...]
    cw = cw_ref[...]

    tails = jnp.where(si == 0, 0.0, halo_ref[...])
    tails = [tails[:, ci * CONV_COLS:(ci + 1) * CONV_COLS] for ci in range(2 * ML_WIDTH // CONV_COLS)]

    for r0 in range(0, tm, PROJ_SUB):
        rows = slice(r0, r0 + PROJ_SUB)
        hb = _mod_norm(x_ref[0, rows, :], g_ref[...], mod[0:1], mod[1:2]).astype(BF16)

        pre = lax.dot_general(wif_ref[...], hb, nt, preferred_element_type=F32)[0:2 * nh] + bif_ref[...]
        zf = pre[nh:2 * nh]
        log_f = jnp.minimum(zf, 0.0) - jnp.log1p(jnp.exp(-jnp.abs(zf)))
        b = _chunk_scan(log_f, jnp.add, 0.0)
        c = pre[0:nh] - b
        gate_out[0, :, rows] = jnp.concatenate(
            [c, b, _chunk_scan(c, jnp.maximum, -jnp.inf), jnp.zeros_like(c)], axis=0)

        cos, sina, sinb = (rope_ref[rows, j * LANES:(j + 1) * LANES] for j in range(3))

        def conv_chunk(ci):
            cols = slice(ci * CONV_COLS, (ci + 1) * CONV_COLS)
            u = jnp.dot(hb, wmqk_ref[:, cols], preferred_element_type=F32)
            ext = jnp.concatenate([tails[ci], u], axis=0)
            tails[ci] = u[PROJ_SUB - CONV_HALO:, :]
            acc = cb_ref[:, cols] + u * cw[CONV_K - 1:CONV_K, cols]
            for d in range(1, CONV_K):
                acc = acc + pltpu.roll(ext, d, 0)[CONV_HALO:, :] * cw[CONV_K - 1 - d:CONV_K - d, cols]
            act = acc * _sigmoid(acc)
            if ci < ML_WIDTH // CONV_COLS:
                mq_out[0, rows, cols] = (act * (ML_DIM ** -0.5)).astype(BF16)
            else:
                mk_out[0, rows, ci * CONV_COLS - ML_WIDTH:(ci + 1) * CONV_COLS - ML_WIDTH] = act.astype(BF16)

        chunks = iter(range(2 * ML_WIDTH // CONV_COLS))
        per_slot = 2 * ML_WIDTH // CONV_COLS // 4
        uq = jnp.dot(hb, wq_ref[...], preferred_element_type=F32)
        for _ in range(per_slot):
            conv_chunk(next(chunks))
        q_out[0, rows, :] = _group_norm_rope(uq, gq_ref[...], bd, cos, sina, sinb).astype(BF16)
        uk = jnp.dot(hb, wk_ref[...], preferred_element_type=F32)
        for _ in range(per_slot):
            conv_chunk(next(chunks))
        k_out[0, rows, :] = _group_norm_rope(uk, gk_ref[...], bd, cos, sina, sinb).astype(BF16)
        uv = jnp.dot(hb, wv_ref[...], preferred_element_type=F32)
        umv_t = lax.dot_general(wmv_ref[...], hb, nt, preferred_element_type=F32)
        for _ in range(per_slot):
            conv_chunk(next(chunks))
        v_out[0, rows, :] = uv.astype(BF16)
        mv_out[0, :, rows] = umv_t.astype(BF16)
        umo = jnp.dot(hb, wmo_ref[...], preferred_element_type=F32)
        for _ in range(per_slot):
            conv_chunk(next(chunks))
        mo_out[0, rows, :] = _sigmoid(umo).astype(BF16)

    halo_ref[...] = jnp.concatenate(tails, axis=1)


def _inproj(x, mod3, g, wq, wk, wv, wmqk, wmv, wmo, wif, gq, gk, rope, bd, cw, cb, bif):
    bsz, s, d = x.shape
    tm = PROJ_TM
    tok = lambda width, dt: jax.ShapeDtypeStruct((bsz, s, width), dt)
    tok_spec = lambda width: pl.BlockSpec((1, tm, width), lambda b, i: (b, i, 0))
    return pl.pallas_call(
        _inproj_kernel,
        grid=(bsz, s // tm),
        in_specs=[tok_spec(d),
                  pl.BlockSpec((1, 3, d), lambda b, i: (b, 0, 0)),
                  _const_spec((1, d)),
                  _const_spec(wq.shape), _const_spec(wk.shape), _const_spec(wv.shape),
                  _const_spec(wmqk.shape), _const_spec(wmv.shape), _const_spec(wmo.shape),
                  _const_spec(wif.shape),
                  _const_spec(gq.shape), _const_spec(gk.shape),
                  pl.BlockSpec((tm, 3 * LANES), lambda b, i: (i, 0)),
                  _const_spec(bd.shape), _const_spec(cw.shape), _const_spec(cb.shape),
                  _const_spec(bif.shape)],
        out_specs=[tok_spec(DA_WIDTH), tok_spec(DA_WIDTH), tok_spec(DA_WIDTH),
                   tok_spec(ML_WIDTH), tok_spec(ML_WIDTH),
                   pl.BlockSpec((1, ML_WIDTH, tm), lambda b, i: (b, 0, i)),
                   tok_spec(ML_WIDTH),
                   pl.BlockSpec((1, GATE_ROWS, tm), lambda b, i: (b, 0, i))],
        out_shape=[tok(DA_WIDTH, BF16), tok(DA_WIDTH, BF16), tok(DA_WIDTH, BF16),
                   tok(ML_WIDTH, BF16), tok(ML_WIDTH, BF16),
                   jax.ShapeDtypeStruct((bsz, ML_WIDTH, s), BF16),
                   tok(ML_WIDTH, BF16),
                   jax.ShapeDtypeStruct((bsz, GATE_ROWS, s), F32)],
        scratch_shapes=[pltpu.VMEM((CONV_HALO, 2 * ML_WIDTH), F32)],
        compiler_params=pltpu.CompilerParams(
            dimension_semantics=("arbitrary", "arbitrary"), vmem_limit_bytes=VMEM_LIMIT),
        name="in_proj",
    )(x, mod3, g.reshape(1, d), wq, wk, wv, wmqk, wmv, wmo, wif, gq, gk, rope, bd, cw, cb, bif)


def _attn_kernel(bounded_ref, q_ref, k_ref, v_ref, lam_ref, gout_ref, o_ref, qs_ref, m_ref, acc_ref):
    t = ATT_T
    u = t // 2
    lane = lax.broadcasted_iota(jnp.int32, (u, DA_V_DIM), 1)
    ones_cols = jnp.ones((t, DA_V_DIM), BF16)
    lv = lam_ref[...]
    lam = (jnp.exp(jnp.sum(lv[0:1] * lv[1:2], axis=1, keepdims=True))
           - jnp.exp(jnp.sum(lv[2:3] * lv[3:4], axis=1, keepdims=True)) + LAMBDA_INIT)

    def query_rows(qi, half, part):
        return pl.ds(pl.multiple_of(qi * (2 * t) + half * t + part * u, u), u)

    def stack_queries(qi):
        for half in range(2):
            for part in range(2):
                q = q_ref[0, query_rows(qi, half, part), :]
                zero = jnp.zeros_like(q)
                qs_ref[half, (2 * part) * u:(2 * part + 1) * u, :] = jnp.where(lane < DA_QK_DIM, q, zero)
                qs_ref[half, (2 * part + 1) * u:(2 * part + 2) * u, :] = jnp.where(lane >= DA_QK_DIM, q, zero)

    def keys(kb):
        start = pl.multiple_of(kb * t, t)
        return k_ref[0, pl.ds(start, t), :], jnp.concatenate([v_ref[0, pl.ds(start, t), :], ones_cols], axis=1)

    def piece(half, rows, ks, vs, causal_shift, first, stabilised):
        s = lax.dot_general(qs_ref[half, rows, :], ks, (((1,), (1,)), ((), ())), preferred_element_type=F32)
        if causal_shift is not None:
            row = lax.broadcasted_iota(jnp.int32, s.shape, 0) & (u - 1)
            col = lax.broadcasted_iota(jnp.int32, s.shape, 1)
            s = jnp.where(col <= row + causal_shift, s, -jnp.inf)
        if stabilised:
            m_new = jnp.max(s, axis=1, keepdims=True)
            if not first:
                m_prev = m_ref[half, rows, :]
                m_new = jnp.maximum(m_prev, m_new)
                alpha = jnp.exp2(m_prev - m_new)
            m_ref[half, rows, :] = m_new
            s = s - m_new
        pv = jnp.dot(jnp.exp2(s).astype(BF16), vs, preferred_element_type=F32)
        if first:
            acc_ref[half, rows, :] = pv
        elif stabilised:
            acc_ref[half, rows, :] = alpha * acc_ref[half, rows, :] + pv
        else:
            acc_ref[half, rows, :] += pv

    def block(half, kv, diagonal, first, stabilised):
        ks, vs = kv
        if diagonal:
            piece(half, slice(0, 2 * u), ks[0:u], vs[0:u], 0, first, stabilised)
            piece(half, slice(2 * u, 4 * u), ks, vs, u, first, stabilised)
        else:
            piece(half, slice(0, 4 * u), ks, vs, None, first, stabilised)

    def tile_pair(qi, stabilised):
        stack_queries(qi)
        kv = keys(2 * qi)
        block(0, kv, True, True, stabilised)
        block(1, kv, False, True, stabilised)
        block(1, keys(2 * qi + 1), True, False, stabilised)

        def body(pair, carry):
            for j in range(2):
                kv = keys(2 * pair + j)
                block(0, kv, False, False, stabilised)
                block(1, kv, False, False, stabilised)
            return carry

        lax.fori_loop(0, qi, body, 0)

        for half in range(2):
            for part in range(2):
                rows = slice(2 * part * u, (2 * part + 2) * u)
                o = acc_ref[half, rows, 0:DA_V_DIM] / acc_ref[half, rows, DA_V_DIM:2 * DA_V_DIM]
                od = o[0:u] - lam * o[u:2 * u]
                ms = jnp.mean(od * od, axis=-1, keepdims=True)
                o_ref[0, query_rows(qi, half, part), :] = (
                    (od * lax.rsqrt(ms + EPS)) * gout_ref[...] * (1.0 - LAMBDA_INIT)).astype(BF16)

    def run(stabilised):
        def body(qi, carry):
            tile_pair(qi, stabilised)
            return carry

        lax.fori_loop(0, q_ref.shape[1] // (2 * t), body, 0)

    @pl.when(bounded_ref[0] != 0)
    def _():
        run(False)

    @pl.when(bounded_ref[0] == 0)
    def _():
        run(True)


def _attention(bounded, q, k, v, lam_vecs, g_out):
    bsz, s, _ = q.shape
    t = ATT_T
    full = pl.BlockSpec((1, s, DA_V_DIM), lambda b, h: (b, 0, h))
    return pl.pallas_call(
        _attn_kernel,
        grid=(bsz, DA_HEADS),
        in_specs=[pl.BlockSpec(memory_space=pltpu.SMEM),
                  full, full, full, _const_spec(lam_vecs.shape), _const_spec((1, DA_V_DIM))],
        out_specs=full,
        out_shape=jax.ShapeDtypeStruct((bsz, s, DA_WIDTH), BF16),
        scratch_shapes=[pltpu.VMEM((2, 2 * t, DA_V_DIM), BF16),
                        pltpu.VMEM((2, 2 * t, 1), F32),
                        pltpu.VMEM((2, 2 * t, 2 * DA_V_DIM), F32)],
        compiler_params=pltpu.CompilerParams(
            dimension_semantics=("arbitrary", "arbitrary"), vmem_limit_bytes=VMEM_LIMIT),
        name="diff_attention",
    )(bounded, q, k, v, lam_vecs, g_out.reshape(1, DA_V_DIM))


def _mlstm_kernel(q_ref, k_ref, vt_ref, o_ref, gr_ref, gout_ref, y_ref, ct_ref, m_ref):
    ln = ML_CHUNK
    nh = ML_HEADS
    nt = (((1,), (1,)), ((), ()))

    @pl.when(pl.program_id(1) == 0)
    def _():
        ct_ref[...] = jnp.zeros(ct_ref.shape, F32)
        m_ref[...] = jnp.zeros(m_ref.shape, F32)

    row = lax.broadcasted_iota(jnp.int32, (ln, ln), 0)
    col = lax.broadcasted_iota(jnp.int32, (ln, ln), 1)
    lower = col <= row
    ones_row = (lax.broadcasted_iota(jnp.int32, (ML_STATE_ROWS - ML_DIM, ln), 0) == 0).astype(BF16)

    for r in range(ML_ROWS):
        gr = gr_ref[r]
        c, b = gr[0:nh], gr[nh:2 * nh]
        m_prev = jnp.concatenate([m_ref[r, 0:nh, :]] * (ln // LANES), axis=1)
        m_run = jnp.maximum(gr[2 * nh:3 * nh], m_prev)
        m_last = m_run[:, ln - 1:ln]
        e_inter = jnp.exp(m_prev - m_run)
        e_floor = jnp.exp(-(b + m_run))
        w_in = jnp.exp(c - m_last)
        decay = e_inter[:, ln - 1:ln]
        m_ref[r, 0:nh, :] = jnp.broadcast_to(b[:, ln - 1:ln] + m_last, (nh, LANES))
        m_run_cols = jnp.concatenate([m_run, m_run], axis=0).T

        for h in range(nh):
            lanes = slice(h * ML_DIM, (h + 1) * ML_DIM)
            q = q_ref[r, :, lanes]
            k = k_ref[r, :, lanes]
            vt_aug = jnp.concatenate([vt_ref[r, lanes, :], ones_row], axis=0)
            s = lax.dot_general(q, k, nt, preferred_element_type=F32)
            gate = jnp.exp(jnp.where(lower, c[h:h + 1, :] - m_run_cols[:, h:h + 1], -jnp.inf))
            w = (s * gate).astype(BF16)
            state_t = ct_ref[r, h]
            full_t = (e_inter[h:h + 1, :]
                      * lax.dot_general(state_t.astype(BF16), q, nt, preferred_element_type=F32)
                      + lax.dot_general(vt_aug, w, nt, preferred_element_type=F32))
            den = full_t[ML_DIM:ML_DIM + 1, :]
            hh_t = full_t[0:ML_DIM, :] / jnp.maximum(jnp.abs(den), e_floor[h:h + 1, :])
            hn_t = hh_t * lax.rsqrt(jnp.mean(hh_t * hh_t, axis=0, keepdims=True) + EPS)
            y_ref[r, :, lanes] = (o_ref[r, :, lanes].astype(F32) * (hn_t.T * gout_ref[:, lanes])).astype(BF16)

            vw = (vt_aug.astype(F32) * w_in[h:h + 1, :]).astype(BF16)
            ct_ref[r, h] = decay[h:h + 1, :] * state_t + jnp.dot(vw, k, preferred_element_type=F32)


def _mlstm(q, k, vt, o, gates_row, g_out):
    bsz, s, _ = q.shape
    ln = ML_CHUNK
    rows = ML_ROWS
    tok = pl.BlockSpec((rows, ln, ML_WIDTH), lambda b, c: (b, c, 0))
    return pl.pallas_call(
        _mlstm_kernel,
        grid=(bsz // rows, s // ln),
        in_specs=[tok, tok, pl.BlockSpec((rows, ML_WIDTH, ln), lambda b, c: (b, 0, c)), tok,
                  pl.BlockSpec((rows, GATE_ROWS, ln), lambda b, c: (b, 0, c)),
                  _const_spec((1, ML_WIDTH))],
        out_specs=tok,
        out_shape=jax.ShapeDtypeStruct((bsz, s, ML_WIDTH), BF16),
        scratch_shapes=[pltpu.VMEM((rows, ML_HEADS, ML_STATE_ROWS, ML_DIM), F32),
                        pltpu.VMEM((rows, 8, LANES), F32)],
        compiler_params=pltpu.CompilerParams(
            dimension_semantics=("arbitrary", "arbitrary"), vmem_limit_bytes=VMEM_LIMIT),
        name="mlstm",
    )(q, k, vt, o, gates_row, g_out.reshape(1, ML_WIDTH))


def _rope_tables(s):
    half = ROPE_DIM // 2
    pos = jnp.arange(s, dtype=F32)
    inv_freq = ROPE_THETA ** (-jnp.arange(0, ROPE_DIM, 2, dtype=F32) / ROPE_DIM)
    ang = pos[:, None] * inv_freq[None, :]
    basis = jnp.concatenate([jnp.cos(ang), jnp.sin(ang), jnp.ones((s, 1), F32)], axis=1)
    g = jnp.arange(LANES) % DA_QK_DIM
    j = jnp.arange(2 * half + 1)[:, None]
    cos_sel = jnp.where(g < ROPE_DIM, j == g % half, j == 2 * half).astype(F32)
    sin_up_sel = -((g < half) & (j == half + g)).astype(F32)
    sin_dn_sel = ((g >= half) & (g < ROPE_DIM) & (j == g)).astype(F32)
    sel = jnp.concatenate([cos_sel, sin_up_sel, sin_dn_sel], axis=1)
    return jnp.dot(basis, sel, precision=lax.Precision.HIGHEST)


def kernel(x, c, w_ada, b_ada, g_norm, ffn1_w12, ffn1_w3, w_in, conv_w, conv_b, b_igate, b_fgate,
           g_qnorm, g_knorm, lambda_qk, g_da_out, g_ml_out, w_out, ffn2_w12, ffn2_w3):
    bsz, s, d = x.shape
    l = 0
    mod = _ada(c, w_ada[l], b_ada[l]).reshape(bsz, 3, 3, d)

    x, wb, w12_2, w3_2, wo = _ffn(
        x, mod[:, 0], g_norm[l, 0], ffn1_w12[l].astype(BF16), ffn1_w3[l].astype(BF16),
        casts=((w_in, l), (ffn2_w12, l), (ffn2_w3, l), (w_out, l)))
    o0 = 0
    parts = []
    for width in (DA_WIDTH, DA_WIDTH, DA_WIDTH, 2 * ML_WIDTH, ML_WIDTH, ML_WIDTH, 2 * ML_HEADS):
        parts.append(wb[:, o0:o0 + width])
        o0 += width
    wq, wk, wv, wmqk, wmv, wmo, wif = parts
    wif = jnp.pad(wif.T, ((0, 2 * ML_HEADS), (0, 0)))
    wmv = wmv.T
    groups = DA_WIDTH // DA_QK_DIM
    q_gain = g_qnorm[l] * (DA_QK_DIM ** -0.5)
    gq = jnp.tile(q_gain * LOG2E, groups).reshape(1, DA_WIDTH)
    score_bound = 1.05 * DA_QK_DIM * jnp.max(jnp.abs(q_gain)) * jnp.max(jnp.abs(g_knorm[l]))
    bounded = (score_bound <= SAFE_SCORE_BOUND).astype(jnp.int32).reshape(1)
    gk = jnp.tile(g_knorm[l], groups).reshape(1, DA_WIDTH)
    rope = _rope_tables(s)
    gid = jnp.arange(MXU_TILE) // DA_QK_DIM
    bd = (gid[:, None] == gid[None, :]).astype(BF16)
    bif = jnp.concatenate([b_igate[l], b_fgate[l]]).reshape(2 * ML_HEADS, 1)
    da_q, da_k, da_v, ml_q, ml_k, ml_v, ml_o, gates = _inproj(
        x, mod[:, 1], g_norm[l, 1], wq, wk, wv, wmqk, wmv, wmo, wif, gq, gk, rope, bd,
        conv_w[l], conv_b[l].reshape(1, 2 * ML_WIDTH), bif)

    y_da = _attention(bounded, da_q, da_k, da_v, lambda_qk[l], g_da_out[l])
    y_ml = _mlstm(ml_q, ml_k, ml_v, ml_o, gates, g_ml_out[l])

    return _ffn(x, mod[:, 2], g_norm[l, 2], w12_2, w3_2,
                mix=(mod[:, 1], y_da, y_ml, wo[:DA_WIDTH], wo[DA_WIDTH:]))
```

```python
import functools

import jax
import jax.numpy as jnp
from jax import lax
from jax.experimental import pallas as pl
from jax.experimental.pallas import tpu as pltpu

F32 = jnp.float32
BF16 = jnp.bfloat16

D_MODEL = 1024
DA_HEADS = 4
DA_QK_DIM = 64
DA_V_DIM = 2 * DA_QK_DIM
DA_WIDTH = DA_HEADS * DA_V_DIM
ML_HEADS = 4
ML_DIM = 128
ML_WIDTH = ML_HEADS * ML_DIM
ROPE_THETA = 500000.0
ROPE_DIM = DA_QK_DIM // 4
D_FF = 2816
CONV_K = 4
EPS = 1e-6
LAMBDA_INIT = 0.8 - 0.6 * 1.0
LOG2E = 1.4426950408889634

LANES = 128
BF16_SUBLANES = 16
MXU_TILE = 256
VMEM_LIMIT = 56 * 1024 * 1024

FFN_TM = 1024
FFN_SUB = 256
PROJ_TM = 1024
PROJ_SUB = 256
CONV_COLS = 256
ATT_T = 512
SAFE_SCORE_BOUND = 40.0
ML_CHUNK = 256
ML_ROWS = 4
ML_STATE_ROWS = ML_DIM + 16
GATE_ROWS = 16
CONV_HALO = 8


def _const_spec(shape):
    nd = len(shape)
    return pl.BlockSpec(shape, lambda *_: (0,) * nd, pipeline_mode=pl.Buffered(1))


def _sigmoid(x):
    return 1.0 / (1.0 + jnp.exp(-x))


def _mod_norm(x, g, shift, scale):
    ms = jnp.mean(x * x, axis=-1, keepdims=True)
    return (x * lax.rsqrt(ms + EPS)) * (g * (1.0 + scale)) + shift


def _ada_kernel(c_ref, w_ref, b_ref, o_ref):
    c = c_ref[...]
    cs = (c * _sigmoid(c)).astype(BF16)
    o_ref[...] = jnp.dot(cs, w_ref[...].astype(BF16), preferred_element_type=F32) + b_ref[...]


def _ada(c, w_ada, b_ada):
    bsz, d = c.shape
    n = w_ada.shape[1]
    tn = 1024
    return pl.pallas_call(
        _ada_kernel,
        grid=(n // tn,),
        in_specs=[pl.BlockSpec((bsz, d), lambda j: (0, 0)),
                  pl.BlockSpec((d, tn), lambda j: (0, j)),
                  pl.BlockSpec((1, tn), lambda j: (0, j))],
        out_specs=pl.BlockSpec((bsz, tn), lambda j: (0, j)),
        out_shape=jax.ShapeDtypeStruct((bsz, n), F32),
        compiler_params=pltpu.CompilerParams(dimension_semantics=("arbitrary",)),
        name="adaln_mod",
    )(c, w_ada, b_ada.reshape(1, n))


def _ffn_body(x, mod_ref, g_ref, w12_ref, w3_ref, o_ref, act_ref):
    mod = mod_ref[0]
    hb = _mod_norm(x, g_ref[...], mod[0:1], mod[1:2]).astype(BF16)
    for c in range(D_FF // FFN_SUB):
        cols = slice(c * FFN_SUB, (c + 1) * FFN_SUB)
        a = jnp.dot(hb, w12_ref[:, cols], preferred_element_type=F32)
        b = jnp.dot(hb, w12_ref[:, D_FF + c * FFN_SUB:D_FF + (c + 1) * FFN_SUB], preferred_element_type=F32)
        act_ref[:, cols] = (a * _sigmoid(a) * b).astype(BF16)
    y = jnp.dot(act_ref[...], w3_ref[...], preferred_element_type=F32)
    o_ref[0] = x + (0.5 * (1.0 + mod[2:3])) * y


def _ffn_kernel(x_ref, mod_ref, g_ref, w12_ref, w3_ref, *rest):
    n = (len(rest) - 2) // 2
    casts_in, o_ref, casts_out, act_ref = rest[:n], rest[n], rest[n + 1:2 * n + 1], rest[-1]
    _ffn_body(x_ref[0], mod_ref, g_ref, w12_ref, w3_ref, o_ref, act_ref)
    for src, dst in zip(casts_in, casts_out):
        dst[...] = src[...].astype(BF16)


def _mix_ffn_kernel(x_ref, mmod_ref, ya_ref, ym_ref, wa_ref, wm_ref,
                    mod_ref, g_ref, w12_ref, w3_ref, o_ref, act_ref):
    y = (jnp.dot(ya_ref[0], wa_ref[...], preferred_element_type=F32)
         + jnp.dot(ym_ref[0], wm_ref[...], preferred_element_type=F32))
    x = x_ref[0] + (1.0 + mmod_ref[0][2:3]) * y
    _ffn_body(x, mod_ref, g_ref, w12_ref, w3_ref, o_ref, act_ref)


def _ffn(x, mod3, g, w12, w3, mix=None, casts=()):
    bsz, s, d = x.shape
    tm = FFN_TM
    steps = bsz * (s // tm)
    tok_spec = lambda width: pl.BlockSpec((1, tm, width), lambda b, i: (b, i, 0))
    mod_spec = pl.BlockSpec((1, 3, d), lambda b, i: (b, 0, 0))
    ffn_specs = [mod_spec, _const_spec((1, d)),
                 _const_spec(w12.shape), _const_spec(w3.shape)]
    ffn_args = (mod3, g.reshape(1, d), w12, w3)
    def slab(w, layer=None):
        rows, cols = w.shape[-2:]
        lead = () if layer is None else (layer,)
        if rows % (steps * BF16_SUBLANES) == 0:
            return pl.BlockSpec((None,) * len(lead) + (rows // steps, cols),
                                lambda b, i: lead + (b * (s // tm) + i, 0))
        return pl.BlockSpec((None,) * len(lead) + (rows // bsz, cols), lambda b, i: lead + (b, 0))

    out_specs, out_shape = tok_spec(d), jax.ShapeDtypeStruct(x.shape, F32)
    if mix is None:
        body = _ffn_kernel
        in_specs = [tok_spec(d)] + ffn_specs + [slab(w, layer) for w, layer in casts]
        args = (x,) + ffn_args + tuple(w for w, _ in casts)
        if casts:
            cast_shapes = [jax.ShapeDtypeStruct(w.shape[-2:], BF16) for w, _ in casts]
            out_specs = [out_specs] + [slab(c) for c in cast_shapes]
            out_shape = [out_shape] + cast_shapes
    else:
        mmod3, y_da, y_ml, w_a, w_m = mix
        body = _mix_ffn_kernel
        in_specs = [tok_spec(d), mod_spec, tok_spec(DA_WIDTH), tok_spec(ML_WIDTH),
                    _const_spec(w_a.shape), _const_spec(w_m.shape)] + ffn_specs
        args = (x, mmod3, y_da, y_ml, w_a, w_m) + ffn_args
    return pl.pallas_call(
        body,
        grid=(bsz, s // tm),
        in_specs=in_specs,
        out_specs=out_specs,
        out_shape=out_shape,
        scratch_shapes=[pltpu.VMEM((tm, D_FF), BF16)],
        compiler_params=pltpu.CompilerParams(
            dimension_semantics=("arbitrary", "arbitrary"), vmem_limit_bytes=VMEM_LIMIT),
        name="ffn" if mix is None else "mix_ffn",
    )(*args)


def _group_norm_rope(u, gvec, bd, cos, sina, sinb):
    x2 = u * u
    ssq = jnp.concatenate(
        [jnp.dot(x2[:, j:j + MXU_TILE].astype(BF16), bd, preferred_element_type=F32)
         for j in range(0, u.shape[1], MXU_TILE)], axis=1)
    xn = (u * lax.rsqrt(ssq * (1.0 / DA_QK_DIM) + EPS)) * gvec
    outs = []
    for h in range(u.shape[1] // LANES):
        xh = xn[:, h * LANES:(h + 1) * LANES]
        up = pltpu.roll(xh, LANES - ROPE_DIM // 2, 1)
        dn = pltpu.roll(xh, ROPE_DIM // 2, 1)
        outs.append(xh * cos + up * sina + dn * sinb)
    return jnp.concatenate(outs, axis=1)


def _chunk_scan(x, op, identity):
    pos = lax.broadcasted_iota(jnp.int32, x.shape, 1) & (ML_CHUNK - 1)
    d = 1
    while d < ML_CHUNK:
        x = op(x, jnp.where(pos >= d, pltpu.roll(x, d, 1), identity))
        d *= 2
    return x


def _inproj_kernel(x_ref, mod_ref, g_ref, wq_ref, wk_ref, wv_ref, wmqk_ref, wmv_ref, wmo_ref, wif_ref,
                   gq_ref, gk_ref, rope_ref, bd_ref, cw_ref, cb_ref, bif_ref,
                   q_out, k_out, v_out, mq_out, mk_out, mv_out, mo_out, gate_out, halo_ref):
    tm = x_ref.shape[1]
    si = pl.program_id(1)
    mod = mod_ref[0]
    nh = ML_HEADS
    nt = (((1,), (1,)), ((), ()))
    bd = bd_ref[...]
    cw = cw_ref[...]

    tails = jnp.where(si == 0, 0.0, halo_ref[...])
    tails = [tails[:, ci * CONV_COLS:(ci + 1) * CONV_COLS] for ci in range(2 * ML_WIDTH // CONV_COLS)]

    for r0 in range(0, tm, PROJ_SUB):
        rows = slice(r0, r0 + PROJ_SUB)
        hb = _mod_norm(x_ref[0, rows, :], g_ref[...], mod[0:1], mod[1:2]).astype(BF16)

        pre = lax.dot_general(wif_ref[...], hb, nt, preferred_element_type=F32)[0:2 * nh] + bif_ref[...]
        zf = pre[nh:2 * nh]
        log_f = jnp.minimum(zf, 0.0) - jnp.log1p(jnp.exp(-jnp.abs(zf)))
        b = _chunk_scan(log_f, jnp.add, 0.0)
        c = pre[0:nh] - b
        gate_out[0, :, rows] = jnp.concatenate(
            [c, b, _chunk_scan(c, jnp.maximum, -jnp.inf), jnp.zeros_like(c)], axis=0)

        cos, sina, sinb = (rope_ref[rows, j * LANES:(j + 1) * LANES] for j in range(3))

        def conv_chunk(ci):
            cols = slice(ci * CONV_COLS, (ci + 1) * CONV_COLS)
            u = jnp.dot(hb, wmqk_ref[:, cols], preferred_element_type=F32)
            ext = jnp.concatenate([tails[ci], u], axis=0)
            tails[ci] = u[PROJ_SUB - CONV_HALO:, :]
            acc = cb_ref[:, cols] + u * cw[CONV_K - 1:CONV_K, cols]
            for d in range(1, CONV_K):
                acc = acc + pltpu.roll(ext, d, 0)[CONV_HALO:, :] * cw[CONV_K - 1 - d:CONV_K - d, cols]
            act = acc * _sigmoid(acc)
            if ci < ML_WIDTH // CONV_COLS:
                mq_out[0, rows, cols] = (act * (ML_DIM ** -0.5)).astype(BF16)
            else:
                mk_out[0, rows, ci * CONV_COLS - ML_WIDTH:(ci + 1) * CONV_COLS - ML_WIDTH] = act.astype(BF16)

        chunks = iter(range(2 * ML_WIDTH // CONV_COLS))
        per_slot = 2 * ML_WIDTH // CONV_COLS // 4
        uq = jnp.dot(hb, wq_ref[...], preferred_element_type=F32)
        for _ in range(per_slot):
            conv_chunk(next(chunks))
        q_out[0, rows, :] = _group_norm_rope(uq, gq_ref[...], bd, cos, sina, sinb).astype(BF16)
        uk = jnp.dot(hb, wk_ref[...], preferred_element_type=F32)
        for _ in range(per_slot):
            conv_chunk(next(chunks))
        k_out[0, rows, :] = _group_norm_rope(uk, gk_ref[...], bd, cos, sina, sinb).astype(BF16)
        uv = jnp.dot(hb, wv_ref[...], preferred_element_type=F32)
        umv_t = lax.dot_general(wmv_ref[...], hb, nt, preferred_element_type=F32)
        for _ in range(per_slot):
            conv_chunk(next(chunks))
        v_out[0, rows, :] = uv.astype(BF16)
        mv_out[0, :, rows] = umv_t.astype(BF16)
        umo = jnp.dot(hb, wmo_ref[...], preferred_element_type=F32)
        for _ in range(per_slot):
            conv_chunk(next(chunks))
        mo_out[0, rows, :] = _sigmoid(umo).astype(BF16)

    halo_ref[...] = jnp.concatenate(tails, axis=1)


def _inproj(x, mod3, g, wq, wk, wv, wmqk, wmv, wmo, wif, gq, gk, rope, bd, cw, cb, bif):
    bsz, s, d = x.shape
    tm = PROJ_TM
    tok = lambda width, dt: jax.ShapeDtypeStruct((bsz, s, width), dt)
    tok_spec = lambda width: pl.BlockSpec((1, tm, width), lambda b, i: (b, i, 0))
    return pl.pallas_call(
        _inproj_kernel,
        grid=(bsz, s // tm),
        in_specs=[tok_spec(d),
                  pl.BlockSpec((1, 3, d), lambda b, i: (b, 0, 0)),
                  _const_spec((1, d)),
                  _const_spec(wq.shape), _const_spec(wk.shape), _const_spec(wv.shape),
                  _const_spec(wmqk.shape), _const_spec(wmv.shape), _const_spec(wmo.shape),
                  _const_spec(wif.shape),
                  _const_spec(gq.shape), _const_spec(gk.shape),
                  pl.BlockSpec((tm, 3 * LANES), lambda b, i: (i, 0)),
                  _const_spec(bd.shape), _const_spec(cw.shape), _const_spec(cb.shape),
                  _const_spec(bif.shape)],
        out_specs=[tok_spec(DA_WIDTH), tok_spec(DA_WIDTH), tok_spec(DA_WIDTH),
                   tok_spec(ML_WIDTH), tok_spec(ML_WIDTH),
                   pl.BlockSpec((1, ML_WIDTH, tm), lambda b, i: (b, 0, i)),
                   tok_spec(ML_WIDTH),
                   pl.BlockSpec((1, GATE_ROWS, tm), lambda b, i: (b, 0, i))],
        out_shape=[tok(DA_WIDTH, BF16), tok(DA_WIDTH, BF16), tok(DA_WIDTH, BF16),
                   tok(ML_WIDTH, BF16), tok(ML_WIDTH, BF16),
                   jax.ShapeDtypeStruct((bsz, ML_WIDTH, s), BF16),
                   tok(ML_WIDTH, BF16),
                   jax.ShapeDtypeStruct((bsz, GATE_ROWS, s), F32)],
        scratch_shapes=[pltpu.VMEM((CONV_HALO, 2 * ML_WIDTH), F32)],
        compiler_params=pltpu.CompilerParams(
            dimension_semantics=("arbitrary", "arbitrary"), vmem_limit_bytes=VMEM_LIMIT),
        name="in_proj",
    )(x, mod3, g.reshape(1, d), wq, wk, wv, wmqk, wmv, wmo, wif, gq, gk, rope, bd, cw, cb, bif)


def _attn_kernel(bounded_ref, q_ref, k_ref, v_ref, lam_ref, gout_ref, o_ref, qs_ref, m_ref, acc_ref):
    t = ATT_T
    u = t // 2
    lane = lax.broadcasted_iota(jnp.int32, (u, DA_V_DIM), 1)
    ones_cols = jnp.ones((t, DA_V_DIM), BF16)
    lv = lam_ref[...]
    lam = (jnp.exp(jnp.sum(lv[0:1] * lv[1:2], axis=1, keepdims=True))
           - jnp.exp(jnp.sum(lv[2:3] * lv[3:4], axis=1, keepdims=True)) + LAMBDA_INIT)

    def query_rows(qi, half, part):
        return pl.ds(pl.multiple_of(qi * (2 * t) + half * t + part * u, u), u)

    def stack_queries(qi):
        for half in range(2):
            for part in range(2):
                q = q_ref[0, query_rows(qi, half, part), :]
                zero = jnp.zeros_like(q)
                qs_ref[half, (2 * part) * u:(2 * part + 1) * u, :] = jnp.where(lane < DA_QK_DIM, q, zero)
                qs_ref[half, (2 * part + 1) * u:(2 * part + 2) * u, :] = jnp.where(lane >= DA_QK_DIM, q, zero)

    def keys(kb):
        start = pl.multiple_of(kb * t, t)
        return k_ref[0, pl.ds(start, t), :], jnp.concatenate([v_ref[0, pl.ds(start, t), :], ones_cols], axis=1)

    def piece(half, rows, ks, vs, causal_shift, first, stabilised):
        s = lax.dot_general(qs_ref[half, rows, :], ks, (((1,), (1,)), ((), ())), preferred_element_type=F32)
        if causal_shift is not None:
            row = lax.broadcasted_iota(jnp.int32, s.shape, 0) & (u - 1)
            col = lax.broadcasted_iota(jnp.int32, s.shape, 1)
            s = jnp.where(col <= row + causal_shift, s, -jnp.inf)
        if stabilised:
            m_new = jnp.max(s, axis=1, keepdims=True)
            if not first:
                m_prev = m_ref[half, rows, :]
                m_new = jnp.maximum(m_prev, m_new)
                alpha = jnp.exp2(m_prev - m_new)
            m_ref[half, rows, :] = m_new
            s = s - m_new
        pv = jnp.dot(jnp.exp2(s).astype(BF16), vs, preferred_element_type=F32)
        if first:
            acc_ref[half, rows, :] = pv
        elif stabilised:
            acc_ref[half, rows, :] = alpha * acc_ref[half, rows, :] + pv
        else:
            acc_ref[half, rows, :] += pv

    def block(half, kv, diagonal, first, stabilised):
        ks, vs = kv
        if diagonal:
            piece(half, slice(0, 2 * u), ks[0:u], vs[0:u], 0, first, stabilised)
            piece(half, slice(2 * u, 4 * u), ks, vs, u, first, stabilised)
        else:
            piece(half, slice(0, 4 * u), ks, vs, None, first, stabilised)

    def tile_pair(qi, stabilised):
        stack_queries(qi)
        kv = keys(2 * qi)
        block(0, kv, True, True, stabilised)
        block(1, kv, False, True, stabilised)
        block(1, keys(2 * qi + 1), True, False, stabilised)

        def body(pair, carry):
            for j in range(2):
                kv = keys(2 * pair + j)
                block(0, kv, False, False, stabilised)
                block(1, kv, False, False, stabilised)
            return carry

        lax.fori_loop(0, qi, body, 0)

        for half in range(2):
            for part in range(2):
                rows = slice(2 * part * u, (2 * part + 2) * u)
                o = acc_ref[half, rows, 0:DA_V_DIM] / acc_ref[half, rows, DA_V_DIM:2 * DA_V_DIM]
                od = o[0:u] - lam * o[u:2 * u]
                ms = jnp.mean(od * od, axis=-1, keepdims=True)
                o_ref[0, query_rows(qi, half, part), :] = (
                    (od * lax.rsqrt(ms + EPS)) * gout_ref[...] * (1.0 - LAMBDA_INIT)).astype(BF16)

    def run(stabilised):
        def body(qi, carry):
            tile_pair(qi, stabilised)
            return carry

        lax.fori_loop(0, q_ref.shape[1] // (2 * t), body, 0)

    @pl.when(bounded_ref[0] != 0)
    def _():
        run(False)

    @pl.when(bounded_ref[0] == 0)
    def _():
        run(True)


def _attention(bounded, q, k, v, lam_vecs, g_out):
    bsz, s, _ = q.shape
    t = ATT_T
    full = pl.BlockSpec((1, s, DA_V_DIM), lambda b, h: (b, 0, h))
    return pl.pallas_call(
        _attn_kernel,
        grid=(bsz, DA_HEADS),
        in_specs=[pl.BlockSpec(memory_space=pltpu.SMEM),
                  full, full, full, _const_spec(lam_vecs.shape), _const_spec((1, DA_V_DIM))],
        out_specs=full,
        out_shape=jax.ShapeDtypeStruct((bsz, s, DA_WIDTH), BF16),
        scratch_shapes=[pltpu.VMEM((2, 2 * t, DA_V_DIM), BF16),
                        pltpu.VMEM((2, 2 * t, 1), F32),
                        pltpu.VMEM((2, 2 * t, 2 * DA_V_DIM), F32)],
        compiler_params=pltpu.CompilerParams(
            dimension_semantics=("arbitrary", "arbitrary"), vmem_limit_bytes=VMEM_LIMIT),
        name="diff_attention",
    )(bounded, q, k, v, lam_vecs, g_out.reshape(1, DA_V_DIM))


def _mlstm_kernel(q_ref, k_ref, vt_ref, o_ref, gr_ref, gout_ref, y_ref, ct_ref, m_ref):
    ln = ML_CHUNK
    nh = ML_HEADS
    nt = (((1,), (1,)), ((), ()))

    @pl.when(pl.program_id(1) == 0)
    def _():
        ct_ref[...] = jnp.zeros(ct_ref.shape, F32)
        m_ref[...] = jnp.zeros(m_ref.shape, F32)

    row = lax.broadcasted_iota(jnp.int32, (ln, ln), 0)
    col = lax.broadcasted_iota(jnp.int32, (ln, ln), 1)
    lower = col <= row
    ones_row = (lax.broadcasted_iota(jnp.int32, (ML_STATE_ROWS - ML_DIM, ln), 0) == 0).astype(BF16)

    for r in range(ML_ROWS):
        gr = gr_ref[r]
        c, b = gr[0:nh], gr[nh:2 * nh]
        m_prev = jnp.concatenate([m_ref[r, 0:nh, :]] * (ln // LANES), axis=1)
        m_run = jnp.maximum(gr[2 * nh:3 * nh], m_prev)
        m_last = m_run[:, ln - 1:ln]
        e_inter = jnp.exp(m_prev - m_run)
        e_floor = jnp.exp(-(b + m_run))
        w_in = jnp.exp(c - m_last)
        decay = e_inter[:, ln - 1:ln]
        m_ref[r, 0:nh, :] = jnp.broadcast_to(b[:, ln - 1:ln] + m_last, (nh, LANES))
        m_run_cols = jnp.concatenate([m_run, m_run], axis=0).T

        for h in range(nh):
            lanes = slice(h * ML_DIM, (h + 1) * ML_DIM)
            q = q_ref[r, :, lanes]
            k = k_ref[r, :, lanes]
            vt_aug = jnp.concatenate([vt_ref[r, lanes, :], ones_row], axis=0)
            s = lax.dot_general(q, k, nt, preferred_element_type=F32)
            gate = jnp.exp(jnp.where(lower, c[h:h + 1, :] - m_run_cols[:, h:h + 1], -jnp.inf))
            w = (s * gate).astype(BF16)
            state_t = ct_ref[r, h]
            full_t = (e_inter[h:h + 1, :]
                      * lax.dot_general(state_t.astype(BF16), q, nt, preferred_element_type=F32)
                      + lax.dot_general(vt_aug, w, nt, preferred_element_type=F32))
            den = full_t[ML_DIM:ML_DIM + 1, :]
            hh_t = full_t[0:ML_DIM, :] / jnp.maximum(jnp.abs(den), e_floor[h:h + 1, :])
            hn_t = hh_t * lax.rsqrt(jnp.mean(hh_t * hh_t, axis=0, keepdims=True) + EPS)
            y_ref[r, :, lanes] = (o_ref[r, :, lanes].astype(F32) * (hn_t.T * gout_ref[:, lanes])).astype(BF16)

            vw = (vt_aug.astype(F32) * w_in[h:h + 1, :]).astype(BF16)
            ct_ref[r, h] = decay[h:h + 1, :] * state_t + jnp.dot(vw, k, preferred_element_type=F32)


def _mlstm(q, k, vt, o, gates_row, g_out):
    bsz, s, _ = q.shape
    ln = ML_CHUNK
    rows = ML_ROWS
    tok = pl.BlockSpec((rows, ln, ML_WIDTH), lambda b, c: (b, c, 0))
    return pl.pallas_call(
        _mlstm_kernel,
        grid=(bsz // rows, s // ln),
        in_specs=[tok, tok, pl.BlockSpec((rows, ML_WIDTH, ln), lambda b, c: (b, 0, c)), tok,
                  pl.BlockSpec((rows, GATE_ROWS, ln), lambda b, c: (b, 0, c)),
                  _const_spec((1, ML_WIDTH))],
        out_specs=tok,
        out_shape=jax.ShapeDtypeStruct((bsz, s, ML_WIDTH), BF16),
        scratch_shapes=[pltpu.VMEM((rows, ML_HEADS, ML_STATE_ROWS, ML_DIM), F32),
                        pltpu.VMEM((rows, 8, LANES), F32)],
        compiler_params=pltpu.CompilerParams(
            dimension_semantics=("arbitrary", "arbitrary"), vmem_limit_bytes=VMEM_LIMIT),
        name="mlstm",
    )(q, k, vt, o, gates_row, g_out.reshape(1, ML_WIDTH))


def _rope_tables(s):
    half = ROPE_DIM // 2
    pos = jnp.arange(s, dtype=F32)
    inv_freq = ROPE_THETA ** (-jnp.arange(0, ROPE_DIM, 2, dtype=F32) / ROPE_DIM)
    ang = pos[:, None] * inv_freq[None, :]
    basis = jnp.concatenate([jnp.cos(ang), jnp.sin(ang), jnp.ones((s, 1), F32)], axis=1)
    g = jnp.arange(LANES) % DA_QK_DIM
    j = jnp.arange(2 * half + 1)[:, None]
    cos_sel = jnp.where(g < ROPE_DIM, j == g % half, j == 2 * half).astype(F32)
    sin_up_sel = -((g < half) & (j == half + g)).astype(F32)
    sin_dn_sel = ((g >= half) & (g < ROPE_DIM) & (j == g)).astype(F32)
    sel = jnp.concatenate([cos_sel, sin_up_sel, sin_dn_sel], axis=1)
    return jnp.dot(basis, sel, precision=lax.Precision.HIGHEST)


def kernel(x, c, w_ada, b_ada, g_norm, ffn1_w12, ffn1_w3, w_in, conv_w, conv_b, b_igate, b_fgate,
           g_qnorm, g_knorm, lambda_qk, g_da_out, g_ml_out, w_out, ffn2_w12, ffn2_w3):
    bsz, s, d = x.shape
    l = 0
    mod = _ada(c, w_ada[l], b_ada[l]).reshape(bsz, 3, 3, d)

    x, w12_2, w3_2, wo = _ffn(
        x, mod[:, 0], g_norm[l, 0], ffn1_w12[l].astype(BF16), ffn1_w3[l].astype(BF16),
        casts=((ffn2_w12, l), (ffn2_w3, l), (w_out, l)))
    wb = w_in[l].astype(BF16)
    o0 = 0
    parts = []
    for width in (DA_WIDTH, DA_WIDTH, DA_WIDTH, 2 * ML_WIDTH, ML_WIDTH, ML_WIDTH, 2 * ML_HEADS):
        parts.append(wb[:, o0:o0 + width])
        o0 += width
    wq, wk, wv, wmqk, wmv, wmo, wif = parts
    wif = jnp.pad(wif.T, ((0, 2 * ML_HEADS), (0, 0)))
    wmv = wmv.T
    groups = DA_WIDTH // DA_QK_DIM
    q_gain = g_qnorm[l] * (DA_QK_DIM ** -0.5)
    gq = jnp.tile(q_gain * LOG2E, groups).reshape(1, DA_WIDTH)
    score_bound = 1.05 * DA_QK_DIM * jnp.max(jnp.abs(q_gain)) * jnp.max(jnp.abs(g_knorm[l]))
    bounded = (score_bound <= SAFE_SCORE_BOUND).astype(jnp.int32).reshape(1)
    gk = jnp.tile(g_knorm[l], groups).reshape(1, DA_WIDTH)
    rope = _rope_tables(s)
    gid = jnp.arange(MXU_TILE) // DA_QK_DIM
    bd = (gid[:, None] == gid[None, :]).astype(BF16)
    bif = jnp.concatenate([b_igate[l], b_fgate[l]]).reshape(2 * ML_HEADS, 1)
    da_q, da_k, da_v, ml_q, ml_k, ml_v, ml_o, gates = _inproj(
        x, mod[:, 1], g_norm[l, 1], wq, wk, wv, wmqk, wmv, wmo, wif, gq, gk, rope, bd,
        conv_w[l], conv_b[l].reshape(1, 2 * ML_WIDTH), bif)

    y_da = _attention(bounded, da_q, da_k, da_v, lambda_qk[l], g_da_out[l])
    y_ml = _mlstm(ml_q, ml_k, ml_v, ml_o, gates, g_ml_out[l])

    return _ffn(x, mod[:, 2], g_norm[l, 2], w12_2, w3_2,
                mix=(mod[:, 1], y_da, y_ml, wo[:DA_WIDTH], wo[DA_WIDTH:]))
```

```python
import math

import jax
import jax.numpy as jnp
from jax import lax
from jax.experimental import pallas as pl
from jax.experimental.pallas import tpu as pltpu

F32 = jnp.float32
BF16 = jnp.bfloat16

DA_HEADS = 4
DA_QK_DIM = 64
DA_V_DIM = 2 * DA_QK_DIM
DA_WIDTH = DA_HEADS * DA_V_DIM
ML_HEADS = 4
ML_DIM = 128
ML_WIDTH = ML_HEADS * ML_DIM
ROPE_THETA = 500000.0
ROPE_DIM = DA_QK_DIM // 4
D_FF = 2816
CONV_K = 4
EPS = 1e-6
LAMBDA_INIT = 0.8 - 0.6 * math.exp(-0.3 * 0)
LOG2E = 1.4426950408889634

LANES = 128
F32_SUBLANES = 8
BF16_SUBLANES = 16
MXU_TILE = 256
VMEM_LIMIT = 56 * 1024 * 1024

ADA_TN = 1024
FFN_TM = 1024
FFN_SUB = 256
PROJ_TM = 1024
PROJ_SUB = 256
CONV_COLS = 256
ATT_T = 512
SAFE_SCORE_BOUND = 40.0
BF16_ROUNDING_SLACK = 1.05
ML_CHUNK = 256
ML_ROWS = 4
ML_STATE_ROWS = ML_DIM + BF16_SUBLANES
GATE_ROWS = 16
CONV_HALO = F32_SUBLANES
assert CONV_K == 4


def _const_spec(shape):
    nd = len(shape)
    return pl.BlockSpec(shape, lambda *_: (0,) * nd, pipeline_mode=pl.Buffered(1))


def _sigmoid(x):
    return 1.0 / (1.0 + jnp.exp(-x))


def _mod_norm(x, g, shift, scale):
    ms = jnp.mean(x * x, axis=-1, keepdims=True)
    return (x * lax.rsqrt(ms + EPS)) * (g * (1.0 + scale)) + shift


def _ada_kernel(c_ref, w_ref, b_ref, o_ref):
    c = c_ref[...]
    cs = (c * _sigmoid(c)).astype(BF16)
    o_ref[...] = jnp.dot(cs, w_ref[...].astype(BF16), preferred_element_type=F32) + b_ref[...]


def _ada(c, w_ada, b_ada):
    bsz, d = c.shape
    n = w_ada.shape[1]
    tn = ADA_TN
    return pl.pallas_call(
        _ada_kernel,
        grid=(n // tn,),
        in_specs=[pl.BlockSpec((bsz, d), lambda j: (0, 0)),
                  pl.BlockSpec((d, tn), lambda j: (0, j)),
                  pl.BlockSpec((1, tn), lambda j: (0, j))],
        out_specs=pl.BlockSpec((bsz, tn), lambda j: (0, j)),
        out_shape=jax.ShapeDtypeStruct((bsz, n), F32),
        compiler_params=pltpu.CompilerParams(dimension_semantics=("arbitrary",)),
        name="adaln_mod",
    )(c, w_ada, b_ada.reshape(1, n))


def _ffn_body(x, mod_ref, g_ref, w12_ref, w3_ref, o_ref, act_ref):
    mod = mod_ref[0]
    hb = _mod_norm(x, g_ref[...], mod[0:1], mod[1:2]).astype(BF16)
    for c in range(D_FF // FFN_SUB):
        cols = slice(c * FFN_SUB, (c + 1) * FFN_SUB)
        a = jnp.dot(hb, w12_ref[:, cols], preferred_element_type=F32)
        b = jnp.dot(hb, w12_ref[:, D_FF + c * FFN_SUB:D_FF + (c + 1) * FFN_SUB], preferred_element_type=F32)
        act_ref[:, cols] = (a * _sigmoid(a) * b).astype(BF16)
    y = jnp.dot(act_ref[...], w3_ref[...], preferred_element_type=F32)
    o_ref[0] = x + (0.5 * (1.0 + mod[2:3])) * y


def _ffn_kernel(x_ref, mod_ref, g_ref, w12_ref, w3_ref, *rest):
    n = (len(rest) - 2) // 2
    casts_in, o_ref, casts_out, act_ref = rest[:n], rest[n], rest[n + 1:2 * n + 1], rest[-1]
    _ffn_body(x_ref[0], mod_ref, g_ref, w12_ref, w3_ref, o_ref, act_ref)
    for src, dst in zip(casts_in, casts_out):
        dst[...] = src[...].astype(BF16)


def _mix_ffn_kernel(x_ref, mmod_ref, ya_ref, ym_ref, wa_ref, wm_ref,
                    mod_ref, g_ref, w12_ref, w3_ref, o_ref, act_ref):
    y = (jnp.dot(ya_ref[0], wa_ref[...], preferred_element_type=F32)
         + jnp.dot(ym_ref[0], wm_ref[...], preferred_element_type=F32))
    x = x_ref[0] + (1.0 + mmod_ref[0][2:3]) * y
    _ffn_body(x, mod_ref, g_ref, w12_ref, w3_ref, o_ref, act_ref)


def _ffn(x, mod3, g, w12, w3, mix=None, casts=()):
    bsz, s, d = x.shape
    tm = FFN_TM
    steps = bsz * (s // tm)
    tok_spec = lambda width: pl.BlockSpec((1, tm, width), lambda b, i: (b, i, 0))
    mod_spec = pl.BlockSpec((1, 3, d), lambda b, i: (b, 0, 0))
    ffn_specs = [mod_spec, _const_spec((1, d)),
                 _const_spec(w12.shape), _const_spec(w3.shape)]
    ffn_args = (mod3, g.reshape(1, d), w12, w3)

    def slab(w, layer=None):
        rows, cols = w.shape[-2:]
        lead = () if layer is None else (layer,)
        if rows % (steps * BF16_SUBLANES) == 0:
            return pl.BlockSpec((None,) * len(lead) + (rows // steps, cols),
                                lambda b, i: lead + (b * (s // tm) + i, 0))
        return pl.BlockSpec((None,) * len(lead) + (rows // bsz, cols), lambda b, i: lead + (b, 0))

    out_specs, out_shape = tok_spec(d), jax.ShapeDtypeStruct(x.shape, F32)
    if mix is None:
        body = _ffn_kernel
        in_specs = [tok_spec(d)] + ffn_specs + [slab(w, layer) for w, layer in casts]
        args = (x,) + ffn_args + tuple(w for w, _ in casts)
        if casts:
            cast_shapes = [jax.ShapeDtypeStruct(w.shape[-2:], BF16) for w, _ in casts]
            out_specs = [out_specs] + [slab(c) for c in cast_shapes]
            out_shape = [out_shape] + cast_shapes
    else:
        mmod3, y_da, y_ml, w_a, w_m = mix
        body = _mix_ffn_kernel
        in_specs = [tok_spec(d), mod_spec, tok_spec(DA_WIDTH), tok_spec(ML_WIDTH),
                    _const_spec(w_a.shape), _const_spec(w_m.shape)] + ffn_specs
        args = (x, mmod3, y_da, y_ml, w_a, w_m) + ffn_args
    return pl.pallas_call(
        body,
        grid=(bsz, s // tm),
        in_specs=in_specs,
        out_specs=out_specs,
        out_shape=out_shape,
        scratch_shapes=[pltpu.VMEM((tm, D_FF), BF16)],
        compiler_params=pltpu.CompilerParams(
            dimension_semantics=("arbitrary", "arbitrary"), vmem_limit_bytes=VMEM_LIMIT),
        name="ffn" if mix is None else "mix_ffn",
    )(*args)


def _group_norm_rope(u, gvec, bd, cos, sin):
    x2 = u * u
    ssq = jnp.concatenate(
        [jnp.dot(x2[:, j:j + MXU_TILE].astype(BF16), bd, preferred_element_type=F32)
         for j in range(0, u.shape[1], MXU_TILE)], axis=1)
    xn = (u * lax.rsqrt(ssq * (1.0 / DA_QK_DIM) + EPS)) * gvec
    half = ROPE_DIM // 2
    first_half = (lax.broadcasted_iota(jnp.int32, (1, LANES), 1) % DA_QK_DIM) < half
    outs = []
    for h in range(u.shape[1] // LANES):
        xh = xn[:, h * LANES:(h + 1) * LANES]
        up = pltpu.roll(xh, LANES - half, 1)
        dn = pltpu.roll(xh, half, 1)
        outs.append(xh * cos + jnp.where(first_half, up, dn) * sin)
    return jnp.concatenate(outs, axis=1)


def _chunk_scan(x, op, identity):
    pos = lax.broadcasted_iota(jnp.int32, x.shape, 1) & (ML_CHUNK - 1)
    d = 1
    while d < ML_CHUNK:
        x = op(x, jnp.where(pos >= d, pltpu.roll(x, d, 1), identity))
        d *= 2
    return x


def _inproj_kernel(x_ref, mod_ref, g_ref, wq_ref, wk_ref, wv_ref, wmqk_ref, wmv_ref, wmo_ref, wif_ref,
                   gq_ref, gk_ref, rope_ref, bd_ref, cw_ref, cb_ref, bif_ref,
                   q_out, k_out, v_out, mq_out, mk_out, mv_out, mo_out, gate_out, halo_ref):
    tm = x_ref.shape[1]
    si = pl.program_id(1)
    mod = mod_ref[0]
    nh = ML_HEADS
    nt = (((1,), (1,)), ((), ()))
    bd = bd_ref[...]
    cw = cw_ref[...]

    tails = jnp.where(si == 0, 0.0, halo_ref[...])
    tails = [tails[:, ci * CONV_COLS:(ci + 1) * CONV_COLS] for ci in range(2 * ML_WIDTH // CONV_COLS)]

    for r0 in range(0, tm, PROJ_SUB):
        rows = slice(r0, r0 + PROJ_SUB)
        hb = _mod_norm(x_ref[0, rows, :], g_ref[...], mod[0:1], mod[1:2]).astype(BF16)

        pre = lax.dot_general(wif_ref[...], hb, nt, preferred_element_type=F32)[0:2 * nh] + bif_ref[...]
        zf = pre[nh:2 * nh]
        log_f = jnp.minimum(zf, 0.0) - jnp.log1p(jnp.exp(-jnp.abs(zf)))
        b = _chunk_scan(log_f, jnp.add, 0.0)
        c = pre[0:nh] - b
        gate_out[0, :, rows] = jnp.concatenate(
            [c, b, _chunk_scan(c, jnp.maximum, -jnp.inf), jnp.zeros_like(c)], axis=0)

        cos, sin = rope_ref[rows, 0:LANES], rope_ref[rows, LANES:2 * LANES]

        def conv_chunk(ci):
            cols = slice(ci * CONV_COLS, (ci + 1) * CONV_COLS)
            u = jnp.dot(hb, wmqk_ref[:, cols], preferred_element_type=F32)
            ext = jnp.concatenate([tails[ci], u], axis=0)
            tails[ci] = u[PROJ_SUB - CONV_HALO:, :]
            prev = pltpu.roll(ext, 1, 0)
            near = ext * cw[3:4, cols] + prev * cw[2:3, cols]
            far = ext * cw[1:2, cols] + prev * cw[0:1, cols]
            acc = cb_ref[:, cols] + near[CONV_HALO:, :] + pltpu.roll(far, 2, 0)[CONV_HALO:, :]
            act = acc * _sigmoid(acc)
            if ci < ML_WIDTH // CONV_COLS:
                mq_out[0, rows, cols] = (act * (ML_DIM ** -0.5)).astype(BF16)
            else:
                mk_out[0, rows, ci * CONV_COLS - ML_WIDTH:(ci + 1) * CONV_COLS - ML_WIDTH] = act.astype(BF16)

        chunks = iter(range(2 * ML_WIDTH // CONV_COLS))
        per_slot = 2 * ML_WIDTH // CONV_COLS // 4
        uq = jnp.dot(hb, wq_ref[...], preferred_element_type=F32)
        for _ in range(per_slot):
            conv_chunk(next(chunks))
        q_out[0, rows, :] = _group_norm_rope(uq, gq_ref[...], bd, cos, sin).astype(BF16)
        uk = jnp.dot(hb, wk_ref[...], preferred_element_type=F32)
        for _ in range(per_slot):
            conv_chunk(next(chunks))
        k_out[0, rows, :] = _group_norm_rope(uk, gk_ref[...], bd, cos, sin).astype(BF16)
        uv = jnp.dot(hb, wv_ref[...], preferred_element_type=F32)
        umv_t = lax.dot_general(wmv_ref[...], hb, nt, preferred_element_type=F32)
        for _ in range(per_slot):
            conv_chunk(next(chunks))
        v_out[0, rows, :] = uv.astype(BF16)
        mv_out[0, :, rows] = umv_t.astype(BF16)
        umo = jnp.dot(hb, wmo_ref[...], preferred_element_type=F32)
        for _ in range(per_slot):
            conv_chunk(next(chunks))
        mo_out[0, rows, :] = _sigmoid(umo).astype(BF16)

    halo_ref[...] = jnp.concatenate(tails, axis=1)


def _inproj(x, mod3, g, wq, wk, wv, wmqk, wmv, wmo, wif, gq, gk, rope, bd, cw, cb, bif):
    bsz, s, d = x.shape
    tm = PROJ_TM
    tok = lambda width, dt: jax.ShapeDtypeStruct((bsz, s, width), dt)
    tok_spec = lambda width: pl.BlockSpec((1, tm, width), lambda b, i: (b, i, 0))
    return pl.pallas_call(
        _inproj_kernel,
        grid=(bsz, s // tm),
        in_specs=[tok_spec(d),
                  pl.BlockSpec((1, 3, d), lambda b, i: (b, 0, 0)),
                  _const_spec((1, d)),
                  _const_spec(wq.shape), _const_spec(wk.shape), _const_spec(wv.shape),
                  _const_spec(wmqk.shape), _const_spec(wmv.shape), _const_spec(wmo.shape),
                  _const_spec(wif.shape),
                  _const_spec(gq.shape), _const_spec(gk.shape),
                  pl.BlockSpec((tm, 2 * LANES), lambda b, i: (i, 0)),
                  _const_spec(bd.shape), _const_spec(cw.shape), _const_spec(cb.shape),
                  _const_spec(bif.shape)],
        out_specs=[tok_spec(DA_WIDTH), tok_spec(DA_WIDTH), tok_spec(DA_WIDTH),
                   tok_spec(ML_WIDTH), tok_spec(ML_WIDTH),
                   pl.BlockSpec((1, ML_WIDTH, tm), lambda b, i: (b, 0, i)),
                   tok_spec(ML_WIDTH),
                   pl.BlockSpec((1, GATE_ROWS, tm), lambda b, i: (b, 0, i))],
        out_shape=[tok(DA_WIDTH, BF16), tok(DA_WIDTH, BF16), tok(DA_WIDTH, BF16),
                   tok(ML_WIDTH, BF16), tok(ML_WIDTH, BF16),
                   jax.ShapeDtypeStruct((bsz, ML_WIDTH, s), BF16),
                   tok(ML_WIDTH, BF16),
                   jax.ShapeDtypeStruct((bsz, GATE_ROWS, s), F32)],
        scratch_shapes=[pltpu.VMEM((CONV_HALO, 2 * ML_WIDTH), F32)],
        compiler_params=pltpu.CompilerParams(
            dimension_semantics=("arbitrary", "arbitrary"), vmem_limit_bytes=VMEM_LIMIT),
        name="in_proj",
    )(x, mod3, g.reshape(1, d), wq, wk, wv, wmqk, wmv, wmo, wif, gq, gk, rope, bd, cw, cb, bif)


def _attn_kernel(bounded_ref, q_ref, k_ref, v_ref, lam_ref, gout_ref, o_ref, qs_ref, m_ref, acc_ref):
    t = ATT_T
    u = t // 2
    lane = lax.broadcasted_iota(jnp.int32, (u, DA_V_DIM), 1)
    ones_cols = jnp.ones((t, DA_V_DIM), BF16)
    lv = lam_ref[...]
    lam = (jnp.exp(jnp.sum(lv[0:1] * lv[1:2], axis=1, keepdims=True))
           - jnp.exp(jnp.sum(lv[2:3] * lv[3:4], axis=1, keepdims=True)) + LAMBDA_INIT)

    def query_rows(qi, half, part):
        return pl.ds(pl.multiple_of(qi * (2 * t) + half * t + part * u, u), u)

    def stack_queries(qi):
        for half in range(2):
            for part in range(2):
                q = q_ref[0, query_rows(qi, half, part), :]
                zero = jnp.zeros_like(q)
                qs_ref[half, (2 * part) * u:(2 * part + 1) * u, :] = jnp.where(lane < DA_QK_DIM, q, zero)
                qs_ref[half, (2 * part + 1) * u:(2 * part + 2) * u, :] = jnp.where(lane >= DA_QK_DIM, q, zero)

    def keys(kb):
        start = pl.multiple_of(kb * t, t)
        return k_ref[0, pl.ds(start, t), :], jnp.concatenate([v_ref[0, pl.ds(start, t), :], ones_cols], axis=1)

    def piece(half, rows, ks, vs, causal_shift, first, stabilised):
        s = lax.dot_general(qs_ref[half, rows, :], ks, (((1,), (1,)), ((), ())), preferred_element_type=F32)
        if causal_shift is not None:
            row = lax.broadcasted_iota(jnp.int32, s.shape, 0) & (u - 1)
            col = lax.broadcasted_iota(jnp.int32, s.shape, 1)
            s = jnp.where(col <= row + causal_shift, s, -jnp.inf)
        if stabilised:
            m_new = jnp.max(s, axis=1, keepdims=True)
            if not first:
                m_prev = m_ref[half, rows, :]
                m_new = jnp.maximum(m_prev, m_new)
                alpha = jnp.exp2(m_prev - m_new)
            m_ref[half, rows, :] = m_new
            s = s - m_new
        pv = jnp.dot(jnp.exp2(s).astype(BF16), vs, preferred_element_type=F32)
        if first:
            acc_ref[half, rows, :] = pv
        elif stabilised:
            acc_ref[half, rows, :] = alpha * acc_ref[half, rows, :] + pv
        else:
            acc_ref[half, rows, :] += pv

    def block(half, kv, diagonal, first, stabilised):
        ks, vs = kv
        if diagonal:
            piece(half, slice(0, 2 * u), ks[0:u], vs[0:u], 0, first, stabilised)
            piece(half, slice(2 * u, 4 * u), ks, vs, u, first, stabilised)
        else:
            piece(half, slice(0, 4 * u), ks, vs, None, first, stabilised)

    def tile_pair(qi, stabilised):
        stack_queries(qi)
        kv = keys(2 * qi)
        block(0, kv, True, True, stabilised)
        block(1, kv, False, True, stabilised)
        block(1, keys(2 * qi + 1), True, False, stabilised)

        def body(pair, carry):
            for j in range(2):
                kv = keys(2 * pair + j)
                block(0, kv, False, False, stabilised)
                block(1, kv, False, False, stabilised)
            return carry

        lax.fori_loop(0, qi, body, 0)

        for half in range(2):
            for part in range(2):
                rows = slice(2 * part * u, (2 * part + 2) * u)
                o = acc_ref[half, rows, 0:DA_V_DIM] / acc_ref[half, rows, DA_V_DIM:2 * DA_V_DIM]
                od = o[0:u] - lam * o[u:2 * u]
                ms = jnp.mean(od * od, axis=-1, keepdims=True)
                o_ref[0, query_rows(qi, half, part), :] = (
                    (od * lax.rsqrt(ms + EPS)) * gout_ref[...] * (1.0 - LAMBDA_INIT)).astype(BF16)

    def run(stabilised):
        def body(qi, carry):
            tile_pair(qi, stabilised)
            return carry

        lax.fori_loop(0, q_ref.shape[1] // (2 * t), body, 0)

    @pl.when(bounded_ref[0] != 0)
    def _():
        run(False)

    @pl.when(bounded_ref[0] == 0)
    def _():
        run(True)


def _attention(bounded, q, k, v, lam_vecs, g_out):
    bsz, s, _ = q.shape
    t = ATT_T
    full = pl.BlockSpec((1, s, DA_V_DIM), lambda b, h: (b, 0, h))
    return pl.pallas_call(
        _attn_kernel,
        grid=(bsz, DA_HEADS),
        in_specs=[pl.BlockSpec(memory_space=pltpu.SMEM),
                  full, full, full, _const_spec(lam_vecs.shape), _const_spec((1, DA_V_DIM))],
        out_specs=full,
        out_shape=jax.ShapeDtypeStruct((bsz, s, DA_WIDTH), BF16),
        scratch_shapes=[pltpu.VMEM((2, 2 * t, DA_V_DIM), BF16),
                        pltpu.VMEM((2, 2 * t, 1), F32),
                        pltpu.VMEM((2, 2 * t, 2 * DA_V_DIM), F32)],
        compiler_params=pltpu.CompilerParams(
            dimension_semantics=("arbitrary", "arbitrary"), vmem_limit_bytes=VMEM_LIMIT),
        name="diff_attention",
    )(bounded, q, k, v, lam_vecs, g_out.reshape(1, DA_V_DIM))


def _mlstm_kernel(q_ref, k_ref, vt_ref, o_ref, gr_ref, gout_ref, y_ref, ct_ref, m_ref):
    ln = ML_CHUNK
    nh = ML_HEADS
    nt = (((1,), (1,)), ((), ()))

    @pl.when(pl.program_id(1) == 0)
    def _():
        ct_ref[...] = jnp.zeros(ct_ref.shape, F32)
        m_ref[...] = jnp.zeros(m_ref.shape, F32)

    row = lax.broadcasted_iota(jnp.int32, (ln, ln), 0)
    col = lax.broadcasted_iota(jnp.int32, (ln, ln), 1)
    lower = col <= row
    ones_row = (lax.broadcasted_iota(jnp.int32, (ML_STATE_ROWS - ML_DIM, ln), 0) == 0).astype(BF16)

    for r in range(ML_ROWS):
        gr = gr_ref[r]
        c, b = gr[0:nh], gr[nh:2 * nh]
        m_prev = jnp.concatenate([m_ref[r, 0:nh, :]] * (ln // LANES), axis=1)
        m_run = jnp.maximum(gr[2 * nh:3 * nh], m_prev)
        m_last = m_run[:, ln - 1:ln]
        e_inter = jnp.exp(m_prev - m_run)
        e_floor = jnp.exp(-(b + m_run))
        w_in = jnp.exp(c - m_last)
        decay = e_inter[:, ln - 1:ln]
        m_ref[r, 0:nh, :] = jnp.broadcast_to(b[:, ln - 1:ln] + m_last, (nh, LANES))
        m_run_cols = jnp.concatenate([m_run, m_run], axis=0).T

        for h in range(nh):
            lanes = slice(h * ML_DIM, (h + 1) * ML_DIM)
            q = q_ref[r, :, lanes]
            k = k_ref[r, :, lanes]
            vt_aug = jnp.concatenate([vt_ref[r, lanes, :], ones_row], axis=0)
            s = lax.dot_general(q, k, nt, preferred_element_type=F32)
            gate = jnp.exp(jnp.where(lower, c[h:h + 1, :] - m_run_cols[:, h:h + 1], -jnp.inf))
            w = (s * gate).astype(BF16)
            state_t = ct_ref[r, h]
            full_t = (e_inter[h:h + 1, :]
                      * lax.dot_general(state_t.astype(BF16), q, nt, preferred_element_type=F32)
                      + lax.dot_general(vt_aug, w, nt, preferred_element_type=F32))
            den = full_t[ML_DIM:ML_DIM + 1, :]
            hh_t = full_t[0:ML_DIM, :] / jnp.maximum(jnp.abs(den), e_floor[h:h + 1, :])
            hn_t = hh_t * lax.rsqrt(jnp.mean(hh_t * hh_t, axis=0, keepdims=True) + EPS)
            y_ref[r, :, lanes] = (o_ref[r, :, lanes].astype(F32) * (hn_t.T * gout_ref[:, lanes])).astype(BF16)

            vw = (vt_aug.astype(F32) * w_in[h:h + 1, :]).astype(BF16)
            ct_ref[r, h] = decay[h:h + 1, :] * state_t + jnp.dot(vw, k, preferred_element_type=F32)


def _mlstm(q, k, vt, o, gates_row, g_out):
    bsz, s, _ = q.shape
    ln = ML_CHUNK
    rows = ML_ROWS
    tok = pl.BlockSpec((rows, ln, ML_WIDTH), lambda b, c: (b, c, 0))
    return pl.pallas_call(
        _mlstm_kernel,
        grid=(bsz // rows, s // ln),
        in_specs=[tok, tok, pl.BlockSpec((rows, ML_WIDTH, ln), lambda b, c: (b, 0, c)), tok,
                  pl.BlockSpec((rows, GATE_ROWS, ln), lambda b, c: (b, 0, c)),
                  _const_spec((1, ML_WIDTH))],
        out_specs=tok,
        out_shape=jax.ShapeDtypeStruct((bsz, s, ML_WIDTH), BF16),
        scratch_shapes=[pltpu.VMEM((rows, ML_HEADS, ML_STATE_ROWS, ML_DIM), F32),
                        pltpu.VMEM((rows, F32_SUBLANES, LANES), F32)],
        compiler_params=pltpu.CompilerParams(
            dimension_semantics=("arbitrary", "arbitrary"), vmem_limit_bytes=VMEM_LIMIT),
        name="mlstm",
    )(q, k, vt, o, gates_row, g_out.reshape(1, ML_WIDTH))


def _rope_tables(s):
    half = ROPE_DIM // 2
    pos = jnp.arange(s, dtype=F32)
    inv_freq = ROPE_THETA ** (-jnp.arange(0, ROPE_DIM, 2, dtype=F32) / ROPE_DIM)
    ang = pos[:, None] * inv_freq[None, :]
    basis = jnp.concatenate([jnp.cos(ang), jnp.sin(ang), jnp.ones((s, 1), F32)], axis=1)
    g = jnp.arange(LANES) % DA_QK_DIM
    j = jnp.arange(2 * half + 1)[:, None]
    cos_sel = jnp.where(g < ROPE_DIM, j == g % half, j == 2 * half).astype(F32)
    sin_sel = ((g >= half) & (g < ROPE_DIM) & (j == g)).astype(F32) - ((g < half) & (j == half + g)).astype(F32)
    sel = jnp.concatenate([cos_sel, sin_sel], axis=1)
    return jnp.dot(basis, sel, precision=lax.Precision.HIGHEST)


def kernel(x, c, w_ada, b_ada, g_norm, ffn1_w12, ffn1_w3, w_in, conv_w, conv_b, b_igate, b_fgate,
           g_qnorm, g_knorm, lambda_qk, g_da_out, g_ml_out, w_out, ffn2_w12, ffn2_w3):
    bsz, s, d = x.shape
    assert w_ada.shape[0] == 1, "one layer (depth 1) is implemented"
    l = 0
    mod = _ada(c, w_ada[l], b_ada[l]).reshape(bsz, 3, 3, d)

    x, w12_2, w3_2, wo = _ffn(
        x, mod[:, 0], g_norm[l, 0], ffn1_w12[l].astype(BF16), ffn1_w3[l].astype(BF16),
        casts=((ffn2_w12, l), (ffn2_w3, l), (w_out, l)))
    wb = w_in[l].astype(BF16)
    o0 = 0
    parts = []
    for width in (DA_WIDTH, DA_WIDTH, DA_WIDTH, 2 * ML_WIDTH, ML_WIDTH, ML_WIDTH, 2 * ML_HEADS):
        parts.append(wb[:, o0:o0 + width])
        o0 += width
    wq, wk, wv, wmqk, wmv, wmo, wif = parts
    wif = jnp.pad(wif.T, ((0, BF16_SUBLANES - 2 * ML_HEADS), (0, 0)))
    wmv = wmv.T
    groups = DA_WIDTH // DA_QK_DIM
    q_gain = g_qnorm[l] * (DA_QK_DIM ** -0.5)
    gq = jnp.tile(q_gain * LOG2E, groups).reshape(1, DA_WIDTH)
    score_bound = BF16_ROUNDING_SLACK * DA_QK_DIM * jnp.max(jnp.abs(q_gain)) * jnp.max(jnp.abs(g_knorm[l]))
    bounded = (score_bound <= SAFE_SCORE_BOUND).astype(jnp.int32).reshape(1)
    gk = jnp.tile(g_knorm[l], groups).reshape(1, DA_WIDTH)
    rope = _rope_tables(s)
    gid = jnp.arange(MXU_TILE) // DA_QK_DIM
    bd = (gid[:, None] == gid[None, :]).astype(BF16)
    bif = jnp.concatenate([b_igate[l], b_fgate[l]]).reshape(2 * ML_HEADS, 1)
    da_q, da_k, da_v, ml_q, ml_k, ml_v, ml_o, gates = _inproj(
        x, mod[:, 1], g_norm[l, 1], wq, wk, wv, wmqk, wmv, wmo, wif, gq, gk, rope, bd,
        conv_w[l], conv_b[l].reshape(1, 2 * ML_WIDTH), bif)

    y_da = _attention(bounded, da_q, da_k, da_v, lambda_qk[l], g_da_out[l])
    y_ml = _mlstm(ml_q, ml_k, ml_v, ml_o, gates, g_ml_out[l])

    return _ffn(x, mod[:, 2], g_norm[l, 2], w12_2, w3_2,
                mix=(mod[:, 1], y_da, y_ml, wo[:DA_WIDTH], wo[DA_WIDTH:]))
```

```python
import math

import jax
import jax.numpy as jnp
from jax import lax
from jax.experimental import pallas as pl
from jax.experimental.pallas import tpu as pltpu

F32 = jnp.float32
BF16 = jnp.bfloat16

DA_HEADS = 4
DA_QK_DIM = 64
DA_V_DIM = 2 * DA_QK_DIM
DA_WIDTH = DA_HEADS * DA_V_DIM
ML_HEADS = 4
ML_DIM = 128
ML_WIDTH = ML_HEADS * ML_DIM
ROPE_THETA = 500000.0
ROPE_DIM = DA_QK_DIM // 4
D_FF = 2816
CONV_K = 4
EPS = 1e-6
LAMBDA_INIT = 0.8 - 0.6 * math.exp(-0.3 * 0)
LOG2E = 1.4426950408889634

LANES = 128
F32_SUBLANES = 8
BF16_SUBLANES = 16
MXU_TILE = 256
VMEM_LIMIT = 56 * 1024 * 1024

ADA_TN = 1024
FFN_TM = 1024
FFN_SUB = 256
PROJ_TM = 1024
PROJ_SUB = 256
CONV_COLS = 256
ATT_T = 512
SAFE_SCORE_BOUND = 40.0
BF16_ROUNDING_SLACK = 1.05
ML_CHUNK = 256
ML_ROWS = 8
ML_STATE_ROWS = ML_DIM + BF16_SUBLANES
GATE_ROWS = 16
CONV_HALO = F32_SUBLANES
assert CONV_K == 4


def _const_spec(shape):
    nd = len(shape)
    return pl.BlockSpec(shape, lambda *_: (0,) * nd, pipeline_mode=pl.Buffered(1))


def _sigmoid(x):
    return 1.0 / (1.0 + jnp.exp(-x))


def _mod_norm(x, g, shift, scale):
    ms = jnp.mean(x * x, axis=-1, keepdims=True)
    return (x * lax.rsqrt(ms + EPS)) * (g * (1.0 + scale)) + shift


def _ada_kernel(c_ref, w_ref, b_ref, o_ref):
    c = c_ref[...]
    cs = (c * _sigmoid(c)).astype(BF16)
    o_ref[...] = jnp.dot(cs, w_ref[...].astype(BF16), preferred_element_type=F32) + b_ref[...]


def _ada(c, w_ada, b_ada):
    bsz, d = c.shape
    n = w_ada.shape[1]
    tn = ADA_TN
    return pl.pallas_call(
        _ada_kernel,
        grid=(n // tn,),
        in_specs=[pl.BlockSpec((bsz, d), lambda j: (0, 0)),
                  pl.BlockSpec((d, tn), lambda j: (0, j)),
                  pl.BlockSpec((1, tn), lambda j: (0, j))],
        out_specs=pl.BlockSpec((bsz, tn), lambda j: (0, j)),
        out_shape=jax.ShapeDtypeStruct((bsz, n), F32),
        compiler_params=pltpu.CompilerParams(dimension_semantics=("arbitrary",)),
        name="adaln_mod",
    )(c, w_ada, b_ada.reshape(1, n))


def _ffn_body(x, mod_ref, g_ref, w12_ref, w3_ref, o_ref, act_ref):
    mod = mod_ref[0]
    hb = _mod_norm(x, g_ref[...], mod[0:1], mod[1:2]).astype(BF16)
    for c in range(D_FF // FFN_SUB):
        cols = slice(c * FFN_SUB, (c + 1) * FFN_SUB)
        a = jnp.dot(hb, w12_ref[:, cols], preferred_element_type=F32)
        b = jnp.dot(hb, w12_ref[:, D_FF + c * FFN_SUB:D_FF + (c + 1) * FFN_SUB], preferred_element_type=F32)
        act_ref[:, cols] = (a * _sigmoid(a) * b).astype(BF16)
    y = jnp.dot(act_ref[...], w3_ref[...], preferred_element_type=F32)
    o_ref[0] = x + (0.5 * (1.0 + mod[2:3])) * y


def _ffn_kernel(x_ref, mod_ref, g_ref, w12_ref, w3_ref, *rest):
    n = (len(rest) - 2) // 2
    casts_in, o_ref, casts_out, act_ref = rest[:n], rest[n], rest[n + 1:2 * n + 1], rest[-1]
    _ffn_body(x_ref[0], mod_ref, g_ref, w12_ref, w3_ref, o_ref, act_ref)
    for src, dst in zip(casts_in, casts_out):
        dst[...] = src[...].astype(BF16)


def _mix_ffn_kernel(x_ref, mmod_ref, ya_ref, ym_ref, wa_ref, wm_ref,
                    mod_ref, g_ref, w12_ref, w3_ref, o_ref, act_ref):
    y = (jnp.dot(ya_ref[0], wa_ref[...], preferred_element_type=F32)
         + jnp.dot(ym_ref[0], wm_ref[...], preferred_element_type=F32))
    x = x_ref[0] + (1.0 + mmod_ref[0][2:3]) * y
    _ffn_body(x, mod_ref, g_ref, w12_ref, w3_ref, o_ref, act_ref)


def _ffn(x, mod3, g, w12, w3, mix=None, casts=()):
    bsz, s, d = x.shape
    tm = FFN_TM
    steps = bsz * (s // tm)
    tok_spec = lambda width: pl.BlockSpec((1, tm, width), lambda b, i: (b, i, 0))
    mod_spec = pl.BlockSpec((1, 3, d), lambda b, i: (b, 0, 0))
    ffn_specs = [mod_spec, _const_spec((1, d)),
                 _const_spec(w12.shape), _const_spec(w3.shape)]
    ffn_args = (mod3, g.reshape(1, d), w12, w3)

    def slab(w, layer=None):
        rows, cols = w.shape[-2:]
        lead = () if layer is None else (layer,)
        if rows % (steps * BF16_SUBLANES) == 0:
            return pl.BlockSpec((None,) * len(lead) + (rows // steps, cols),
                                lambda b, i: lead + (b * (s // tm) + i, 0))
        return pl.BlockSpec((None,) * len(lead) + (rows // bsz, cols), lambda b, i: lead + (b, 0))

    out_specs, out_shape = tok_spec(d), jax.ShapeDtypeStruct(x.shape, F32)
    if mix is None:
        body = _ffn_kernel
        in_specs = [tok_spec(d)] + ffn_specs + [slab(w, layer) for w, layer in casts]
        args = (x,) + ffn_args + tuple(w for w, _ in casts)
        if casts:
            cast_shapes = [jax.ShapeDtypeStruct(w.shape[-2:], BF16) for w, _ in casts]
            out_specs = [out_specs] + [slab(c) for c in cast_shapes]
            out_shape = [out_shape] + cast_shapes
    else:
        mmod3, y_da, y_ml, w_a, w_m = mix
        body = _mix_ffn_kernel
        in_specs = [tok_spec(d), mod_spec, tok_spec(DA_WIDTH), tok_spec(ML_WIDTH),
                    _const_spec(w_a.shape), _const_spec(w_m.shape)] + ffn_specs
        args = (x, mmod3, y_da, y_ml, w_a, w_m) + ffn_args
    return pl.pallas_call(
        body,
        grid=(bsz, s // tm),
        in_specs=in_specs,
        out_specs=out_specs,
        out_shape=out_shape,
        scratch_shapes=[pltpu.VMEM((tm, D_FF), BF16)],
        compiler_params=pltpu.CompilerParams(
            dimension_semantics=("arbitrary", "arbitrary"), vmem_limit_bytes=VMEM_LIMIT),
        name="ffn" if mix is None else "mix_ffn",
    )(*args)


def _group_norm_rope(u, gvec, bd, cos, sin):
    x2 = u * u
    ssq = jnp.concatenate(
        [jnp.dot(x2[:, j:j + MXU_TILE].astype(BF16), bd, preferred_element_type=F32)
         for j in range(0, u.shape[1], MXU_TILE)], axis=1)
    xn = (u * lax.rsqrt(ssq * (1.0 / DA_QK_DIM) + EPS)) * gvec
    half = ROPE_DIM // 2
    first_half = (lax.broadcasted_iota(jnp.int32, (1, LANES), 1) % DA_QK_DIM) < half
    outs = []
    for h in range(u.shape[1] // LANES):
        xh = xn[:, h * LANES:(h + 1) * LANES]
        up = pltpu.roll(xh, LANES - half, 1)
        dn = pltpu.roll(xh, half, 1)
        outs.append(xh * cos + jnp.where(first_half, up, dn) * sin)
    return jnp.concatenate(outs, axis=1)


def _chunk_scan(x, op, identity):
    pos = lax.broadcasted_iota(jnp.int32, x.shape, 1) & (ML_CHUNK - 1)
    d = 1
    while d < ML_CHUNK:
        x = op(x, jnp.where(pos >= d, pltpu.roll(x, d, 1), identity))
        d *= 2
    return x


def _inproj_kernel(x_ref, mod_ref, g_ref, wq_ref, wk_ref, wv_ref, wmqk_ref, wmv_ref, wmo_ref, wif_ref,
                   gq_ref, gk_ref, rope_ref, bd_ref, cw_ref, cb_ref, bif_ref,
                   q_out, k_out, v_out, mq_out, mk_out, mv_out, mo_out, gate_out, halo_ref):
    tm = x_ref.shape[1]
    si = pl.program_id(1)
    mod = mod_ref[0]
    nh = ML_HEADS
    nt = (((1,), (1,)), ((), ()))
    bd = bd_ref[...]
    cw = cw_ref[...]

    tails = jnp.where(si == 0, 0.0, halo_ref[...])
    tails = [tails[:, ci * CONV_COLS:(ci + 1) * CONV_COLS] for ci in range(2 * ML_WIDTH // CONV_COLS)]

    for r0 in range(0, tm, PROJ_SUB):
        rows = slice(r0, r0 + PROJ_SUB)
        hb = _mod_norm(x_ref[0, rows, :], g_ref[...], mod[0:1], mod[1:2]).astype(BF16)

        pre = lax.dot_general(wif_ref[...], hb, nt, preferred_element_type=F32)[0:2 * nh] + bif_ref[...]
        zf = pre[nh:2 * nh]
        log_f = jnp.minimum(zf, 0.0) - jnp.log1p(jnp.exp(-jnp.abs(zf)))
        b = _chunk_scan(log_f, jnp.add, 0.0)
        c = pre[0:nh] - b
        gate_out[0, :, rows] = jnp.concatenate(
            [c, b, _chunk_scan(c, jnp.maximum, -jnp.inf), jnp.zeros_like(c)], axis=0)

        cos, sin = rope_ref[rows, 0:LANES], rope_ref[rows, LANES:2 * LANES]

        def conv_chunk(ci):
            cols = slice(ci * CONV_COLS, (ci + 1) * CONV_COLS)
            u = jnp.dot(hb, wmqk_ref[:, cols], preferred_element_type=F32)
            ext = jnp.concatenate([tails[ci], u], axis=0)
            tails[ci] = u[PROJ_SUB - CONV_HALO:, :]
            prev = pltpu.roll(ext, 1, 0)
            near = ext * cw[3:4, cols] + prev * cw[2:3, cols]
            far = ext * cw[1:2, cols] + prev * cw[0:1, cols]
            acc = cb_ref[:, cols] + near[CONV_HALO:, :] + pltpu.roll(far, 2, 0)[CONV_HALO:, :]
            act = acc * _sigmoid(acc)
            if ci < ML_WIDTH // CONV_COLS:
                mq_out[0, rows, cols] = (act * (ML_DIM ** -0.5)).astype(BF16)
            else:
                mk_out[0, rows, ci * CONV_COLS - ML_WIDTH:(ci + 1) * CONV_COLS - ML_WIDTH] = act.astype(BF16)

        chunks = iter(range(2 * ML_WIDTH // CONV_COLS))
        per_slot = 2 * ML_WIDTH // CONV_COLS // 4
        uq = jnp.dot(hb, wq_ref[...], preferred_element_type=F32)
        for _ in range(per_slot):
            conv_chunk(next(chunks))
        uk = jnp.dot(hb, wk_ref[...], preferred_element_type=F32)
        q_out[0, rows, :] = _group_norm_rope(uq, gq_ref[...], bd, cos, sin).astype(BF16)
        for _ in range(per_slot):
            conv_chunk(next(chunks))
        uv = jnp.dot(hb, wv_ref[...], preferred_element_type=F32)
        k_out[0, rows, :] = _group_norm_rope(uk, gk_ref[...], bd, cos, sin).astype(BF16)
        umv_t = lax.dot_general(wmv_ref[...], hb, nt, preferred_element_type=F32)
        for _ in range(per_slot):
            conv_chunk(next(chunks))
        umo = jnp.dot(hb, wmo_ref[...], preferred_element_type=F32)
        v_out[0, rows, :] = uv.astype(BF16)
        mv_out[0, :, rows] = umv_t.astype(BF16)
        for _ in range(per_slot):
            conv_chunk(next(chunks))
        mo_out[0, rows, :] = _sigmoid(umo).astype(BF16)

    halo_ref[...] = jnp.concatenate(tails, axis=1)


def _inproj(x, mod3, g, wq, wk, wv, wmqk, wmv, wmo, wif, gq, gk, rope, bd, cw, cb, bif):
    bsz, s, d = x.shape
    tm = PROJ_TM
    tok = lambda width, dt: jax.ShapeDtypeStruct((bsz, s, width), dt)
    tok_spec = lambda width: pl.BlockSpec((1, tm, width), lambda b, i: (b, i, 0))
    return pl.pallas_call(
        _inproj_kernel,
        grid=(bsz, s // tm),
        in_specs=[tok_spec(d),
                  pl.BlockSpec((1, 3, d), lambda b, i: (b, 0, 0)),
                  _const_spec((1, d)),
                  _const_spec(wq.shape), _const_spec(wk.shape), _const_spec(wv.shape),
                  _const_spec(wmqk.shape), _const_spec(wmv.shape), _const_spec(wmo.shape),
                  _const_spec(wif.shape),
                  _const_spec(gq.shape), _const_spec(gk.shape),
                  pl.BlockSpec((tm, 2 * LANES), lambda b, i: (i, 0)),
                  _const_spec(bd.shape), _const_spec(cw.shape), _const_spec(cb.shape),
                  _const_spec(bif.shape)],
        out_specs=[tok_spec(DA_WIDTH), tok_spec(DA_WIDTH), tok_spec(DA_WIDTH),
                   tok_spec(ML_WIDTH), tok_spec(ML_WIDTH),
                   pl.BlockSpec((1, ML_WIDTH, tm), lambda b, i: (b, 0, i)),
                   tok_spec(ML_WIDTH),
                   pl.BlockSpec((1, GATE_ROWS, tm), lambda b, i: (b, 0, i))],
        out_shape=[tok(DA_WIDTH, BF16), tok(DA_WIDTH, BF16), tok(DA_WIDTH, BF16),
                   tok(ML_WIDTH, BF16), tok(ML_WIDTH, BF16),
                   jax.ShapeDtypeStruct((bsz, ML_WIDTH, s), BF16),
                   tok(ML_WIDTH, BF16),
                   jax.ShapeDtypeStruct((bsz, GATE_ROWS, s), F32)],
        scratch_shapes=[pltpu.VMEM((CONV_HALO, 2 * ML_WIDTH), F32)],
        compiler_params=pltpu.CompilerParams(
            dimension_semantics=("arbitrary", "arbitrary"), vmem_limit_bytes=VMEM_LIMIT),
        name="in_proj",
    )(x, mod3, g.reshape(1, d), wq, wk, wv, wmqk, wmv, wmo, wif, gq, gk, rope, bd, cw, cb, bif)


def _attn_kernel(bounded_ref, q_ref, k_ref, v_ref, lam_ref, gout_ref, o_ref, qs_ref, m_ref, acc_ref):
    t = ATT_T
    u = t // 2
    lane = lax.broadcasted_iota(jnp.int32, (u, DA_V_DIM), 1)
    ones_cols = jnp.ones((t, DA_V_DIM), BF16)
    lv = lam_ref[...]
    lam = (jnp.exp(jnp.sum(lv[0:1] * lv[1:2], axis=1, keepdims=True))
           - jnp.exp(jnp.sum(lv[2:3] * lv[3:4], axis=1, keepdims=True)) + LAMBDA_INIT)

    def query_rows(qi, half, part):
        return pl.ds(pl.multiple_of(qi * (2 * t) + half * t + part * u, u), u)

    def stack_queries(qi):
        for half in range(2):
            for part in range(2):
                q = q_ref[0, query_rows(qi, half, part), :]
                zero = jnp.zeros_like(q)
                qs_ref[half, (2 * part) * u:(2 * part + 1) * u, :] = jnp.where(lane < DA_QK_DIM, q, zero)
                qs_ref[half, (2 * part + 1) * u:(2 * part + 2) * u, :] = jnp.where(lane >= DA_QK_DIM, q, zero)

    def keys(kb):
        start = pl.multiple_of(kb * t, t)
        return k_ref[0, pl.ds(start, t), :], jnp.concatenate([v_ref[0, pl.ds(start, t), :], ones_cols], axis=1)

    def piece(half, rows, ks, vs, causal_shift, first, stabilised):
        s = lax.dot_general(qs_ref[half, rows, :], ks, (((1,), (1,)), ((), ())), preferred_element_type=F32)
        if causal_shift is not None:
            row = lax.broadcasted_iota(jnp.int32, s.shape, 0) & (u - 1)
            col = lax.broadcasted_iota(jnp.int32, s.shape, 1)
            s = jnp.where(col <= row + causal_shift, s, -jnp.inf)
        if stabilised:
            m_new = jnp.max(s, axis=1, keepdims=True)
            if not first:
                m_prev = m_ref[half, rows, :]
                m_new = jnp.maximum(m_prev, m_new)
                alpha = jnp.exp2(m_prev - m_new)
            m_ref[half, rows, :] = m_new
            s = s - m_new
        pv = jnp.dot(jnp.exp2(s).astype(BF16), vs, preferred_element_type=F32)
        if first:
            acc_ref[half, rows, :] = pv
        elif stabilised:
            acc_ref[half, rows, :] = alpha * acc_ref[half, rows, :] + pv
        else:
            acc_ref[half, rows, :] += pv

    def block(half, kv, diagonal, first, stabilised):
        ks, vs = kv
        if diagonal:
            piece(half, slice(0, 2 * u), ks[0:u], vs[0:u], 0, first, stabilised)
            piece(half, slice(2 * u, 4 * u), ks, vs, u, first, stabilised)
        else:
            piece(half, slice(0, 4 * u), ks, vs, None, first, stabilised)

    def tile_pair(qi, stabilised):
        stack_queries(qi)
        kv = keys(2 * qi)
        block(0, kv, True, True, stabilised)
        block(1, kv, False, True, stabilised)
        block(1, keys(2 * qi + 1), True, False, stabilised)

        def body(pair, carry):
            for j in range(2):
                kv = keys(2 * pair + j)
                block(0, kv, False, False, stabilised)
                block(1, kv, False, False, stabilised)
            return carry

        lax.fori_loop(0, qi, body, 0)

        for half in range(2):
            for part in range(2):
                rows = slice(2 * part * u, (2 * part + 2) * u)
                o = acc_ref[half, rows, 0:DA_V_DIM] / acc_ref[half, rows, DA_V_DIM:2 * DA_V_DIM]
                od = o[0:u] - lam * o[u:2 * u]
                ms = jnp.mean(od * od, axis=-1, keepdims=True)
                o_ref[0, query_rows(qi, half, part), :] = (
                    (od * lax.rsqrt(ms + EPS)) * gout_ref[...] * (1.0 - LAMBDA_INIT)).astype(BF16)

    def run(stabilised):
        def body(qi, carry):
            tile_pair(qi, stabilised)
            return carry

        lax.fori_loop(0, q_ref.shape[1] // (2 * t), body, 0)

    @pl.when(bounded_ref[0] != 0)
    def _():
        run(False)

    @pl.when(bounded_ref[0] == 0)
    def _():
        run(True)


def _attention(bounded, q, k, v, lam_vecs, g_out):
    bsz, s, _ = q.shape
    t = ATT_T
    full = pl.BlockSpec((1, s, DA_V_DIM), lambda b, h: (b, 0, h))
    return pl.pallas_call(
        _attn_kernel,
        grid=(bsz, DA_HEADS),
        in_specs=[pl.BlockSpec(memory_space=pltpu.SMEM),
                  full, full, full, _const_spec(lam_vecs.shape), _const_spec((1, DA_V_DIM))],
        out_specs=full,
        out_shape=jax.ShapeDtypeStruct((bsz, s, DA_WIDTH), BF16),
        scratch_shapes=[pltpu.VMEM((2, 2 * t, DA_V_DIM), BF16),
                        pltpu.VMEM((2, 2 * t, 1), F32),
                        pltpu.VMEM((2, 2 * t, 2 * DA_V_DIM), F32)],
        compiler_params=pltpu.CompilerParams(
            dimension_semantics=("arbitrary", "arbitrary"), vmem_limit_bytes=VMEM_LIMIT),
        name="diff_attention",
    )(bounded, q, k, v, lam_vecs, g_out.reshape(1, DA_V_DIM))


def _mlstm_kernel(q_ref, k_ref, vt_ref, o_ref, gr_ref, gout_ref, y_ref, ct_ref, m_ref):
    ln = ML_CHUNK
    nh = ML_HEADS
    nt = (((1,), (1,)), ((), ()))

    @pl.when(pl.program_id(1) == 0)
    def _():
        ct_ref[...] = jnp.zeros(ct_ref.shape, F32)
        m_ref[...] = jnp.zeros(m_ref.shape, F32)

    row = lax.broadcasted_iota(jnp.int32, (ln, ln), 0)
    col = lax.broadcasted_iota(jnp.int32, (ln, ln), 1)
    lower = col <= row
    ones_row = (lax.broadcasted_iota(jnp.int32, (ML_STATE_ROWS - ML_DIM, ln), 0) == 0).astype(BF16)

    for r in range(ML_ROWS):
        gr = gr_ref[r]
        c, b = gr[0:nh], gr[nh:2 * nh]
        m_prev = jnp.concatenate([m_ref[r, 0:nh, :]] * (ln // LANES), axis=1)
        m_run = jnp.maximum(gr[2 * nh:3 * nh], m_prev)
        m_last = m_run[:, ln - 1:ln]
        e_inter = jnp.exp(m_prev - m_run)
        e_floor = jnp.exp(-(b + m_run))
        w_in = jnp.exp(c - m_last)
        decay = e_inter[:, ln - 1:ln]
        m_ref[r, 0:nh, :] = jnp.broadcast_to(b[:, ln - 1:ln] + m_last, (nh, LANES))
        m_run_cols = jnp.concatenate([m_run, m_run], axis=0).T

        for h in range(nh):
            lanes = slice(h * ML_DIM, (h + 1) * ML_DIM)
            q = q_ref[r, :, lanes]
            k = k_ref[r, :, lanes]
            vt_aug = jnp.concatenate([vt_ref[r, lanes, :], ones_row], axis=0)
            s = lax.dot_general(q, k, nt, preferred_element_type=F32)
            gate = jnp.exp(jnp.where(lower, c[h:h + 1, :] - m_run_cols[:, h:h + 1], -jnp.inf))
            w = (s * gate).astype(BF16)
            state_t = ct_ref[r, h]
            full_t = (e_inter[h:h + 1, :]
                      * lax.dot_general(state_t.astype(BF16), q, nt, preferred_element_type=F32)
                      + lax.dot_general(vt_aug, w, nt, preferred_element_type=F32))
            den = full_t[ML_DIM:ML_DIM + 1, :]
            hh_t = full_t[0:ML_DIM, :] / jnp.maximum(jnp.abs(den), e_floor[h:h + 1, :])
            hn_t = hh_t * lax.rsqrt(jnp.mean(hh_t * hh_t, axis=0, keepdims=True) + EPS)
            y_ref[r, :, lanes] = (o_ref[r, :, lanes].astype(F32) * (hn_t.T * gout_ref[:, lanes])).astype(BF16)

            vw = (vt_aug.astype(F32) * w_in[h:h + 1, :]).astype(BF16)
            ct_ref[r, h] = decay[h:h + 1, :] * state_t + jnp.dot(vw, k, preferred_element_type=F32)


def _mlstm(q, k, vt, o, gates_row, g_out):
    bsz, s, _ = q.shape
    ln = ML_CHUNK
    rows = ML_ROWS
    tok = pl.BlockSpec((rows, ln, ML_WIDTH), lambda b, c: (b, c, 0))
    return pl.pallas_call(
        _mlstm_kernel,
        grid=(bsz // rows, s // ln),
        in_specs=[tok, tok, pl.BlockSpec((rows, ML_WIDTH, ln), lambda b, c: (b, 0, c)), tok,
                  pl.BlockSpec((rows, GATE_ROWS, ln), lambda b, c: (b, 0, c)),
                  _const_spec((1, ML_WIDTH))],
        out_specs=tok,
        out_shape=jax.ShapeDtypeStruct((bsz, s, ML_WIDTH), BF16),
        scratch_shapes=[pltpu.VMEM((rows, ML_HEADS, ML_STATE_ROWS, ML_DIM), F32),
                        pltpu.VMEM((rows, F32_SUBLANES, LANES), F32)],
        compiler_params=pltpu.CompilerParams(
            dimension_semantics=("arbitrary", "arbitrary"), vmem_limit_bytes=VMEM_LIMIT),
        name="mlstm",
    )(q, k, vt, o, gates_row, g_out.reshape(1, ML_WIDTH))


def _rope_tables(s):
    half = ROPE_DIM // 2
    pos = jnp.arange(s, dtype=F32)
    inv_freq = ROPE_THETA ** (-jnp.arange(0, ROPE_DIM, 2, dtype=F32) / ROPE_DIM)
    ang = pos[:, None] * inv_freq[None, :]
    basis = jnp.concatenate([jnp.cos(ang), jnp.sin(ang), jnp.ones((s, 1), F32)], axis=1)
    g = jnp.arange(LANES) % DA_QK_DIM
    j = jnp.arange(2 * half + 1)[:, None]
    cos_sel = jnp.where(g < ROPE_DIM, j == g % half, j == 2 * half).astype(F32)
    sin_sel = ((g >= half) & (g < ROPE_DIM) & (j == g)).astype(F32) - ((g < half) & (j == half + g)).astype(F32)
    sel = jnp.concatenate([cos_sel, sin_sel], axis=1)
    return jnp.dot(basis, sel, precision=lax.Precision.HIGHEST)


def kernel(x, c, w_ada, b_ada, g_norm, ffn1_w12, ffn1_w3, w_in, conv_w, conv_b, b_igate, b_fgate,
           g_qnorm, g_knorm, lambda_qk, g_da_out, g_ml_out, w_out, ffn2_w12, ffn2_w3):
    bsz, s, d = x.shape
    assert w_ada.shape[0] == 1, "one layer (depth 1) is implemented"
    l = 0
    mod = _ada(c, w_ada[l], b_ada[l]).reshape(bsz, 3, 3, d)

    x, w12_2, w3_2, wo = _ffn(
        x, mod[:, 0], g_norm[l, 0], ffn1_w12[l].astype(BF16), ffn1_w3[l].astype(BF16),
        casts=((ffn2_w12, l), (ffn2_w3, l), (w_out, l)))
    wb = w_in[l].astype(BF16)
    o0 = 0
    parts = []
    for width in (DA_WIDTH, DA_WIDTH, DA_WIDTH, 2 * ML_WIDTH, ML_WIDTH, ML_WIDTH, 2 * ML_HEADS):
        parts.append(wb[:, o0:o0 + width])
        o0 += width
    wq, wk, wv, wmqk, wmv, wmo, wif = parts
    wif = jnp.pad(wif.T, ((0, BF16_SUBLANES - 2 * ML_HEADS), (0, 0)))
    wmv = wmv.T
    groups = DA_WIDTH // DA_QK_DIM
    q_gain = g_qnorm[l] * (DA_QK_DIM ** -0.5)
    gq = jnp.tile(q_gain * LOG2E, groups).reshape(1, DA_WIDTH)
    score_bound = BF16_ROUNDING_SLACK * DA_QK_DIM * jnp.max(jnp.abs(q_gain)) * jnp.max(jnp.abs(g_knorm[l]))
    bounded = (score_bound <= SAFE_SCORE_BOUND).astype(jnp.int32).reshape(1)
    gk = jnp.tile(g_knorm[l], groups).reshape(1, DA_WIDTH)
    rope = _rope_tables(s)
    gid = jnp.arange(MXU_TILE) // DA_QK_DIM
    bd = (gid[:, None] == gid[None, :]).astype(BF16)
    bif = jnp.concatenate([b_igate[l], b_fgate[l]]).reshape(2 * ML_HEADS, 1)
    da_q, da_k, da_v, ml_q, ml_k, ml_v, ml_o, gates = _inproj(
        x, mod[:, 1], g_norm[l, 1], wq, wk, wv, wmqk, wmv, wmo, wif, gq, gk, rope, bd,
        conv_w[l], conv_b[l].reshape(1, 2 * ML_WIDTH), bif)

    y_da = _attention(bounded, da_q, da_k, da_v, lambda_qk[l], g_da_out[l])
    y_ml = _mlstm(ml_q, ml_k, ml_v, ml_o, gates, g_ml_out[l])

    return _ffn(x, mod[:, 2], g_norm[l, 2], w12_2, w3_2,
                mix=(mod[:, 1], y_da, y_ml, wo[:DA_WIDTH], wo[DA_WIDTH:]))
```

```python
import math

import jax
import jax.numpy as jnp
from jax import lax
from jax.experimental import pallas as pl
from jax.experimental.pallas import tpu as pltpu

F32 = jnp.float32
BF16 = jnp.bfloat16

DA_HEADS = 4
DA_QK_DIM = 64
DA_V_DIM = 2 * DA_QK_DIM
DA_WIDTH = DA_HEADS * DA_V_DIM
ML_HEADS = 4
ML_DIM = 128
ML_WIDTH = ML_HEADS * ML_DIM
ROPE_THETA = 500000.0
ROPE_DIM = DA_QK_DIM // 4
D_FF = 2816
CONV_K = 4
EPS = 1e-6
LAMBDA_INIT = 0.8 - 0.6 * math.exp(-0.3 * 0)
LOG2E = 1.4426950408889634

LANES = 128
F32_SUBLANES = 8
BF16_SUBLANES = 16
MXU_TILE = 256
VMEM_LIMIT = 56 * 1024 * 1024

ADA_TN = 1024
FFN_TM = 1024
FFN_SUB = 256
PROJ_TM = 1024
PROJ_SUB = 256
CONV_COLS = 256
ATT_T = 512
SAFE_SCORE_BOUND = 40.0
BF16_ROUNDING_SLACK = 1.05
ML_CHUNK = 256
ML_ROWS = 8
ML_STATE_ROWS = ML_DIM + BF16_SUBLANES
GATE_ROWS = 16
CONV_HALO = F32_SUBLANES
assert CONV_K == 4


def _const_spec(shape):
    nd = len(shape)
    return pl.BlockSpec(shape, lambda *_: (0,) * nd, pipeline_mode=pl.Buffered(1))


def _sigmoid(x):
    return 1.0 / (1.0 + jnp.exp(-x))


def _mod_norm(x, g, shift, scale):
    ms = jnp.mean(x * x, axis=-1, keepdims=True)
    return (x * lax.rsqrt(ms + EPS)) * (g * (1.0 + scale)) + shift


def _ada_kernel(c_ref, w_ref, b_ref, o_ref):
    c = c_ref[...]
    cs = (c * _sigmoid(c)).astype(BF16)
    o_ref[...] = jnp.dot(cs, w_ref[...].astype(BF16), preferred_element_type=F32) + b_ref[...]


def _ada(c, w_ada, b_ada):
    bsz, d = c.shape
    n = w_ada.shape[1]
    tn = ADA_TN
    return pl.pallas_call(
        _ada_kernel,
        grid=(n // tn,),
        in_specs=[pl.BlockSpec((bsz, d), lambda j: (0, 0)),
                  pl.BlockSpec((d, tn), lambda j: (0, j)),
                  pl.BlockSpec((1, tn), lambda j: (0, j))],
        out_specs=pl.BlockSpec((bsz, tn), lambda j: (0, j)),
        out_shape=jax.ShapeDtypeStruct((bsz, n), F32),
        compiler_params=pltpu.CompilerParams(dimension_semantics=("arbitrary",)),
        name="adaln_mod",
    )(c, w_ada, b_ada.reshape(1, n))


def _ffn_body(x, mod_ref, g_ref, w12_ref, w3_ref, o_ref, act_ref):
    mod = mod_ref[0]
    hb = _mod_norm(x, g_ref[...], mod[0:1], mod[1:2]).astype(BF16)
    for c in range(D_FF // FFN_SUB):
        cols = slice(c * FFN_SUB, (c + 1) * FFN_SUB)
        a = jnp.dot(hb, w12_ref[:, cols], preferred_element_type=F32)
        b = jnp.dot(hb, w12_ref[:, D_FF + c * FFN_SUB:D_FF + (c + 1) * FFN_SUB], preferred_element_type=F32)
        act_ref[:, cols] = (a * _sigmoid(a) * b).astype(BF16)
    y = jnp.dot(act_ref[...], w3_ref[...], preferred_element_type=F32)
    o_ref[0] = x + (0.5 * (1.0 + mod[2:3])) * y


def _ffn_kernel(x_ref, mod_ref, g_ref, w12_ref, w3_ref, *rest):
    n = (len(rest) - 2) // 2
    casts_in, o_ref, casts_out, act_ref = rest[:n], rest[n], rest[n + 1:2 * n + 1], rest[-1]
    _ffn_body(x_ref[0], mod_ref, g_ref, w12_ref, w3_ref, o_ref, act_ref)
    for src, dst in zip(casts_in, casts_out):
        dst[...] = src[...].astype(BF16)


def _mix_ffn_kernel(x_ref, mmod_ref, ya_ref, ym_ref, wa_ref, wm_ref,
                    mod_ref, g_ref, w12_ref, w3_ref, o_ref, act_ref):
    y = (jnp.dot(ya_ref[0], wa_ref[...], preferred_element_type=F32)
         + jnp.dot(ym_ref[0], wm_ref[...], preferred_element_type=F32))
    x = x_ref[0] + (1.0 + mmod_ref[0][2:3]) * y
    _ffn_body(x, mod_ref, g_ref, w12_ref, w3_ref, o_ref, act_ref)


def _ffn(x, mod3, g, w12, w3, mix=None, casts=()):
    bsz, s, d = x.shape
    tm = FFN_TM
    steps = bsz * (s // tm)
    tok_spec = lambda width: pl.BlockSpec((1, tm, width), lambda b, i: (b, i, 0))
    mod_spec = pl.BlockSpec((1, 3, d), lambda b, i: (b, 0, 0))
    ffn_specs = [mod_spec, _const_spec((1, d)),
                 _const_spec(w12.shape), _const_spec(w3.shape)]
    ffn_args = (mod3, g.reshape(1, d), w12, w3)

    def slab(w, layer=None):
        rows, cols = w.shape[-2:]
        lead = () if layer is None else (layer,)
        if rows % (steps * BF16_SUBLANES) == 0:
            return pl.BlockSpec((None,) * len(lead) + (rows // steps, cols),
                                lambda b, i: lead + (b * (s // tm) + i, 0))
        return pl.BlockSpec((None,) * len(lead) + (rows // bsz, cols), lambda b, i: lead + (b, 0))

    out_specs, out_shape = tok_spec(d), jax.ShapeDtypeStruct(x.shape, F32)
    if mix is None:
        body = _ffn_kernel
        in_specs = [tok_spec(d)] + ffn_specs + [slab(w, layer) for w, layer in casts]
        args = (x,) + ffn_args + tuple(w for w, _ in casts)
        if casts:
            cast_shapes = [jax.ShapeDtypeStruct(w.shape[-2:], BF16) for w, _ in casts]
            out_specs = [out_specs] + [slab(c) for c in cast_shapes]
            out_shape = [out_shape] + cast_shapes
    else:
        mmod3, y_da, y_ml, w_a, w_m = mix
        body = _mix_ffn_kernel
        in_specs = [tok_spec(d), mod_spec, tok_spec(DA_WIDTH), tok_spec(ML_WIDTH),
                    _const_spec(w_a.shape), _const_spec(w_m.shape)] + ffn_specs
        args = (x, mmod3, y_da, y_ml, w_a, w_m) + ffn_args
    return pl.pallas_call(
        body,
        grid=(bsz, s // tm),
        in_specs=in_specs,
        out_specs=out_specs,
        out_shape=out_shape,
        scratch_shapes=[pltpu.VMEM((tm, D_FF), BF16)],
        compiler_params=pltpu.CompilerParams(
            dimension_semantics=("arbitrary", "arbitrary"), vmem_limit_bytes=VMEM_LIMIT),
        name="ffn" if mix is None else "mix_ffn",
    )(*args)


def _group_norm_rope(u, gvec, bd, cos, sin):
    x2 = u * u
    ssq = jnp.concatenate(
        [jnp.dot(x2[:, j:j + MXU_TILE].astype(BF16), bd, preferred_element_type=F32)
         for j in range(0, u.shape[1], MXU_TILE)], axis=1)
    xn = (u * lax.rsqrt(ssq * (1.0 / DA_QK_DIM) + EPS)) * gvec
    half = ROPE_DIM // 2
    first_half = (lax.broadcasted_iota(jnp.int32, (1, LANES), 1) % DA_QK_DIM) < half
    outs = []
    for h in range(u.shape[1] // LANES):
        xh = xn[:, h * LANES:(h + 1) * LANES]
        up = pltpu.roll(xh, LANES - half, 1)
        dn = pltpu.roll(xh, half, 1)
        outs.append(xh * cos + jnp.where(first_half, up, dn) * sin)
    return jnp.concatenate(outs, axis=1)


def _chunk_scan(x, op, identity):
    pos = lax.broadcasted_iota(jnp.int32, x.shape, 1) & (ML_CHUNK - 1)
    d = 1
    while d < ML_CHUNK:
        x = op(x, jnp.where(pos >= d, pltpu.roll(x, d, 1), identity))
        d *= 2
    return x


def _inproj_kernel(x_ref, mod_ref, g_ref, wq_ref, wk_ref, wv_ref, wmqk_ref, wmv_ref, wmo_ref, wif_ref,
                   gq_ref, gk_ref, rope_ref, bd_ref, cw_ref, cb_ref, bif_ref,
                   qkv_out, mqko_out, mv_out, gate_out, halo_ref):
    tm = x_ref.shape[1]
    si = pl.program_id(1)
    mod = mod_ref[0]
    nh = ML_HEADS
    nt = (((1,), (1,)), ((), ()))
    bd = bd_ref[...]
    cw = cw_ref[...]

    tails = jnp.where(si == 0, 0.0, halo_ref[...])
    tails = [tails[:, ci * CONV_COLS:(ci + 1) * CONV_COLS] for ci in range(2 * ML_WIDTH // CONV_COLS)]

    for r0 in range(0, tm, PROJ_SUB):
        rows = slice(r0, r0 + PROJ_SUB)
        hb = _mod_norm(x_ref[0, rows, :], g_ref[...], mod[0:1], mod[1:2]).astype(BF16)

        pre = lax.dot_general(wif_ref[...], hb, nt, preferred_element_type=F32)[0:2 * nh] + bif_ref[...]
        zf = pre[nh:2 * nh]
        log_f = jnp.minimum(zf, 0.0) - jnp.log1p(jnp.exp(-jnp.abs(zf)))
        b = _chunk_scan(log_f, jnp.add, 0.0)
        c = pre[0:nh] - b
        gate_out[0, :, rows] = jnp.concatenate(
            [c, b, _chunk_scan(c, jnp.maximum, -jnp.inf), jnp.zeros_like(c)], axis=0)

        cos, sin = rope_ref[rows, 0:LANES], rope_ref[rows, LANES:2 * LANES]

        def conv_chunk(ci):
            cols = slice(ci * CONV_COLS, (ci + 1) * CONV_COLS)
            u = jnp.dot(hb, wmqk_ref[:, cols], preferred_element_type=F32)
            ext = jnp.concatenate([tails[ci], u], axis=0)
            tails[ci] = u[PROJ_SUB - CONV_HALO:, :]
            prev = pltpu.roll(ext, 1, 0)
            near = ext * cw[3:4, cols] + prev * cw[2:3, cols]
            far = ext * cw[1:2, cols] + prev * cw[0:1, cols]
            acc = cb_ref[:, cols] + near[CONV_HALO:, :] + pltpu.roll(far, 2, 0)[CONV_HALO:, :]
            act = acc * _sigmoid(acc)
            if ci < ML_WIDTH // CONV_COLS:
                act = act * (ML_DIM ** -0.5)
            mqko_out[0, rows, cols] = act.astype(BF16)

        chunks = iter(range(2 * ML_WIDTH // CONV_COLS))
        per_slot = 2 * ML_WIDTH // CONV_COLS // 4
        uq = jnp.dot(hb, wq_ref[...], preferred_element_type=F32)
        for _ in range(per_slot):
            conv_chunk(next(chunks))
        qkv_out[0, rows, 0:DA_WIDTH] = _group_norm_rope(uq, gq_ref[...], bd, cos, sin).astype(BF16)
        uk = jnp.dot(hb, wk_ref[...], preferred_element_type=F32)
        for _ in range(per_slot):
            conv_chunk(next(chunks))
        qkv_out[0, rows, DA_WIDTH:2 * DA_WIDTH] = _group_norm_rope(uk, gk_ref[...], bd, cos, sin).astype(BF16)
        uv = jnp.dot(hb, wv_ref[...], preferred_element_type=F32)
        umv_t = lax.dot_general(wmv_ref[...], hb, nt, preferred_element_type=F32)
        for _ in range(per_slot):
            conv_chunk(next(chunks))
        qkv_out[0, rows, 2 * DA_WIDTH:3 * DA_WIDTH] = uv.astype(BF16)
        mv_out[0, :, rows] = umv_t.astype(BF16)
        umo = jnp.dot(hb, wmo_ref[...], preferred_element_type=F32)
        for _ in range(per_slot):
            conv_chunk(next(chunks))
        mqko_out[0, rows, 2 * ML_WIDTH:3 * ML_WIDTH] = _sigmoid(umo).astype(BF16)

    halo_ref[...] = jnp.concatenate(tails, axis=1)


def _inproj(x, mod3, g, wq, wk, wv, wmqk, wmv, wmo, wif, gq, gk, rope, bd, cw, cb, bif):
    bsz, s, d = x.shape
    tm = PROJ_TM
    tok = lambda width, dt: jax.ShapeDtypeStruct((bsz, s, width), dt)
    tok_spec = lambda width: pl.BlockSpec((1, tm, width), lambda b, i: (b, i, 0))
    return pl.pallas_call(
        _inproj_kernel,
        grid=(bsz, s // tm),
        in_specs=[tok_spec(d),
                  pl.BlockSpec((1, 3, d), lambda b, i: (b, 0, 0)),
                  _const_spec((1, d)),
                  _const_spec(wq.shape), _const_spec(wk.shape), _const_spec(wv.shape),
                  _const_spec(wmqk.shape), _const_spec(wmv.shape), _const_spec(wmo.shape),
                  _const_spec(wif.shape),
                  _const_spec(gq.shape), _const_spec(gk.shape),
                  pl.BlockSpec((tm, 2 * LANES), lambda b, i: (i, 0)),
                  _const_spec(bd.shape), _const_spec(cw.shape), _const_spec(cb.shape),
                  _const_spec(bif.shape)],
        out_specs=[tok_spec(3 * DA_WIDTH), tok_spec(3 * ML_WIDTH),
                   pl.BlockSpec((1, ML_WIDTH, tm), lambda b, i: (b, 0, i)),
                   pl.BlockSpec((1, GATE_ROWS, tm), lambda b, i: (b, 0, i))],
        out_shape=[tok(3 * DA_WIDTH, BF16), tok(3 * ML_WIDTH, BF16),
                   jax.ShapeDtypeStruct((bsz, ML_WIDTH, s), BF16),
                   jax.ShapeDtypeStruct((bsz, GATE_ROWS, s), F32)],
        scratch_shapes=[pltpu.VMEM((CONV_HALO, 2 * ML_WIDTH), F32)],
        compiler_params=pltpu.CompilerParams(
            dimension_semantics=("arbitrary", "arbitrary"), vmem_limit_bytes=VMEM_LIMIT),
        name="in_proj",
    )(x, mod3, g.reshape(1, d), wq, wk, wv, wmqk, wmv, wmo, wif, gq, gk, rope, bd, cw, cb, bif)


def _attn_kernel(bounded_ref, q_ref, k_ref, v_ref, lam_ref, gout_ref, o_ref, qs_ref, m_ref, acc_ref):
    t = ATT_T
    u = t // 2
    lane = lax.broadcasted_iota(jnp.int32, (u, DA_V_DIM), 1)
    ones_cols = jnp.ones((t, DA_V_DIM), BF16)
    lv = lam_ref[...]
    lam = (jnp.exp(jnp.sum(lv[0:1] * lv[1:2], axis=1, keepdims=True))
           - jnp.exp(jnp.sum(lv[2:3] * lv[3:4], axis=1, keepdims=True)) + LAMBDA_INIT)

    def query_rows(qi, half, part):
        return pl.ds(pl.multiple_of(qi * (2 * t) + half * t + part * u, u), u)

    def stack_queries(qi):
        for half in range(2):
            for part in range(2):
                q = q_ref[0, query_rows(qi, half, part), :]
                zero = jnp.zeros_like(q)
                qs_ref[half, (2 * part) * u:(2 * part + 1) * u, :] = jnp.where(lane < DA_QK_DIM, q, zero)
                qs_ref[half, (2 * part + 1) * u:(2 * part + 2) * u, :] = jnp.where(lane >= DA_QK_DIM, q, zero)

    def keys(kb):
        start = pl.multiple_of(kb * t, t)
        return k_ref[0, pl.ds(start, t), :], jnp.concatenate([v_ref[0, pl.ds(start, t), :], ones_cols], axis=1)

    def piece(half, rows, ks, vs, causal_shift, first, stabilised):
        s = lax.dot_general(qs_ref[half, rows, :], ks, (((1,), (1,)), ((), ())), preferred_element_type=F32)
        if causal_shift is not None:
            row = lax.broadcasted_iota(jnp.int32, s.shape, 0) & (u - 1)
            col = lax.broadcasted_iota(jnp.int32, s.shape, 1)
            s = jnp.where(col <= row + causal_shift, s, -jnp.inf)
        if stabilised:
            m_new = jnp.max(s, axis=1, keepdims=True)
            if not first:
                m_prev = m_ref[half, rows, :]
                m_new = jnp.maximum(m_prev, m_new)
                alpha = jnp.exp2(m_prev - m_new)
            m_ref[half, rows, :] = m_new
            s = s - m_new
        pv = jnp.dot(jnp.exp2(s).astype(BF16), vs, preferred_element_type=F32)
        if first:
            acc_ref[half, rows, :] = pv
        elif stabilised:
            acc_ref[half, rows, :] = alpha * acc_ref[half, rows, :] + pv
        else:
            acc_ref[half, rows, :] += pv

    def block(half, kv, diagonal, first, stabilised):
        ks, vs = kv
        if diagonal:
            piece(half, slice(0, 2 * u), ks[0:u], vs[0:u], 0, first, stabilised)
            piece(half, slice(2 * u, 4 * u), ks, vs, u, first, stabilised)
        else:
            piece(half, slice(0, 4 * u), ks, vs, None, first, stabilised)

    def tile_pair(qi, stabilised):
        stack_queries(qi)
        kv = keys(2 * qi)
        block(0, kv, True, True, stabilised)
        block(1, kv, False, True, stabilised)
        block(1, keys(2 * qi + 1), True, False, stabilised)

        def body(pair, carry):
            for j in range(2):
                kv = keys(2 * pair + j)
                block(0, kv, False, False, stabilised)
                block(1, kv, False, False, stabilised)
            return carry

        lax.fori_loop(0, qi, body, 0)

        for half in range(2):
            for part in range(2):
                rows = slice(2 * part * u, (2 * part + 2) * u)
                o = acc_ref[half, rows, 0:DA_V_DIM] / acc_ref[half, rows, DA_V_DIM:2 * DA_V_DIM]
                od = o[0:u] - lam * o[u:2 * u]
                ms = jnp.mean(od * od, axis=-1, keepdims=True)
                o_ref[0, query_rows(qi, half, part), :] = (
                    (od * lax.rsqrt(ms + EPS)) * gout_ref[...] * (1.0 - LAMBDA_INIT)).astype(BF16)

    def run(stabilised):
        def body(qi, carry):
            tile_pair(qi, stabilised)
            return carry

        lax.fori_loop(0, q_ref.shape[1] // (2 * t), body, 0)

    @pl.when(bounded_ref[0] != 0)
    def _():
        run(False)

    @pl.when(bounded_ref[0] == 0)
    def _():
        run(True)


def _attention(bounded, qkv, lam_vecs, g_out):
    bsz, s, _ = qkv.shape
    t = ATT_T
    full = pl.BlockSpec((1, s, DA_V_DIM), lambda b, h: (b, 0, h))
    part = lambda j: pl.BlockSpec((1, s, DA_V_DIM), lambda b, h: (b, 0, j * DA_HEADS + h))
    return pl.pallas_call(
        _attn_kernel,
        grid=(bsz, DA_HEADS),
        in_specs=[pl.BlockSpec(memory_space=pltpu.SMEM),
                  part(0), part(1), part(2), _const_spec(lam_vecs.shape), _const_spec((1, DA_V_DIM))],
        out_specs=full,
        out_shape=jax.ShapeDtypeStruct((bsz, s, DA_WIDTH), BF16),
        scratch_shapes=[pltpu.VMEM((2, 2 * t, DA_V_DIM), BF16),
                        pltpu.VMEM((2, 2 * t, 1), F32),
                        pltpu.VMEM((2, 2 * t, 2 * DA_V_DIM), F32)],
        compiler_params=pltpu.CompilerParams(
            dimension_semantics=("arbitrary", "arbitrary"), vmem_limit_bytes=VMEM_LIMIT),
        name="diff_attention",
    )(bounded, qkv, qkv, qkv, lam_vecs, g_out.reshape(1, DA_V_DIM))


def _mlstm_kernel(q_ref, k_ref, vt_ref, o_ref, gr_ref, gout_ref, y_ref, ct_ref, m_ref):
    ln = ML_CHUNK
    nh = ML_HEADS
    nt = (((1,), (1,)), ((), ()))

    @pl.when(pl.program_id(1) == 0)
    def _():
        ct_ref[...] = jnp.zeros(ct_ref.shape, F32)
        m_ref[...] = jnp.zeros(m_ref.shape, F32)

    row = lax.broadcasted_iota(jnp.int32, (ln, ln), 0)
    col = lax.broadcasted_iota(jnp.int32, (ln, ln), 1)
    lower = col <= row
    ones_row = (lax.broadcasted_iota(jnp.int32, (ML_STATE_ROWS - ML_DIM, ln), 0) == 0).astype(BF16)

    for r in range(ML_ROWS):
        gr = gr_ref[r]
        c, b = gr[0:nh], gr[nh:2 * nh]
        m_prev = jnp.concatenate([m_ref[r, 0:nh, :]] * (ln // LANES), axis=1)
        m_run = jnp.maximum(gr[2 * nh:3 * nh], m_prev)
        m_last = m_run[:, ln - 1:ln]
        e_inter = jnp.exp(m_prev - m_run)
        e_floor = jnp.exp(-(b + m_run))
        w_in = jnp.exp(c - m_last)
        decay = e_inter[:, ln - 1:ln]
        m_ref[r, 0:nh, :] = jnp.broadcast_to(b[:, ln - 1:ln] + m_last, (nh, LANES))
        m_run_cols = jnp.concatenate([m_run, m_run], axis=0).T

        for h in range(nh):
            lanes = slice(h * ML_DIM, (h + 1) * ML_DIM)
            q = q_ref[r, :, lanes]
            k = k_ref[r, :, lanes]
            vt_aug = jnp.concatenate([vt_ref[r, lanes, :], ones_row], axis=0)
            s = lax.dot_general(q, k, nt, preferred_element_type=F32)
            gate = jnp.exp(jnp.where(lower, c[h:h + 1, :] - m_run_cols[:, h:h + 1], -jnp.inf))
            w = (s * gate).astype(BF16)
            state_t = ct_ref[r, h]
            full_t = (e_inter[h:h + 1, :]
                      * lax.dot_general(state_t.astype(BF16), q, nt, preferred_element_type=F32)
                      + lax.dot_general(vt_aug, w, nt, preferred_element_type=F32))
            den = full_t[ML_DIM:ML_DIM + 1, :]
            hh_t = full_t[0:ML_DIM, :] / jnp.maximum(jnp.abs(den), e_floor[h:h + 1, :])
            hn_t = hh_t * lax.rsqrt(jnp.mean(hh_t * hh_t, axis=0, keepdims=True) + EPS)
            y_ref[r, :, lanes] = (o_ref[r, :, lanes].astype(F32) * (hn_t.T * gout_ref[:, lanes])).astype(BF16)

            vw = (vt_aug.astype(F32) * w_in[h:h + 1, :]).astype(BF16)
            ct_ref[r, h] = decay[h:h + 1, :] * state_t + jnp.dot(vw, k, preferred_element_type=F32)


def _mlstm(qko, vt, gates_row, g_out):
    bsz, s, _ = qko.shape
    ln = ML_CHUNK
    rows = ML_ROWS
    tok = pl.BlockSpec((rows, ln, ML_WIDTH), lambda b, c: (b, c, 0))
    part = lambda j: pl.BlockSpec((rows, ln, ML_WIDTH), lambda b, c: (b, c, j))
    return pl.pallas_call(
        _mlstm_kernel,
        grid=(bsz // rows, s // ln),
        in_specs=[part(0), part(1), pl.BlockSpec((rows, ML_WIDTH, ln), lambda b, c: (b, 0, c)), part(2),
                  pl.BlockSpec((rows, GATE_ROWS, ln), lambda b, c: (b, 0, c)),
                  _const_spec((1, ML_WIDTH))],
        out_specs=tok,
        out_shape=jax.ShapeDtypeStruct((bsz, s, ML_WIDTH), BF16),
        scratch_shapes=[pltpu.VMEM((rows, ML_HEADS, ML_STATE_ROWS, ML_DIM), F32),
                        pltpu.VMEM((rows, F32_SUBLANES, LANES), F32)],
        compiler_params=pltpu.CompilerParams(
            dimension_semantics=("arbitrary", "arbitrary"), vmem_limit_bytes=VMEM_LIMIT),
        name="mlstm",
    )(qko, qko, vt, qko, gates_row, g_out.reshape(1, ML_WIDTH))


def _rope_tables(s):
    half = ROPE_DIM // 2
    pos = jnp.arange(s, dtype=F32)
    inv_freq = ROPE_THETA ** (-jnp.arange(0, ROPE_DIM, 2, dtype=F32) / ROPE_DIM)
    ang = pos[:, None] * inv_freq[None, :]
    basis = jnp.concatenate([jnp.cos(ang), jnp.sin(ang), jnp.ones((s, 1), F32)], axis=1)
    g = jnp.arange(LANES) % DA_QK_DIM
    j = jnp.arange(2 * half + 1)[:, None]
    cos_sel = jnp.where(g < ROPE_DIM, j == g % half, j == 2 * half).astype(F32)
    sin_sel = ((g >= half) & (g < ROPE_DIM) & (j == g)).astype(F32) - ((g < half) & (j == half + g)).astype(F32)
    sel = jnp.concatenate([cos_sel, sin_sel], axis=1)
    return jnp.dot(basis, sel, precision=lax.Precision.HIGHEST)


def kernel(x, c, w_ada, b_ada, g_norm, ffn1_w12, ffn1_w3, w_in, conv_w, conv_b, b_igate, b_fgate,
           g_qnorm, g_knorm, lambda_qk, g_da_out, g_ml_out, w_out, ffn2_w12, ffn2_w3):
    bsz, s, d = x.shape
    assert w_ada.shape[0] == 1, "one layer (depth 1) is implemented"
    l = 0
    mod = _ada(c, w_ada[l], b_ada[l]).reshape(bsz, 3, 3, d)

    x, w12_2, w3_2, wo = _ffn(
        x, mod[:, 0], g_norm[l, 0], ffn1_w12[l].astype(BF16), ffn1_w3[l].astype(BF16),
        casts=((ffn2_w12, l), (ffn2_w3, l), (w_out, l)))
    wb = w_in[l].astype(BF16)
    o0 = 0
    parts = []
    for width in (DA_WIDTH, DA_WIDTH, DA_WIDTH, 2 * ML_WIDTH, ML_WIDTH, ML_WIDTH, 2 * ML_HEADS):
        parts.append(wb[:, o0:o0 + width])
        o0 += width
    wq, wk, wv, wmqk, wmv, wmo, wif = parts
    wif = jnp.pad(wif.T, ((0, BF16_SUBLANES - 2 * ML_HEADS), (0, 0)))
    wmv = wmv.T
    groups = DA_WIDTH // DA_QK_DIM
    q_gain = g_qnorm[l] * (DA_QK_DIM ** -0.5)
    gq = jnp.tile(q_gain * LOG2E, groups).reshape(1, DA_WIDTH)
    score_bound = BF16_ROUNDING_SLACK * DA_QK_DIM * jnp.max(jnp.abs(q_gain)) * jnp.max(jnp.abs(g_knorm[l]))
    bounded = (score_bound <= SAFE_SCORE_BOUND).astype(jnp.int32).reshape(1)
    gk = jnp.tile(g_knorm[l], groups).reshape(1, DA_WIDTH)
    rope = _rope_tables(s)
    gid = jnp.arange(MXU_TILE) // DA_QK_DIM
    bd = (gid[:, None] == gid[None, :]).astype(BF16)
    bif = jnp.concatenate([b_igate[l], b_fgate[l]]).reshape(2 * ML_HEADS, 1)
    da_qkv, ml_qko, ml_vt, gates = _inproj(
        x, mod[:, 1], g_norm[l, 1], wq, wk, wv, wmqk, wmv, wmo, wif, gq, gk, rope, bd,
        conv_w[l], conv_b[l].reshape(1, 2 * ML_WIDTH), bif)

    y_da = _attention(bounded, da_qkv, lambda_qk[l], g_da_out[l])
    y_ml = _mlstm(ml_qko, ml_vt, gates, g_ml_out[l])

    return _ffn(x, mod[:, 2], g_norm[l, 2], w12_2, w3_2,
                mix=(mod[:, 1], y_da, y_ml, wo[:DA_WIDTH], wo[DA_WIDTH:]))
```

```python
import math

import jax
import jax.numpy as jnp
from jax import lax
from jax.experimental import pallas as pl
from jax.experimental.pallas import tpu as pltpu

F32 = jnp.float32
BF16 = jnp.bfloat16

DA_HEADS = 4
DA_QK_DIM = 64
DA_V_DIM = 2 * DA_QK_DIM
DA_WIDTH = DA_HEADS * DA_V_DIM
ML_HEADS = 4
ML_DIM = 128
ML_WIDTH = ML_HEADS * ML_DIM
ROPE_THETA = 500000.0
ROPE_DIM = DA_QK_DIM // 4
D_FF = 2816
CONV_K = 4
EPS = 1e-6
LAMBDA_INIT = 0.8 - 0.6 * math.exp(-0.3 * 0)
LOG2E = 1.4426950408889634

LANES = 128
F32_SUBLANES = 8
BF16_SUBLANES = 16
MXU_TILE = 256
VMEM_LIMIT = 56 * 1024 * 1024

ADA_TN = 1024
FFN_TM = 1024
FFN_SUB = 256
PROJ_TM = 1024
PROJ_SUB = 256
CONV_COLS = 256
ATT_T = 512
SAFE_SCORE_BOUND = 40.0
BF16_ROUNDING_SLACK = 1.05
ML_CHUNK = 256
ML_ROWS = 8
ML_STATE_ROWS = ML_DIM + BF16_SUBLANES
GATE_ROWS = 16
CONV_HALO = F32_SUBLANES
assert CONV_K == 4


def _const_spec(shape):
    nd = len(shape)
    return pl.BlockSpec(shape, lambda *_: (0,) * nd, pipeline_mode=pl.Buffered(1))


def _sigmoid(x):
    return 1.0 / (1.0 + jnp.exp(-x))


def _mod_norm(x, g, shift, scale):
    ms = jnp.mean(x * x, axis=-1, keepdims=True)
    return (x * lax.rsqrt(ms + EPS)) * (g * (1.0 + scale)) + shift


def _ada_kernel(c_ref, w_ref, b_ref, o_ref):
    c = c_ref[...]
    cs = (c * _sigmoid(c)).astype(BF16)
    o_ref[...] = jnp.dot(cs, w_ref[...].astype(BF16), preferred_element_type=F32) + b_ref[...]


def _ada(c, w_ada, b_ada):
    bsz, d = c.shape
    n = w_ada.shape[1]
    tn = ADA_TN
    return pl.pallas_call(
        _ada_kernel,
        grid=(n // tn,),
        in_specs=[pl.BlockSpec((bsz, d), lambda j: (0, 0)),
                  pl.BlockSpec((d, tn), lambda j: (0, j)),
                  pl.BlockSpec((1, tn), lambda j: (0, j))],
        out_specs=pl.BlockSpec((bsz, tn), lambda j: (0, j)),
        out_shape=jax.ShapeDtypeStruct((bsz, n), F32),
        compiler_params=pltpu.CompilerParams(dimension_semantics=("arbitrary",)),
        name="adaln_mod",
    )(c, w_ada, b_ada.reshape(1, n))


def _ffn_body(x, mod_ref, g_ref, w12_ref, w3_ref, o_ref, act_ref):
    mod = mod_ref[0]
    hb = _mod_norm(x, g_ref[...], mod[0:1], mod[1:2]).astype(BF16)
    for c in range(D_FF // FFN_SUB):
        cols = slice(c * FFN_SUB, (c + 1) * FFN_SUB)
        a = jnp.dot(hb, w12_ref[:, cols], preferred_element_type=F32)
        b = jnp.dot(hb, w12_ref[:, D_FF + c * FFN_SUB:D_FF + (c + 1) * FFN_SUB], preferred_element_type=F32)
        act_ref[:, cols] = (a * _sigmoid(a) * b).astype(BF16)
    y = jnp.dot(act_ref[...], w3_ref[...], preferred_element_type=F32)
    o_ref[0] = x + (0.5 * (1.0 + mod[2:3])) * y


def _ffn_kernel(x_ref, mod_ref, g_ref, w12_ref, w3_ref, *rest):
    n = (len(rest) - 2) // 2
    casts_in, o_ref, casts_out, act_ref = rest[:n], rest[n], rest[n + 1:2 * n + 1], rest[-1]
    _ffn_body(x_ref[0], mod_ref, g_ref, w12_ref, w3_ref, o_ref, act_ref)
    for src, dst in zip(casts_in, casts_out):
        dst[...] = src[...].astype(BF16)


def _mix_ffn_kernel(x_ref, mmod_ref, ya_ref, ym_ref, wa_ref, wm_ref,
                    mod_ref, g_ref, w12_ref, w3_ref, o_ref, act_ref):
    y = (jnp.dot(ya_ref[0], wa_ref[...], preferred_element_type=F32)
         + jnp.dot(ym_ref[0], wm_ref[...], preferred_element_type=F32))
    x = x_ref[0] + (1.0 + mmod_ref[0][2:3]) * y
    _ffn_body(x, mod_ref, g_ref, w12_ref, w3_ref, o_ref, act_ref)


def _ffn(x, mod3, g, w12, w3, mix=None, casts=()):
    bsz, s, d = x.shape
    tm = FFN_TM
    steps = bsz * (s // tm)
    tok_spec = lambda width: pl.BlockSpec((1, tm, width), lambda b, i: (b, i, 0))
    mod_spec = pl.BlockSpec((1, 3, d), lambda b, i: (b, 0, 0))
    ffn_specs = [mod_spec, _const_spec((1, d)),
                 _const_spec(w12.shape), _const_spec(w3.shape)]
    ffn_args = (mod3, g.reshape(1, d), w12, w3)

    def slab(w, layer=None):
        rows, cols = w.shape[-2:]
        lead = () if layer is None else (layer,)
        if rows % (steps * BF16_SUBLANES) == 0:
            return pl.BlockSpec((None,) * len(lead) + (rows // steps, cols),
                                lambda b, i: lead + (b * (s // tm) + i, 0))
        return pl.BlockSpec((None,) * len(lead) + (rows // bsz, cols), lambda b, i: lead + (b, 0))

    out_specs, out_shape = tok_spec(d), jax.ShapeDtypeStruct(x.shape, F32)
    if mix is None:
        body = _ffn_kernel
        in_specs = [tok_spec(d)] + ffn_specs + [slab(w, layer) for w, layer in casts]
        args = (x,) + ffn_args + tuple(w for w, _ in casts)
        if casts:
            cast_shapes = [jax.ShapeDtypeStruct(w.shape[-2:], BF16) for w, _ in casts]
            out_specs = [out_specs] + [slab(c) for c in cast_shapes]
            out_shape = [out_shape] + cast_shapes
    else:
        mmod3, y_da, y_ml, w_a, w_m = mix
        body = _mix_ffn_kernel
        in_specs = [tok_spec(d), mod_spec, tok_spec(DA_WIDTH), tok_spec(ML_WIDTH),
                    _const_spec(w_a.shape), _const_spec(w_m.shape)] + ffn_specs
        args = (x, mmod3, y_da, y_ml, w_a, w_m) + ffn_args
    return pl.pallas_call(
        body,
        grid=(bsz, s // tm),
        in_specs=in_specs,
        out_specs=out_specs,
        out_shape=out_shape,
        scratch_shapes=[pltpu.VMEM((tm, D_FF), BF16)],
        compiler_params=pltpu.CompilerParams(
            dimension_semantics=("arbitrary", "arbitrary"), vmem_limit_bytes=VMEM_LIMIT),
        name="ffn" if mix is None else "mix_ffn",
    )(*args)


def _group_norm_rope(u, gvec, bd, cos, sin):
    x2 = u * u
    ssq = jnp.concatenate(
        [jnp.dot(x2[:, j:j + MXU_TILE].astype(BF16), bd, preferred_element_type=F32)
         for j in range(0, u.shape[1], MXU_TILE)], axis=1)
    xn = (u * lax.rsqrt(ssq * (1.0 / DA_QK_DIM) + EPS)) * gvec
    half = ROPE_DIM // 2
    first_half = (lax.broadcasted_iota(jnp.int32, (1, LANES), 1) % DA_QK_DIM) < half
    outs = []
    for h in range(u.shape[1] // LANES):
        xh = xn[:, h * LANES:(h + 1) * LANES]
        up = pltpu.roll(xh, LANES - half, 1)
        dn = pltpu.roll(xh, half, 1)
        outs.append(xh * cos + jnp.where(first_half, up, dn) * sin)
    return jnp.concatenate(outs, axis=1)


def _chunk_scan(x, op, identity):
    pos = lax.broadcasted_iota(jnp.int32, x.shape, 1) & (ML_CHUNK - 1)
    d = 1
    while d < ML_CHUNK:
        x = op(x, jnp.where(pos >= d, pltpu.roll(x, d, 1), identity))
        d *= 2
    return x


def _inproj_kernel(x_ref, mod_ref, g_ref, wq_ref, wk_ref, wv_ref, wmqk_ref, wmv_ref, wmo_ref, wif_ref,
                   gq_ref, gk_ref, rope_ref, bd_ref, cw_ref, cb_ref, bif_ref,
                   qkv_out, mqko_out, mv_out, gate_out, halo_ref):
    tm = x_ref.shape[1]
    si = pl.program_id(1)
    mod = mod_ref[0]
    nh = ML_HEADS
    nt = (((1,), (1,)), ((), ()))
    bd = bd_ref[...]
    cw = cw_ref[...]

    tails = jnp.where(si == 0, 0.0, halo_ref[...])
    tails = [tails[:, ci * CONV_COLS:(ci + 1) * CONV_COLS] for ci in range(2 * ML_WIDTH // CONV_COLS)]

    for r0 in range(0, tm, PROJ_SUB):
        rows = slice(r0, r0 + PROJ_SUB)
        hb = _mod_norm(x_ref[0, rows, :], g_ref[...], mod[0:1], mod[1:2]).astype(BF16)

        pre = lax.dot_general(wif_ref[...], hb, nt, preferred_element_type=F32)[0:2 * nh] + bif_ref[...]
        zf = pre[nh:2 * nh]
        log_f = jnp.minimum(zf, 0.0) - jnp.log1p(jnp.exp(-jnp.abs(zf)))
        b = _chunk_scan(log_f, jnp.add, 0.0)
        c = pre[0:nh] - b
        gate_out[0, :, rows] = jnp.concatenate(
            [c, b, _chunk_scan(c, jnp.maximum, -jnp.inf), jnp.zeros_like(c)], axis=0)

        cos, sin = rope_ref[rows, 0:LANES], rope_ref[rows, LANES:2 * LANES]

        def conv_chunk(ci):
            cols = slice(ci * CONV_COLS, (ci + 1) * CONV_COLS)
            u = jnp.dot(hb, wmqk_ref[:, cols], preferred_element_type=F32)
            ext = jnp.concatenate([tails[ci], u], axis=0)
            tails[ci] = u[PROJ_SUB - CONV_HALO:, :]
            prev = pltpu.roll(ext, 1, 0)
            near = ext * cw[3:4, cols] + prev * cw[2:3, cols]
            far = ext * cw[1:2, cols] + prev * cw[0:1, cols]
            acc = cb_ref[:, cols] + near[CONV_HALO:, :] + pltpu.roll(far, 2, 0)[CONV_HALO:, :]
            act = acc * _sigmoid(acc)
            if ci < ML_WIDTH // CONV_COLS:
                act = act * (ML_DIM ** -0.5)
            mqko_out[0, rows, cols] = act.astype(BF16)

        chunks = iter(range(2 * ML_WIDTH // CONV_COLS))
        per_slot = 2 * ML_WIDTH // CONV_COLS // 4
        uq = jnp.dot(hb, wq_ref[...], preferred_element_type=F32)
        for _ in range(per_slot):
            conv_chunk(next(chunks))
        qkv_out[0, rows, 0:DA_WIDTH] = _group_norm_rope(uq, gq_ref[...], bd, cos, sin).astype(BF16)
        uk = jnp.dot(hb, wk_ref[...], preferred_element_type=F32)
        for _ in range(per_slot):
            conv_chunk(next(chunks))
        qkv_out[0, rows, DA_WIDTH:2 * DA_WIDTH] = _group_norm_rope(uk, gk_ref[...], bd, cos, sin).astype(BF16)
        uv = jnp.dot(hb, wv_ref[...], preferred_element_type=F32)
        umv_t = lax.dot_general(wmv_ref[...], hb, nt, preferred_element_type=F32)
        for _ in range(per_slot):
            conv_chunk(next(chunks))
        qkv_out[0, rows, 2 * DA_WIDTH:3 * DA_WIDTH] = uv.astype(BF16)
        mv_out[0, :, rows] = umv_t.astype(BF16)
        umo = jnp.dot(hb, wmo_ref[...], preferred_element_type=F32)
        for _ in range(per_slot):
            conv_chunk(next(chunks))
        mqko_out[0, rows, 2 * ML_WIDTH:3 * ML_WIDTH] = _sigmoid(umo).astype(BF16)

    halo_ref[...] = jnp.concatenate(tails, axis=1)


def _inproj(x, mod3, g, wq, wk, wv, wmqk, wmv, wmo, wif, gq, gk, rope, bd, cw, cb, bif):
    bsz, s, d = x.shape
    tm = PROJ_TM
    tok = lambda width, dt: jax.ShapeDtypeStruct((bsz, s, width), dt)
    tok_spec = lambda width: pl.BlockSpec((1, tm, width), lambda b, i: (b, i, 0))
    return pl.pallas_call(
        _inproj_kernel,
        grid=(bsz, s // tm),
        in_specs=[tok_spec(d),
                  pl.BlockSpec((1, 3, d), lambda b, i: (b, 0, 0)),
                  _const_spec((1, d)),
                  _const_spec(wq.shape), _const_spec(wk.shape), _const_spec(wv.shape),
                  _const_spec(wmqk.shape), _const_spec(wmv.shape), _const_spec(wmo.shape),
                  _const_spec(wif.shape),
                  _const_spec(gq.shape), _const_spec(gk.shape),
                  pl.BlockSpec((tm, 2 * LANES), lambda b, i: (i, 0)),
                  _const_spec(bd.shape), _const_spec(cw.shape), _const_spec(cb.shape),
                  _const_spec(bif.shape)],
        out_specs=[tok_spec(3 * DA_WIDTH), tok_spec(3 * ML_WIDTH),
                   pl.BlockSpec((1, ML_WIDTH, tm), lambda b, i: (b, 0, i)),
                   pl.BlockSpec((1, GATE_ROWS, tm), lambda b, i: (b, 0, i))],
        out_shape=[tok(3 * DA_WIDTH, BF16), tok(3 * ML_WIDTH, BF16),
                   jax.ShapeDtypeStruct((bsz, ML_WIDTH, s), BF16),
                   jax.ShapeDtypeStruct((bsz, GATE_ROWS, s), F32)],
        scratch_shapes=[pltpu.VMEM((CONV_HALO, 2 * ML_WIDTH), F32)],
        compiler_params=pltpu.CompilerParams(
            dimension_semantics=("arbitrary", "arbitrary"), vmem_limit_bytes=VMEM_LIMIT),
        name="in_proj",
    )(x, mod3, g.reshape(1, d), wq, wk, wv, wmqk, wmv, wmo, wif, gq, gk, rope, bd, cw, cb, bif)


def _attn_kernel(bounded_ref, q_ref, k_ref, v_ref, lam_ref, gout_ref, o_ref, qs_ref, m_ref, acc_ref):
    t = ATT_T
    u = t // 2
    lane = lax.broadcasted_iota(jnp.int32, (u, DA_V_DIM), 1)
    ones_cols = jnp.ones((t, DA_V_DIM), BF16)
    lv = lam_ref[...]
    lam = (jnp.exp(jnp.sum(lv[0:1] * lv[1:2], axis=1, keepdims=True))
           - jnp.exp(jnp.sum(lv[2:3] * lv[3:4], axis=1, keepdims=True)) + LAMBDA_INIT)

    def query_rows(qi, half, part):
        return pl.ds(pl.multiple_of(qi * (2 * t) + half * t + part * u, u), u)

    def stack_queries(qi):
        for half in range(2):
            for part in range(2):
                q = q_ref[0, query_rows(qi, half, part), :]
                zero = jnp.zeros_like(q)
                qs_ref[half, (2 * part) * u:(2 * part + 1) * u, :] = jnp.where(lane < DA_QK_DIM, q, zero)
                qs_ref[half, (2 * part + 1) * u:(2 * part + 2) * u, :] = jnp.where(lane >= DA_QK_DIM, q, zero)

    def keys(kb):
        start = pl.multiple_of(kb * t, t)
        return k_ref[0, pl.ds(start, t), :], jnp.concatenate([v_ref[0, pl.ds(start, t), :], ones_cols], axis=1)

    def piece(half, rows, ks, vs, causal_shift, first, stabilised):
        s = lax.dot_general(qs_ref[half, rows, :], ks, (((1,), (1,)), ((), ())), preferred_element_type=F32)
        if causal_shift is not None:
            row = lax.broadcasted_iota(jnp.int32, s.shape, 0) & (u - 1)
            col = lax.broadcasted_iota(jnp.int32, s.shape, 1)
            s = jnp.where(col <= row + causal_shift, s, -jnp.inf)
        if stabilised:
            m_new = jnp.max(s, axis=1, keepdims=True)
            if not first:
                m_prev = m_ref[half, rows, :]
                m_new = jnp.maximum(m_prev, m_new)
                alpha = jnp.exp2(m_prev - m_new)
            m_ref[half, rows, :] = m_new
            s = s - m_new
        pv = jnp.dot(jnp.exp2(s).astype(BF16), vs, preferred_element_type=F32)
        if first:
            acc_ref[half, rows, :] = pv
        elif stabilised:
            acc_ref[half, rows, :] = alpha * acc_ref[half, rows, :] + pv
        else:
            acc_ref[half, rows, :] += pv

    def block(half, kv, diagonal, first, stabilised):
        ks, vs = kv
        if diagonal:
            piece(half, slice(0, 2 * u), ks[0:u], vs[0:u], 0, first, stabilised)
            piece(half, slice(2 * u, 4 * u), ks, vs, u, first, stabilised)
        else:
            piece(half, slice(0, 4 * u), ks, vs, None, first, stabilised)

    def tile_pair(qi, stabilised):
        stack_queries(qi)
        kv = keys(2 * qi)
        block(0, kv, True, True, stabilised)
        block(1, kv, False, True, stabilised)
        block(1, keys(2 * qi + 1), True, False, stabilised)

        def blocks(first, count):
            for j in range(count):
                kv = keys(first + j)
                block(0, kv, False, False, stabilised)
                block(1, kv, False, False, stabilised)

        def body(quad, carry):
            blocks(4 * quad, 4)
            return carry

        lax.fori_loop(0, qi // 2, body, 0)

        @pl.when(qi % 2 == 1)
        def _():
            blocks(2 * qi - 2, 2)

        for half in range(2):
            for part in range(2):
                rows = slice(2 * part * u, (2 * part + 2) * u)
                o = acc_ref[half, rows, 0:DA_V_DIM] / acc_ref[half, rows, DA_V_DIM:2 * DA_V_DIM]
                od = o[0:u] - lam * o[u:2 * u]
                ms = jnp.mean(od * od, axis=-1, keepdims=True)
                o_ref[0, query_rows(qi, half, part), :] = (
                    (od * lax.rsqrt(ms + EPS)) * gout_ref[...] * (1.0 - LAMBDA_INIT)).astype(BF16)

    def run(stabilised):
        def body(qi, carry):
            tile_pair(qi, stabilised)
            return carry

        lax.fori_loop(0, q_ref.shape[1] // (2 * t), body, 0)

    @pl.when(bounded_ref[0] != 0)
    def _():
        run(False)

    @pl.when(bounded_ref[0] == 0)
    def _():
        run(True)


def _attention(bounded, qkv, lam_vecs, g_out):
    bsz, s, _ = qkv.shape
    t = ATT_T
    full = pl.BlockSpec((1, s, DA_V_DIM), lambda b, h: (b, 0, h))
    part = lambda j: pl.BlockSpec((1, s, DA_V_DIM), lambda b, h: (b, 0, j * DA_HEADS + h))
    return pl.pallas_call(
        _attn_kernel,
        grid=(bsz, DA_HEADS),
        in_specs=[pl.BlockSpec(memory_space=pltpu.SMEM),
                  part(0), part(1), part(2), _const_spec(lam_vecs.shape), _const_spec((1, DA_V_DIM))],
        out_specs=full,
        out_shape=jax.ShapeDtypeStruct((bsz, s, DA_WIDTH), BF16),
        scratch_shapes=[pltpu.VMEM((2, 2 * t, DA_V_DIM), BF16),
                        pltpu.VMEM((2, 2 * t, 1), F32),
                        pltpu.VMEM((2, 2 * t, 2 * DA_V_DIM), F32)],
        compiler_params=pltpu.CompilerParams(
            dimension_semantics=("arbitrary", "arbitrary"), vmem_limit_bytes=VMEM_LIMIT),
        name="diff_attention",
    )(bounded, qkv, qkv, qkv, lam_vecs, g_out.reshape(1, DA_V_DIM))


def _mlstm_kernel(q_ref, k_ref, vt_ref, o_ref, gr_ref, gout_ref, y_ref, ct_ref, m_ref):
    ln = ML_CHUNK
    nh = ML_HEADS
    nt = (((1,), (1,)), ((), ()))

    @pl.when(pl.program_id(1) == 0)
    def _():
        ct_ref[...] = jnp.zeros(ct_ref.shape, F32)
        m_ref[...] = jnp.zeros(m_ref.shape, F32)

    row = lax.broadcasted_iota(jnp.int32, (ln, ln), 0)
    col = lax.broadcasted_iota(jnp.int32, (ln, ln), 1)
    lower = col <= row
    ones_row = (lax.broadcasted_iota(jnp.int32, (ML_STATE_ROWS - ML_DIM, ln), 0) == 0).astype(BF16)

    for r in range(ML_ROWS):
        gr = gr_ref[r]
        c, b = gr[0:nh], gr[nh:2 * nh]
        m_prev = jnp.concatenate([m_ref[r, 0:nh, :]] * (ln // LANES), axis=1)
        m_run = jnp.maximum(gr[2 * nh:3 * nh], m_prev)
        m_last = m_run[:, ln - 1:ln]
        e_inter = jnp.exp(m_prev - m_run)
        e_floor = jnp.exp(-(b + m_run))
        w_in = jnp.exp(c - m_last)
        decay = e_inter[:, ln - 1:ln]
        m_ref[r, 0:nh, :] = jnp.broadcast_to(b[:, ln - 1:ln] + m_last, (nh, LANES))
        m_run_cols = jnp.concatenate([m_run, m_run], axis=0).T

        for h in range(nh):
            lanes = slice(h * ML_DIM, (h + 1) * ML_DIM)
            q = q_ref[r, :, lanes]
            k = k_ref[r, :, lanes]
            vt_aug = jnp.concatenate([vt_ref[r, lanes, :], ones_row], axis=0)
            s = lax.dot_general(q, k, nt, preferred_element_type=F32)
            gate = jnp.exp(jnp.where(lower, c[h:h + 1, :] - m_run_cols[:, h:h + 1], -jnp.inf))
            w = (s * gate).astype(BF16)
            state_t = ct_ref[r, h]
            full_t = (e_inter[h:h + 1, :]
                      * lax.dot_general(state_t.astype(BF16), q, nt, preferred_element_type=F32)
                      + lax.dot_general(vt_aug, w, nt, preferred_element_type=F32))
            den = full_t[ML_DIM:ML_DIM + 1, :]
            hh_t = full_t[0:ML_DIM, :] / jnp.maximum(jnp.abs(den), e_floor[h:h + 1, :])
            hn_t = hh_t * lax.rsqrt(jnp.mean(hh_t * hh_t, axis=0, keepdims=True) + EPS)
            y_ref[r, :, lanes] = (o_ref[r, :, lanes].astype(F32) * (hn_t.T * gout_ref[:, lanes])).astype(BF16)

            vw = (vt_aug.astype(F32) * w_in[h:h + 1, :]).astype(BF16)
            ct_ref[r, h] = decay[h:h + 1, :] * state_t + jnp.dot(vw, k, preferred_element_type=F32)


def _mlstm(qko, vt, gates_row, g_out):
    bsz, s, _ = qko.shape
    ln = ML_CHUNK
    rows = ML_ROWS
    tok = pl.BlockSpec((rows, ln, ML_WIDTH), lambda b, c: (b, c, 0))
    part = lambda j: pl.BlockSpec((rows, ln, ML_WIDTH), lambda b, c: (b, c, j))
    return pl.pallas_call(
        _mlstm_kernel,
        grid=(bsz // rows, s // ln),
        in_specs=[part(0), part(1), pl.BlockSpec((rows, ML_WIDTH, ln), lambda b, c: (b, 0, c)), part(2),
                  pl.BlockSpec((rows, GATE_ROWS, ln), lambda b, c: (b, 0, c)),
                  _const_spec((1, ML_WIDTH))],
        out_specs=tok,
        out_shape=jax.ShapeDtypeStruct((bsz, s, ML_WIDTH), BF16),
        scratch_shapes=[pltpu.VMEM((rows, ML_HEADS, ML_STATE_ROWS, ML_DIM), F32),
                        pltpu.VMEM((rows, F32_SUBLANES, LANES), F32)],
        compiler_params=pltpu.CompilerParams(
            dimension_semantics=("arbitrary", "arbitrary"), vmem_limit_bytes=VMEM_LIMIT),
        name="mlstm",
    )(qko, qko, vt, qko, gates_row, g_out.reshape(1, ML_WIDTH))


def _rope_tables(s):
    half = ROPE_DIM // 2
    pos = jnp.arange(s, dtype=F32)
    inv_freq = ROPE_THETA ** (-jnp.arange(0, ROPE_DIM, 2, dtype=F32) / ROPE_DIM)
    ang = pos[:, None] * inv_freq[None, :]
    basis = jnp.concatenate([jnp.cos(ang), jnp.sin(ang), jnp.ones((s, 1), F32)], axis=1)
    g = jnp.arange(LANES) % DA_QK_DIM
    j = jnp.arange(2 * half + 1)[:, None]
    cos_sel = jnp.where(g < ROPE_DIM, j == g % half, j == 2 * half).astype(F32)
    sin_sel = ((g >= half) & (g < ROPE_DIM) & (j == g)).astype(F32) - ((g < half) & (j == half + g)).astype(F32)
    sel = jnp.concatenate([cos_sel, sin_sel], axis=1)
    return jnp.dot(basis, sel, precision=lax.Precision.HIGHEST)


def kernel(x, c, w_ada, b_ada, g_norm, ffn1_w12, ffn1_w3, w_in, conv_w, conv_b, b_igate, b_fgate,
           g_qnorm, g_knorm, lambda_qk, g_da_out, g_ml_out, w_out, ffn2_w12, ffn2_w3):
    bsz, s, d = x.shape
    assert w_ada.shape[0] == 1, "one layer (depth 1) is implemented"
    l = 0
    mod = _ada(c, w_ada[l], b_ada[l]).reshape(bsz, 3, 3, d)

    x, w12_2, w3_2, wo = _ffn(
        x, mod[:, 0], g_norm[l, 0], ffn1_w12[l].astype(BF16), ffn1_w3[l].astype(BF16),
        casts=((ffn2_w12, l), (ffn2_w3, l), (w_out, l)))
    wb = w_in[l].astype(BF16)
    o0 = 0
    parts = []
    for width in (DA_WIDTH, DA_WIDTH, DA_WIDTH, 2 * ML_WIDTH, ML_WIDTH, ML_WIDTH, 2 * ML_HEADS):
        parts.append(wb[:, o0:o0 + width])
        o0 += width
    wq, wk, wv, wmqk, wmv, wmo, wif = parts
    wif = jnp.pad(wif.T, ((0, BF16_SUBLANES - 2 * ML_HEADS), (0, 0)))
    wmv = wmv.T
    groups = DA_WIDTH // DA_QK_DIM
    q_gain = g_qnorm[l] * (DA_QK_DIM ** -0.5)
    gq = jnp.tile(q_gain * LOG2E, groups).reshape(1, DA_WIDTH)
    score_bound = BF16_ROUNDING_SLACK * DA_QK_DIM * jnp.max(jnp.abs(q_gain)) * jnp.max(jnp.abs(g_knorm[l]))
    bounded = (score_bound <= SAFE_SCORE_BOUND).astype(jnp.int32).reshape(1)
    gk = jnp.tile(g_knorm[l], groups).reshape(1, DA_WIDTH)
    rope = _rope_tables(s)
    gid = jnp.arange(MXU_TILE) // DA_QK_DIM
    bd = (gid[:, None] == gid[None, :]).astype(BF16)
    bif = jnp.concatenate([b_igate[l], b_fgate[l]]).reshape(2 * ML_HEADS, 1)
    da_qkv, ml_qko, ml_vt, gates = _inproj(
        x, mod[:, 1], g_norm[l, 1], wq, wk, wv, wmqk, wmv, wmo, wif, gq, gk, rope, bd,
        conv_w[l], conv_b[l].reshape(1, 2 * ML_WIDTH), bif)

    y_da = _attention(bounded, da_qkv, lambda_qk[l], g_da_out[l])
    y_ml = _mlstm(ml_qko, ml_vt, gates, g_ml_out[l])

    return _ffn(x, mod[:, 2], g_norm[l, 2], w12_2, w3_2,
                mix=(mod[:, 1], y_da, y_ml, wo[:DA_WIDTH], wo[DA_WIDTH:]))
```

```python
import math

import jax
import jax.numpy as jnp
from jax import lax
from jax.experimental import pallas as pl
from jax.experimental.pallas import tpu as pltpu

F32 = jnp.float32
BF16 = jnp.bfloat16

DA_HEADS = 4
DA_QK_DIM = 64
DA_V_DIM = 2 * DA_QK_DIM
DA_WIDTH = DA_HEADS * DA_V_DIM
ML_HEADS = 4
ML_DIM = 128
ML_WIDTH = ML_HEADS * ML_DIM
ROPE_THETA = 500000.0
ROPE_DIM = DA_QK_DIM // 4
D_FF = 2816
CONV_K = 4
EPS = 1e-6
LAMBDA_INIT = 0.8 - 0.6 * math.exp(-0.3 * 0)
LOG2E = 1.4426950408889634

LANES = 128
F32_SUBLANES = 8
BF16_SUBLANES = 16
MXU_TILE = 256
VMEM_LIMIT = 56 * 1024 * 1024

ADA_TN = 1024
FFN_TM = 1024
FFN_SUB = 256
PROJ_TM = 1024
PROJ_SUB = 256
CONV_COLS = 256
ATT_T = 512
SAFE_SCORE_BOUND = 40.0
BF16_ROUNDING_SLACK = 1.05
ML_CHUNK = 256
ML_ROWS = 8
ML_STATE_ROWS = ML_DIM + BF16_SUBLANES
GATE_ROWS = 16
CONV_HALO = F32_SUBLANES
assert CONV_K == 4


def _const_spec(shape):
    nd = len(shape)
    return pl.BlockSpec(shape, lambda *_: (0,) * nd, pipeline_mode=pl.Buffered(1))


def _sigmoid(x):
    return 1.0 / (1.0 + jnp.exp(-x))


def _mod_norm(x, g, shift, scale):
    ms = jnp.mean(x * x, axis=-1, keepdims=True)
    return (x * lax.rsqrt(ms + EPS)) * (g * (1.0 + scale)) + shift


def _ada_kernel(c_ref, w_ref, b_ref, o_ref):
    c = c_ref[...]
    cs = (c * _sigmoid(c)).astype(BF16)
    o_ref[...] = jnp.dot(cs, w_ref[...].astype(BF16), preferred_element_type=F32) + b_ref[...]


def _ada(c, w_ada, b_ada):
    bsz, d = c.shape
    n = w_ada.shape[1]
    tn = ADA_TN
    return pl.pallas_call(
        _ada_kernel,
        grid=(n // tn,),
        in_specs=[pl.BlockSpec((bsz, d), lambda j: (0, 0)),
                  pl.BlockSpec((d, tn), lambda j: (0, j)),
                  pl.BlockSpec((1, tn), lambda j: (0, j))],
        out_specs=pl.BlockSpec((bsz, tn), lambda j: (0, j)),
        out_shape=jax.ShapeDtypeStruct((bsz, n), F32),
        compiler_params=pltpu.CompilerParams(dimension_semantics=("arbitrary",)),
        name="adaln_mod",
    )(c, w_ada, b_ada.reshape(1, n))


def _ffn_body(x, mod_ref, g_ref, w12_ref, w3_ref, o_ref, act_ref):
    mod = mod_ref[0]
    hb = _mod_norm(x, g_ref[...], mod[0:1], mod[1:2]).astype(BF16)
    for c in range(D_FF // FFN_SUB):
        cols = slice(c * FFN_SUB, (c + 1) * FFN_SUB)
        a = jnp.dot(hb, w12_ref[:, cols], preferred_element_type=F32)
        b = jnp.dot(hb, w12_ref[:, D_FF + c * FFN_SUB:D_FF + (c + 1) * FFN_SUB], preferred_element_type=F32)
        act_ref[:, cols] = (a * _sigmoid(a) * b).astype(BF16)
    y = jnp.dot(act_ref[...], w3_ref[...], preferred_element_type=F32)
    o_ref[0] = x + (0.5 * (1.0 + mod[2:3])) * y


def _ffn_kernel(x_ref, mod_ref, g_ref, w12_ref, w3_ref, *rest):
    n = (len(rest) - 2) // 2
    casts_in, o_ref, casts_out, act_ref = rest[:n], rest[n], rest[n + 1:2 * n + 1], rest[-1]
    _ffn_body(x_ref[0], mod_ref, g_ref, w12_ref, w3_ref, o_ref, act_ref)
    for src, dst in zip(casts_in, casts_out):
        dst[...] = src[...].astype(BF16)


def _mix_ffn_kernel(x_ref, mmod_ref, ya_ref, ym_ref, wa_ref, wm_ref,
                    mod_ref, g_ref, w12_ref, w3_ref, o_ref, act_ref):
    y = (jnp.dot(ya_ref[0], wa_ref[...], preferred_element_type=F32)
         + jnp.dot(ym_ref[0], wm_ref[...], preferred_element_type=F32))
    x = x_ref[0] + (1.0 + mmod_ref[0][2:3]) * y
    _ffn_body(x, mod_ref, g_ref, w12_ref, w3_ref, o_ref, act_ref)


def _ffn(x, mod3, g, w12, w3, mix=None, casts=()):
    bsz, s, d = x.shape
    tm = FFN_TM
    steps = bsz * (s // tm)
    tok_spec = lambda width: pl.BlockSpec((1, tm, width), lambda b, i: (b, i, 0))
    mod_spec = pl.BlockSpec((1, 3, d), lambda b, i: (b, 0, 0))
    ffn_specs = [mod_spec, _const_spec((1, d)),
                 _const_spec(w12.shape), _const_spec(w3.shape)]
    ffn_args = (mod3, g.reshape(1, d), w12, w3)

    def slab(w, layer=None):
        rows, cols = w.shape[-2:]
        lead = () if layer is None else (layer,)
        if rows % (steps * BF16_SUBLANES) == 0:
            return pl.BlockSpec((None,) * len(lead) + (rows // steps, cols),
                                lambda b, i: lead + (b * (s // tm) + i, 0))
        return pl.BlockSpec((None,) * len(lead) + (rows // bsz, cols), lambda b, i: lead + (b, 0))

    out_specs, out_shape = tok_spec(d), jax.ShapeDtypeStruct(x.shape, F32)
    if mix is None:
        body = _ffn_kernel
        in_specs = [tok_spec(d)] + ffn_specs + [slab(w, layer) for w, layer in casts]
        args = (x,) + ffn_args + tuple(w for w, _ in casts)
        if casts:
            cast_shapes = [jax.ShapeDtypeStruct(w.shape[-2:], BF16) for w, _ in casts]
            out_specs = [out_specs] + [slab(c) for c in cast_shapes]
            out_shape = [out_shape] + cast_shapes
    else:
        mmod3, y_da, y_ml, w_a, w_m = mix
        body = _mix_ffn_kernel
        in_specs = [tok_spec(d), mod_spec, tok_spec(DA_WIDTH), tok_spec(ML_WIDTH),
                    _const_spec(w_a.shape), _const_spec(w_m.shape)] + ffn_specs
        args = (x, mmod3, y_da, y_ml, w_a, w_m) + ffn_args
    return pl.pallas_call(
        body,
        grid=(bsz, s // tm),
        in_specs=in_specs,
        out_specs=out_specs,
        out_shape=out_shape,
        scratch_shapes=[pltpu.VMEM((tm, D_FF), BF16)],
        compiler_params=pltpu.CompilerParams(
            dimension_semantics=("arbitrary", "arbitrary"), vmem_limit_bytes=VMEM_LIMIT),
        name="ffn" if mix is None else "mix_ffn",
    )(*args)


def _group_norm_rope(u, gvec, bd, cos, sin):
    x2 = u * u
    ssq = jnp.concatenate(
        [jnp.dot(x2[:, j:j + MXU_TILE].astype(BF16), bd, preferred_element_type=F32)
         for j in range(0, u.shape[1], MXU_TILE)], axis=1)
    xn = (u * lax.rsqrt(ssq * (1.0 / DA_QK_DIM) + EPS)) * gvec
    half = ROPE_DIM // 2
    first_half = (lax.broadcasted_iota(jnp.int32, (1, LANES), 1) % DA_QK_DIM) < half
    outs = []
    for h in range(u.shape[1] // LANES):
        xh = xn[:, h * LANES:(h + 1) * LANES]
        up = pltpu.roll(xh, LANES - half, 1)
        dn = pltpu.roll(xh, half, 1)
        outs.append(xh * cos + jnp.where(first_half, up, dn) * sin)
    return jnp.concatenate(outs, axis=1)


def _chunk_scan(x, op, identity):
    pos = lax.broadcasted_iota(jnp.int32, x.shape, 1) & (ML_CHUNK - 1)
    d = 1
    while d < ML_CHUNK:
        x = op(x, jnp.where(pos >= d, pltpu.roll(x, d, 1), identity))
        d *= 2
    return x


def _inproj_kernel(x_ref, mod_ref, g_ref, wq_ref, wk_ref, wv_ref, wmqk_ref, wmv_ref, wmo_ref, wif_ref,
                   gq_ref, gk_ref, rope_ref, bd_ref, cw_ref, cb_ref, bif_ref,
                   qkv_out, mqko_out, mv_out, gate_out, halo_ref):
    tm = x_ref.shape[1]
    si = pl.program_id(1)
    mod = mod_ref[0]
    nh = ML_HEADS
    nt = (((1,), (1,)), ((), ()))
    bd = bd_ref[...]
    cw = cw_ref[...]

    tails = jnp.where(si == 0, 0.0, halo_ref[...])
    tails = [tails[:, ci * CONV_COLS:(ci + 1) * CONV_COLS] for ci in range(2 * ML_WIDTH // CONV_COLS)]

    for r0 in range(0, tm, PROJ_SUB):
        rows = slice(r0, r0 + PROJ_SUB)
        hb = _mod_norm(x_ref[0, rows, :], g_ref[...], mod[0:1], mod[1:2]).astype(BF16)

        pre = lax.dot_general(wif_ref[...], hb, nt, preferred_element_type=F32)[0:2 * nh] + bif_ref[...]
        zf = pre[nh:2 * nh]
        log_f = jnp.minimum(zf, 0.0) - jnp.log1p(jnp.exp(-jnp.abs(zf)))
        b = _chunk_scan(log_f, jnp.add, 0.0)
        c = pre[0:nh] - b
        gate_out[0, :, rows] = jnp.concatenate(
            [c, b, _chunk_scan(c, jnp.maximum, -jnp.inf), jnp.zeros_like(c)], axis=0)

        cos, sin = rope_ref[rows, 0:LANES], rope_ref[rows, LANES:2 * LANES]

        def conv_chunk(ci):
            cols = slice(ci * CONV_COLS, (ci + 1) * CONV_COLS)
            u = jnp.dot(hb, wmqk_ref[:, cols], preferred_element_type=F32)
            ext = jnp.concatenate([tails[ci], u], axis=0)
            tails[ci] = u[PROJ_SUB - CONV_HALO:, :]
            prev = pltpu.roll(ext, 1, 0)
            near = ext * cw[3:4, cols] + prev * cw[2:3, cols]
            far = ext * cw[1:2, cols] + prev * cw[0:1, cols]
            acc = cb_ref[:, cols] + near[CONV_HALO:, :] + pltpu.roll(far, 2, 0)[CONV_HALO:, :]
            act = acc * _sigmoid(acc)
            if ci < ML_WIDTH // CONV_COLS:
                act = act * (ML_DIM ** -0.5)
            mqko_out[0, rows, cols] = act.astype(BF16)

        chunks = iter(range(2 * ML_WIDTH // CONV_COLS))
        per_slot = 2 * ML_WIDTH // CONV_COLS // 4
        uq = jnp.dot(hb, wq_ref[...], preferred_element_type=F32)
        for _ in range(per_slot):
            conv_chunk(next(chunks))
        qkv_out[0, rows, 0:DA_WIDTH] = _group_norm_rope(uq, gq_ref[...], bd, cos, sin).astype(BF16)
        uk = jnp.dot(hb, wk_ref[...], preferred_element_type=F32)
        for _ in range(per_slot):
            conv_chunk(next(chunks))
        qkv_out[0, rows, DA_WIDTH:2 * DA_WIDTH] = _group_norm_rope(uk, gk_ref[...], bd, cos, sin).astype(BF16)
        uv = jnp.dot(hb, wv_ref[...], preferred_element_type=F32)
        umv_t = lax.dot_general(wmv_ref[...], hb, nt, preferred_element_type=F32)
        for _ in range(per_slot):
            conv_chunk(next(chunks))
        qkv_out[0, rows, 2 * DA_WIDTH:3 * DA_WIDTH] = uv.astype(BF16)
        mv_out[0, :, rows] = umv_t.astype(BF16)
        umo = jnp.dot(hb, wmo_ref[...], preferred_element_type=F32)
        for _ in range(per_slot):
            conv_chunk(next(chunks))
        mqko_out[0, rows, 2 * ML_WIDTH:3 * ML_WIDTH] = _sigmoid(umo).astype(BF16)

    halo_ref[...] = jnp.concatenate(tails, axis=1)


def _inproj(x, mod3, g, wq, wk, wv, wmqk, wmv, wmo, wif, gq, gk, rope, bd, cw, cb, bif):
    bsz, s, d = x.shape
    tm = PROJ_TM
    tok = lambda width, dt: jax.ShapeDtypeStruct((bsz, s, width), dt)
    tok_spec = lambda width: pl.BlockSpec((1, tm, width), lambda b, i: (b, i, 0))
    return pl.pallas_call(
        _inproj_kernel,
        grid=(bsz, s // tm),
        in_specs=[tok_spec(d),
                  pl.BlockSpec((1, 3, d), lambda b, i: (b, 0, 0)),
                  _const_spec((1, d)),
                  _const_spec(wq.shape), _const_spec(wk.shape), _const_spec(wv.shape),
                  _const_spec(wmqk.shape), _const_spec(wmv.shape), _const_spec(wmo.shape),
                  _const_spec(wif.shape),
                  _const_spec(gq.shape), _const_spec(gk.shape),
                  pl.BlockSpec((tm, 2 * LANES), lambda b, i: (i, 0)),
                  _const_spec(bd.shape), _const_spec(cw.shape), _const_spec(cb.shape),
                  _const_spec(bif.shape)],
        out_specs=[tok_spec(3 * DA_WIDTH), tok_spec(3 * ML_WIDTH),
                   pl.BlockSpec((1, ML_WIDTH, tm), lambda b, i: (b, 0, i)),
                   pl.BlockSpec((1, GATE_ROWS, tm), lambda b, i: (b, 0, i))],
        out_shape=[tok(3 * DA_WIDTH, BF16), tok(3 * ML_WIDTH, BF16),
                   jax.ShapeDtypeStruct((bsz, ML_WIDTH, s), BF16),
                   jax.ShapeDtypeStruct((bsz, GATE_ROWS, s), F32)],
        scratch_shapes=[pltpu.VMEM((CONV_HALO, 2 * ML_WIDTH), F32)],
        compiler_params=pltpu.CompilerParams(
            dimension_semantics=("arbitrary", "arbitrary"), vmem_limit_bytes=VMEM_LIMIT),
        name="in_proj",
    )(x, mod3, g.reshape(1, d), wq, wk, wv, wmqk, wmv, wmo, wif, gq, gk, rope, bd, cw, cb, bif)


def _attn_kernel(bounded_ref, q_ref, k_ref, v_ref, lam_ref, gout_ref, o_ref, qs_ref, m_ref, acc_ref, kt_ref):
    t = ATT_T
    u = t // 2
    lane = lax.broadcasted_iota(jnp.int32, (u, DA_V_DIM), 1)
    ones_cols = jnp.ones((t, DA_V_DIM), BF16)
    lv = lam_ref[...]
    lam = (jnp.exp(jnp.sum(lv[0:1] * lv[1:2], axis=1, keepdims=True))
           - jnp.exp(jnp.sum(lv[2:3] * lv[3:4], axis=1, keepdims=True)) + LAMBDA_INIT)

    for kb in range(q_ref.shape[1] // t):
        kt_ref[kb] = k_ref[0, kb * t:(kb + 1) * t, :].T

    def query_rows(qi, half, part):
        return pl.ds(pl.multiple_of(qi * (2 * t) + half * t + part * u, u), u)

    def stack_queries(qi):
        for half in range(2):
            for part in range(2):
                q = q_ref[0, query_rows(qi, half, part), :]
                zero = jnp.zeros_like(q)
                qs_ref[half, (2 * part) * u:(2 * part + 1) * u, :] = jnp.where(lane < DA_QK_DIM, q, zero)
                qs_ref[half, (2 * part + 1) * u:(2 * part + 2) * u, :] = jnp.where(lane >= DA_QK_DIM, q, zero)

    def keys(kb):
        start = pl.multiple_of(kb * t, t)
        return kt_ref[kb], jnp.concatenate([v_ref[0, pl.ds(start, t), :], ones_cols], axis=1)

    def piece(half, rows, ks, vs, causal_shift, first, stabilised):
        s = jnp.dot(qs_ref[half, rows, :], ks, preferred_element_type=F32)
        if causal_shift is not None:
            row = lax.broadcasted_iota(jnp.int32, s.shape, 0) & (u - 1)
            col = lax.broadcasted_iota(jnp.int32, s.shape, 1)
            s = jnp.where(col <= row + causal_shift, s, -jnp.inf)
        if stabilised:
            m_new = jnp.max(s, axis=1, keepdims=True)
            if not first:
                m_prev = m_ref[half, rows, :]
                m_new = jnp.maximum(m_prev, m_new)
                alpha = jnp.exp2(m_prev - m_new)
            m_ref[half, rows, :] = m_new
            s = s - m_new
        pv = jnp.dot(jnp.exp2(s).astype(BF16), vs, preferred_element_type=F32)
        if first:
            acc_ref[half, rows, :] = pv
        elif stabilised:
            acc_ref[half, rows, :] = alpha * acc_ref[half, rows, :] + pv
        else:
            acc_ref[half, rows, :] += pv

    def block(half, kv, diagonal, first, stabilised):
        ks, vs = kv
        if diagonal:
            piece(half, slice(0, 2 * u), ks[:, 0:u], vs[0:u], 0, first, stabilised)
            piece(half, slice(2 * u, 4 * u), ks, vs, u, first, stabilised)
        else:
            piece(half, slice(0, 4 * u), ks, vs, None, first, stabilised)

    def tile_pair(qi, stabilised):
        stack_queries(qi)
        kv = keys(2 * qi)
        block(0, kv, True, True, stabilised)
        block(1, kv, False, True, stabilised)
        block(1, keys(2 * qi + 1), True, False, stabilised)

        def body(pair, carry):
            for j in range(2):
                kv = keys(2 * pair + j)
                block(0, kv, False, False, stabilised)
                block(1, kv, False, False, stabilised)
            return carry

        lax.fori_loop(0, qi, body, 0)

        for half in range(2):
            for part in range(2):
                rows = slice(2 * part * u, (2 * part + 2) * u)
                o = acc_ref[half, rows, 0:DA_V_DIM] / acc_ref[half, rows, DA_V_DIM:2 * DA_V_DIM]
                od = o[0:u] - lam * o[u:2 * u]
                ms = jnp.mean(od * od, axis=-1, keepdims=True)
                o_ref[0, query_rows(qi, half, part), :] = (
                    (od * lax.rsqrt(ms + EPS)) * gout_ref[...] * (1.0 - LAMBDA_INIT)).astype(BF16)

    def run(stabilised):
        def body(qi, carry):
            tile_pair(qi, stabilised)
            return carry

        lax.fori_loop(0, q_ref.shape[1] // (2 * t), body, 0)

    @pl.when(bounded_ref[0] != 0)
    def _():
        run(False)

    @pl.when(bounded_ref[0] == 0)
    def _():
        run(True)


def _attention(bounded, qkv, lam_vecs, g_out):
    bsz, s, _ = qkv.shape
    t = ATT_T
    full = pl.BlockSpec((1, s, DA_V_DIM), lambda b, h: (b, 0, h))
    part = lambda j: pl.BlockSpec((1, s, DA_V_DIM), lambda b, h: (b, 0, j * DA_HEADS + h))
    return pl.pallas_call(
        _attn_kernel,
        grid=(bsz, DA_HEADS),
        in_specs=[pl.BlockSpec(memory_space=pltpu.SMEM),
                  part(0), part(1), part(2), _const_spec(lam_vecs.shape), _const_spec((1, DA_V_DIM))],
        out_specs=full,
        out_shape=jax.ShapeDtypeStruct((bsz, s, DA_WIDTH), BF16),
        scratch_shapes=[pltpu.VMEM((2, 2 * t, DA_V_DIM), BF16),
                        pltpu.VMEM((2, 2 * t, 1), F32),
                        pltpu.VMEM((2, 2 * t, 2 * DA_V_DIM), F32),
                        pltpu.VMEM((s // t, DA_V_DIM, t), BF16)],
        compiler_params=pltpu.CompilerParams(
            dimension_semantics=("arbitrary", "arbitrary"), vmem_limit_bytes=VMEM_LIMIT),
        name="diff_attention",
    )(bounded, qkv, qkv, qkv, lam_vecs, g_out.reshape(1, DA_V_DIM))


def _mlstm_kernel(q_ref, k_ref, vt_ref, o_ref, gr_ref, gout_ref, y_ref, ct_ref, m_ref):
    ln = ML_CHUNK
    nh = ML_HEADS
    nt = (((1,), (1,)), ((), ()))

    @pl.when(pl.program_id(1) == 0)
    def _():
        ct_ref[...] = jnp.zeros(ct_ref.shape, F32)
        m_ref[...] = jnp.zeros(m_ref.shape, F32)

    row = lax.broadcasted_iota(jnp.int32, (ln, ln), 0)
    col = lax.broadcasted_iota(jnp.int32, (ln, ln), 1)
    lower = col <= row
    ones_row = (lax.broadcasted_iota(jnp.int32, (ML_STATE_ROWS - ML_DIM, ln), 0) == 0).astype(BF16)

    for r in range(ML_ROWS):
        gr = gr_ref[r]
        c, b = gr[0:nh], gr[nh:2 * nh]
        m_prev = jnp.concatenate([m_ref[r, 0:nh, :]] * (ln // LANES), axis=1)
        m_run = jnp.maximum(gr[2 * nh:3 * nh], m_prev)
        m_last = m_run[:, ln - 1:ln]
        e_inter = jnp.exp(m_prev - m_run)
        e_floor = jnp.exp(-(b + m_run))
        w_in = jnp.exp(c - m_last)
        decay = e_inter[:, ln - 1:ln]
        m_ref[r, 0:nh, :] = jnp.broadcast_to(b[:, ln - 1:ln] + m_last, (nh, LANES))
        m_run_cols = jnp.concatenate([m_run, m_run], axis=0).T

        for h in range(nh):
            lanes = slice(h * ML_DIM, (h + 1) * ML_DIM)
            q = q_ref[r, :, lanes]
            k = k_ref[r, :, lanes]
            vt_aug = jnp.concatenate([vt_ref[r, lanes, :], ones_row], axis=0)
            s = lax.dot_general(q, k, nt, preferred_element_type=F32)
            gate = jnp.exp(jnp.where(lower, c[h:h + 1, :] - m_run_cols[:, h:h + 1], -jnp.inf))
            w = (s * gate).astype(BF16)
            state_t = ct_ref[r, h]
            full_t = (e_inter[h:h + 1, :]
                      * lax.dot_general(state_t.astype(BF16), q, nt, preferred_element_type=F32)
                      + lax.dot_general(vt_aug, w, nt, preferred_element_type=F32))
            den = full_t[ML_DIM:ML_DIM + 1, :]
            hh_t = full_t[0:ML_DIM, :] / jnp.maximum(jnp.abs(den), e_floor[h:h + 1, :])
            hn_t = hh_t * lax.rsqrt(jnp.mean(hh_t * hh_t, axis=0, keepdims=True) + EPS)
            y_ref[r, :, lanes] = (o_ref[r, :, lanes].astype(F32) * (hn_t.T * gout_ref[:, lanes])).astype(BF16)

            vw = (vt_aug.astype(F32) * w_in[h:h + 1, :]).astype(BF16)
            ct_ref[r, h] = decay[h:h + 1, :] * state_t + jnp.dot(vw, k, preferred_element_type=F32)


def _mlstm(qko, vt, gates_row, g_out):
    bsz, s, _ = qko.shape
    ln = ML_CHUNK
    rows = ML_ROWS
    tok = pl.BlockSpec((rows, ln, ML_WIDTH), lambda b, c: (b, c, 0))
    part = lambda j: pl.BlockSpec((rows, ln, ML_WIDTH), lambda b, c: (b, c, j))
    return pl.pallas_call(
        _mlstm_kernel,
        grid=(bsz // rows, s // ln),
        in_specs=[part(0), part(1), pl.BlockSpec((rows, ML_WIDTH, ln), lambda b, c: (b, 0, c)), part(2),
                  pl.BlockSpec((rows, GATE_ROWS, ln), lambda b, c: (b, 0, c)),
                  _const_spec((1, ML_WIDTH))],
        out_specs=tok,
        out_shape=jax.ShapeDtypeStruct((bsz, s, ML_WIDTH), BF16),
        scratch_shapes=[pltpu.VMEM((rows, ML_HEADS, ML_STATE_ROWS, ML_DIM), F32),
                        pltpu.VMEM((rows, F32_SUBLANES, LANES), F32)],
        compiler_params=pltpu.CompilerParams(
            dimension_semantics=("arbitrary", "arbitrary"), vmem_limit_bytes=VMEM_LIMIT),
        name="mlstm",
    )(qko, qko, vt, qko, gates_row, g_out.reshape(1, ML_WIDTH))


def _rope_tables(s):
    half = ROPE_DIM // 2
    pos = jnp.arange(s, dtype=F32)
    inv_freq = ROPE_THETA ** (-jnp.arange(0, ROPE_DIM, 2, dtype=F32) / ROPE_DIM)
    ang = pos[:, None] * inv_freq[None, :]
    basis = jnp.concatenate([jnp.cos(ang), jnp.sin(ang), jnp.ones((s, 1), F32)], axis=1)
    g = jnp.arange(LANES) % DA_QK_DIM
    j = jnp.arange(2 * half + 1)[:, None]
    cos_sel = jnp.where(g < ROPE_DIM, j == g % half, j == 2 * half).astype(F32)
    sin_sel = ((g >= half) & (g < ROPE_DIM) & (j == g)).astype(F32) - ((g < half) & (j == half + g)).astype(F32)
    sel = jnp.concatenate([cos_sel, sin_sel], axis=1)
    return jnp.dot(basis, sel, precision=lax.Precision.HIGHEST)


def kernel(x, c, w_ada, b_ada, g_norm, ffn1_w12, ffn1_w3, w_in, conv_w, conv_b, b_igate, b_fgate,
           g_qnorm, g_knorm, lambda_qk, g_da_out, g_ml_out, w_out, ffn2_w12, ffn2_w3):
    bsz, s, d = x.shape
    assert w_ada.shape[0] == 1, "one layer (depth 1) is implemented"
    l = 0
    mod = _ada(c, w_ada[l], b_ada[l]).reshape(bsz, 3, 3, d)

    x, w12_2, w3_2, wo = _ffn(
        x, mod[:, 0], g_norm[l, 0], ffn1_w12[l].astype(BF16), ffn1_w3[l].astype(BF16),
        casts=((ffn2_w12, l), (ffn2_w3, l), (w_out, l)))
    wb = w_in[l].astype(BF16)
    o0 = 0
    parts = []
    for width in (DA_WIDTH, DA_WIDTH, DA_WIDTH, 2 * ML_WIDTH, ML_WIDTH, ML_WIDTH, 2 * ML_HEADS):
        parts.append(wb[:, o0:o0 + width])
        o0 += width
    wq, wk, wv, wmqk, wmv, wmo, wif = parts
    wif = jnp.pad(wif.T, ((0, BF16_SUBLANES - 2 * ML_HEADS), (0, 0)))
    wmv = wmv.T
    groups = DA_WIDTH // DA_QK_DIM
    q_gain = g_qnorm[l] * (DA_QK_DIM ** -0.5)
    gq = jnp.tile(q_gain * LOG2E, groups).reshape(1, DA_WIDTH)
    score_bound = BF16_ROUNDING_SLACK * DA_QK_DIM * jnp.max(jnp.abs(q_gain)) * jnp.max(jnp.abs(g_knorm[l]))
    bounded = (score_bound <= SAFE_SCORE_BOUND).astype(jnp.int32).reshape(1)
    gk = jnp.tile(g_knorm[l], groups).reshape(1, DA_WIDTH)
    rope = _rope_tables(s)
    gid = jnp.arange(MXU_TILE) // DA_QK_DIM
    bd = (gid[:, None] == gid[None, :]).astype(BF16)
    bif = jnp.concatenate([b_igate[l], b_fgate[l]]).reshape(2 * ML_HEADS, 1)
    da_qkv, ml_qko, ml_vt, gates = _inproj(
        x, mod[:, 1], g_norm[l, 1], wq, wk, wv, wmqk, wmv, wmo, wif, gq, gk, rope, bd,
        conv_w[l], conv_b[l].reshape(1, 2 * ML_WIDTH), bif)

    y_da = _attention(bounded, da_qkv, lambda_qk[l], g_da_out[l])
    y_ml = _mlstm(ml_qko, ml_vt, gates, g_ml_out[l])

    return _ffn(x, mod[:, 2], g_norm[l, 2], w12_2, w3_2,
                mix=(mod[:, 1], y_da, y_ml, wo[:DA_WIDTH], wo[DA_WIDTH:]))
```
